```python
import math
import jax, jax.numpy as jnp
from jax import lax
import numpy as np

D_MODEL = 1024
BATCH = 1
SEQ = 16384
DEPTH = 1

N_HEADS = 8
N_KV_HEADS = 2
HEAD_DIM = 64
Q_PER_KV = N_HEADS // N_KV_HEADS
ROT_DIM = HEAD_DIM // 4
ROPE_THETA = 500000.0
CMP_LEN = 32
CMP_STRIDE = 16
CMP_HIDDEN = 256
SLC_LEN = 64
SLC_TOPK = 16
WINDOW = 512
Q_BLOCK = 128
ATTN_WIDTH = N_HEADS * HEAD_DIM
KV_WIDTH = N_KV_HEADS * HEAD_DIM
SSM_WIDTH = 512
SSM_GROUP = 16
SSM_GROUPS = SSM_WIDTH // SSM_GROUP
SSM_STATE = 64
SSM_CHUNK = 128
D_FF = 2816
CONV_WIDTH = 3
EPS = 1e-6
NEG_INF = -1e30
FORCE_SCORE = 1e4

IN_SIZES = (ATTN_WIDTH, KV_WIDTH, KV_WIDTH, KV_WIDTH, KV_WIDTH, KV_WIDTH, KV_WIDTH,
            3 * N_HEADS, SSM_WIDTH, D_MODEL, D_MODEL)
IN_OFFSETS = tuple(int(v) for v in np.cumsum(IN_SIZES)[:-1])
D_IN = int(sum(IN_SIZES))

kernel_name = "hybrid_nsa_s5_convffn_block"


def rmsnorm(x, g):
    xf = x.astype(jnp.float32)
    r = xf * lax.rsqrt(jnp.mean(xf * xf, axis=-1, keepdims=True) + EPS)
    return (r * g.astype(jnp.float32)).astype(x.dtype)


def partial_rope(x, pos):
    half = ROT_DIM // 2
    inv_freq = ROPE_THETA ** (-jnp.arange(0, ROT_DIM, 2, dtype=jnp.float32) / ROT_DIM)
    ang = pos.astype(jnp.float32)[..., None] * inv_freq
    cos = jnp.cos(ang)[:, :, None, :].astype(x.dtype)
    sin = jnp.sin(ang)[:, :, None, :].astype(x.dtype)
    x1, x2, rest = x[..., :half], x[..., half:ROT_DIM], x[..., ROT_DIM:]
    return jnp.concatenate([x1 * cos - x2 * sin, x2 * cos + x1 * sin, rest], axis=-1)


def masked_softmax(s, mask, axis):
    s = jnp.where(mask, s.astype(jnp.float32), NEG_INF)
    m = jnp.max(s, axis=axis, keepdims=True)
    e = jnp.where(mask, jnp.exp(s - m), 0.0)
    return e / jnp.maximum(jnp.sum(e, axis=axis, keepdims=True), 1e-30)


def cmp_block_ends(seq):
    n_cmp = (seq - CMP_LEN) // CMP_STRIDE + 1
    return jnp.arange(n_cmp, dtype=jnp.int32) * CMP_STRIDE + CMP_LEN - 1, n_cmp


def compress(kraw, pe, w1, w2, n_cmp):
    idx = jnp.arange(n_cmp)[:, None] * CMP_STRIDE + jnp.arange(CMP_LEN)[None, :]
    blocks = kraw[:, idx] + pe[None, None, :, None, :]
    b, n, l, kh, d = blocks.shape
    flat = blocks.transpose(0, 1, 3, 2, 4).reshape(b, n, kh, l * d)
    return jax.nn.gelu(flat @ w1) @ w2


def nsa_attention(q, k_cmp, v_cmp, k_slc, v_slc, k_win, v_win, gates, cmp_end):
    bsz, seq = q.shape[0], q.shape[1]
    nb = seq // Q_BLOCK
    n_slc = seq // SLC_LEN
    topk = min(SLC_TOPK, n_slc)
    c_start = cmp_end - (CMP_LEN - 1)
    s_start = jnp.arange(n_slc, dtype=jnp.int32) * SLC_LEN
    ov = jnp.minimum(c_start[:, None] + CMP_LEN, s_start[None, :] + SLC_LEN) - jnp.maximum(c_start[:, None], s_start[None, :])
    overlap = jnp.clip(ov, 0, None).astype(jnp.float32) / CMP_LEN
    ks_blocks = k_slc.reshape(bsz, n_slc, SLC_LEN, N_KV_HEADS, HEAD_DIM).transpose(0, 3, 1, 2, 4)
    vs_blocks = v_slc.reshape(bsz, n_slc, SLC_LEN, N_KV_HEADS, HEAD_DIM).transpose(0, 3, 1, 2, 4)
    pad = ((0, 0), (WINDOW, 0), (0, 0), (0, 0))
    kw_p, vw_p = jnp.pad(k_win, pad), jnp.pad(v_win, pad)
    q_blocks = q.reshape(bsz, nb, Q_BLOCK, N_KV_HEADS, Q_PER_KV, HEAD_DIM).transpose(1, 0, 2, 3, 4, 5)
    g_blocks = gates.reshape(bsz, nb, Q_BLOCK, N_KV_HEADS, Q_PER_KV, 3).transpose(1, 0, 2, 3, 4, 5)
    b_idx = jnp.arange(bsz)[:, None, None, None]
    h_idx = jnp.arange(N_KV_HEADS)[None, :, None, None]
    j_idx = jnp.arange(n_slc, dtype=jnp.int32)

    def block(args):
        c, qb, gb = args
        t = c * Q_BLOCK + jnp.arange(Q_BLOCK, dtype=jnp.int32)
        s_c = jnp.einsum('bqkgd,bnkd->bkgqn', qb, k_cmp)
        mask_c = (cmp_end[None, :] <= t[:, None])[None, None, None]
        p_c = masked_softmax(s_c, mask_c, axis=-1)
        o_c = jnp.einsum('bkgqn,bnkd->bqkgd', p_c.astype(v_cmp.dtype), v_cmp)
        imp = jnp.einsum('bkgqn,nj->bqkj', p_c, overlap)
        forced = (j_idx[None, :] == 0) | (j_idx[None, :] == (t // SLC_LEN)[:, None])
        valid = (j_idx[None, :] * SLC_LEN) <= t[:, None]
        score = jnp.where(forced[None, :, None, :], FORCE_SCORE, imp)
        score = jnp.where(valid[None, :, None, :], score, NEG_INF)
        top_val, top_idx = lax.top_k(score, topk)
        sel_ok = (top_val > 0.5 * NEG_INF).transpose(0, 2, 1, 3)
        top_idx_t = top_idx.transpose(0, 2, 1, 3)
        ks = ks_blocks[b_idx, h_idx, top_idx_t]
        vs = vs_blocks[b_idx, h_idx, top_idx_t]
        s_s = jnp.einsum('bkgqd,bkqnsd->bkgqns', qb.transpose(0, 2, 3, 1, 4), ks)
        tok = top_idx_t[..., None] * SLC_LEN + jnp.arange(SLC_LEN, dtype=jnp.int32)
        mask_s = (sel_ok[..., None] & (tok <= t[None, None, :, None, None]))[:, :, None]
        p_s = masked_softmax(s_s, mask_s, axis=(-2, -1))
        o_s = jnp.einsum('bkgqns,bkqnsd->bqkgd', p_s.astype(vs.dtype), vs)
        kw = lax.dynamic_slice_in_dim(kw_p, c * Q_BLOCK, Q_BLOCK + WINDOW, axis=1)
        vw = lax.dynamic_slice_in_dim(vw_p, c * Q_BLOCK, Q_BLOCK + WINDOW, axis=1)
        s_w = jnp.einsum('bqkgd,bjkd->bkgqj', qb, kw)
        kpos = c * Q_BLOCK - WINDOW + jnp.arange(Q_BLOCK + WINDOW, dtype=jnp.int32)
        diff = t[:, None] - kpos[None, :]
        mask_w = ((kpos[None, :] >= 0) & (diff >= 0) & (diff < WINDOW))[None, None, None]
        p_w = masked_softmax(s_w, mask_w, axis=-1)
        o_w = jnp.einsum('bkgqj,bjkd->bqkgd', p_w.astype(vw.dtype), vw)
        o = gb[..., 0:1] * o_c + gb[..., 1:2] * o_s + gb[..., 2:3] * o_w
        return o.reshape(bsz, Q_BLOCK, ATTN_WIDTH).astype(q.dtype)

    out = lax.map(block, (jnp.arange(nb, dtype=jnp.int32), q_blocks, g_blocks))
    return out.transpose(1, 0, 2, 3).reshape(bsz, seq, ATTN_WIDTH)


def s5_layer(u, a_re, a_im, b_re, b_im, c_re, c_im, d_skip, log_dt, w_glu):
    bsz, seq = u.shape[0], u.shape[1]
    nc = seq // SSM_CHUNK
    f32 = jnp.float32
    A = lax.complex(a_re.astype(f32), a_im.astype(f32))
    dt = jnp.exp(log_dt.astype(f32))[:, None]
    A_bar = jnp.exp(A * dt)
    B_bar = ((A_bar - 1.0) / A)[..., None] * lax.complex(b_re.astype(f32), b_im.astype(f32))
    Cm = lax.complex(c_re.astype(f32), c_im.astype(f32))
    uf = u.astype(f32)
    uc = uf.reshape(bsz, nc, SSM_CHUNK, SSM_GROUPS, SSM_GROUP).transpose(1, 2, 0, 3, 4)
    a_chunk = jnp.broadcast_to(A_bar, (SSM_CHUNK, bsz, SSM_GROUPS, SSM_STATE))

    def op(e1, e2):
        a1, b1 = e1
        a2, b2 = e2
        return a1 * a2, a2 * b1 + b2

    def step(carry, u_c):
        bu = jnp.einsum('cbgh,gph->cbgp', u_c.astype(jnp.complex64), B_bar)
        a_cum, x_loc = lax.associative_scan(op, (a_chunk, bu), axis=0)
        xs = x_loc + a_cum * carry[None]
        y = jnp.real(jnp.einsum('ghp,cbgp->cbgh', Cm, xs))
        return xs[-1], y

    carry0 = jnp.zeros((bsz, SSM_GROUPS, SSM_STATE), jnp.complex64)
    _, ys = lax.scan(step, carry0, uc)
    y = ys.transpose(2, 0, 1, 3, 4).reshape(bsz, seq, SSM_WIDTH) + d_skip.astype(f32) * uf
    y = jax.nn.gelu(y)
    y = y * jax.nn.sigmoid(y @ w_glu.astype(f32))
    return y.astype(u.dtype)


def causal_dwconv(x, w, b):
    seq = x.shape[1]
    xp = jnp.pad(x, ((0, 0), (CONV_WIDTH - 1, 0), (0, 0)))
    out = b
    for k in range(CONV_WIDTH):
        out = out + w[k] * xp[:, k:k + seq]
    return out


def setup_inputs(seed: int = 0) -> dict:
    key = jax.random.key(seed)
    ks = jax.random.split(key, 32)
    nrm = lambda k, shape, s: jax.random.normal(k, shape, jnp.float32) * s
    L = DEPTH
    n_idx = jnp.arange(SSM_STATE, dtype=jnp.float32)
    return {
        "x": nrm(ks[0], (BATCH, SEQ, D_MODEL), 1.0),
        "positions": jnp.broadcast_to(jnp.arange(SEQ, dtype=jnp.int32), (BATCH, SEQ)),
        "norm_mix": 1.0 + nrm(ks[1], (L, D_MODEL), 0.02),
        "w_in": nrm(ks[2], (L, D_MODEL, D_IN), D_MODEL ** -0.5),
        "cmp_k_pe": nrm(ks[3], (L, CMP_LEN, HEAD_DIM), 0.02),
        "cmp_k_w1": nrm(ks[4], (L, CMP_LEN * HEAD_DIM, CMP_HIDDEN), (CMP_LEN * HEAD_DIM) ** -0.5),
        "cmp_k_w2": nrm(ks[5], (L, CMP_HIDDEN, HEAD_DIM), CMP_HIDDEN ** -0.5),
        "cmp_v_pe": nrm(ks[6], (L, CMP_LEN, HEAD_DIM), 0.02),
        "cmp_v_w1": nrm(ks[7], (L, CMP_LEN * HEAD_DIM, CMP_HIDDEN), (CMP_LEN * HEAD_DIM) ** -0.5),
        "cmp_v_w2": nrm(ks[8], (L, CMP_HIDDEN, HEAD_DIM), CMP_HIDDEN ** -0.5),
        "ssm_a_re": -0.5 + nrm(ks[9], (L, SSM_GROUPS, SSM_STATE), 0.01),
        "ssm_a_im": math.pi * n_idx + nrm(ks[10], (L, SSM_GROUPS, SSM_STATE), 0.01),
        "ssm_b_re": nrm(ks[11], (L, SSM_GROUPS, SSM_STATE, SSM_GROUP), (2.0 * SSM_GROUP) ** -0.5),
        "ssm_b_im": nrm(ks[12], (L, SSM_GROUPS, SSM_STATE, SSM_GROUP), (2.0 * SSM_GROUP) ** -0.5),
        "ssm_c_re": nrm(ks[13], (L, SSM_GROUPS, SSM_GROUP, SSM_STATE), (2.0 * SSM_STATE) ** -0.5),
        "ssm_c_im": nrm(ks[14], (L, SSM_GROUPS, SSM_GROUP, SSM_STATE), (2.0 * SSM_STATE) ** -0.5),
        "ssm_d": nrm(ks[15], (L, SSM_WIDTH), 0.5),
        "ssm_log_dt": jax.random.uniform(ks[16], (L, SSM_GROUPS), jnp.float32, math.log(1e-3), math.log(1e-1)),
        "ssm_w_glu": nrm(ks[17], (L, SSM_WIDTH, SSM_WIDTH), SSM_WIDTH ** -0.5),
        "w_proj_a": nrm(ks[18], (L, ATTN_WIDTH, D_MODEL), ATTN_WIDTH ** -0.5),
        "w_proj_b": nrm(ks[19], (L, SSM_WIDTH, D_MODEL), SSM_WIDTH ** -0.5),
        "w_out": nrm(ks[20], (L, D_MODEL, D_MODEL), D_MODEL ** -0.5),
        "norm_ffn": 1.0 + nrm(ks[21], (L, D_MODEL), 0.02),
        "w_up": nrm(ks[22], (L, D_MODEL, 2 * D_FF), D_MODEL ** -0.5),
        "conv_w": nrm(ks[23], (L, CONV_WIDTH, 2 * D_FF), CONV_WIDTH ** -0.5),
        "conv_b": nrm(ks[24], (L, 2 * D_FF), 0.01),
        "w_down": nrm(ks[25], (L, D_FF, D_MODEL), D_FF ** -0.5),
        "norm_final": 1.0 + nrm(ks[26], (D_MODEL,), 0.02),
    }


def reference(x, positions, norm_mix, w_in, cmp_k_pe, cmp_k_w1, cmp_k_w2, cmp_v_pe, cmp_v_w1, cmp_v_w2,
              ssm_a_re, ssm_a_im, ssm_b_re, ssm_b_im, ssm_c_re, ssm_c_im, ssm_d, ssm_log_dt, ssm_w_glu,
              w_proj_a, w_proj_b, w_out, norm_ffn, w_up, conv_w, conv_b, w_down, norm_final):
    bsz, seq = x.shape[0], x.shape[1]
    cmp_end, n_cmp = cmp_block_ends(seq)
    cmp_pos = positions[:, cmp_end]
    kv_shape = (bsz, seq, N_KV_HEADS, HEAD_DIM)
    for l in range(DEPTH):
        h = rmsnorm(x, norm_mix[l])
        proj = h @ w_in[l]
        q, kc, vc, ksl, vsl, kwn, vwn, ng, u, ga, gb = jnp.split(proj, IN_OFFSETS, axis=-1)
        q = partial_rope(q.reshape(bsz, seq, N_HEADS, HEAD_DIM), positions) * (HEAD_DIM ** -0.5)
        k_cmp = partial_rope(compress(kc.reshape(kv_shape), cmp_k_pe[l], cmp_k_w1[l], cmp_k_w2[l], n_cmp), cmp_pos)
        v_cmp = compress(vc.reshape(kv_shape), cmp_v_pe[l], cmp_v_w1[l], cmp_v_w2[l], n_cmp)
        k_slc = partial_rope(ksl.reshape(kv_shape), positions)
        k_win = partial_rope(kwn.reshape(kv_shape), positions)
        nsa_gates = jax.nn.sigmoid(ng.reshape(bsz, seq, N_HEADS, 3))
        y_a = nsa_attention(q, k_cmp, v_cmp, k_slc, vsl.reshape(kv_shape), k_win, vwn.reshape(kv_shape),
                            nsa_gates, cmp_end)
        y_b = s5_layer(u, ssm_a_re[l], ssm_a_im[l], ssm_b_re[l], ssm_b_im[l], ssm_c_re[l], ssm_c_im[l],
                       ssm_d[l], ssm_log_dt[l], ssm_w_glu[l])
        merged = jax.nn.sigmoid(ga) * (y_a @ w_proj_a[l]) + jax.nn.sigmoid(gb) * (y_b @ w_proj_b[l])
        x = x + merged @ w_out[l]
        h2 = rmsnorm(x, norm_ffn[l])
        up = causal_dwconv(h2 @ w_up[l], conv_w[l], conv_b[l])
        gate, val = jnp.split(up, 2, axis=-1)
        x = x + (jax.nn.silu(gate) * val) @ w_down[l]
    return rmsnorm(x, norm_final)
```

```python
import functools

import jax
import jax.numpy as jnp
from jax import lax
from jax.experimental import pallas as pl
from jax.experimental.pallas import tpu as pltpu

F32 = jnp.float32
BF16 = jnp.bfloat16

N_HEADS = 8
N_KV_HEADS = 2
HEAD_DIM = 64
Q_PER_KV = N_HEADS // N_KV_HEADS
ROT_DIM = HEAD_DIM // 4
ROPE_THETA = 500000.0
CMP_LEN = 32
CMP_STRIDE = 16
CMP_PER_SLC = 4
SLC_LEN = 64
SLC_TOPK = 16
WINDOW = 512
SSM_GROUP = 16
SSM_STATE = 64
CONV_WIDTH = 3
EPS = 1e-6
NEG_INF = -1e30
FORCE_SCORE = 1e4

LANES = 128
SUBLANES = 8
VMEM_LIMIT_BYTES = 56 * 1024 * 1024

ROW_TILE = 512
FFN_ROW_TILE = 256
Q_TILE = 128
KV_TILE = 512
SCAN_SEGMENTS = SUBLANES
SCAN_LANE_BLOCK = 512


def _rmsnorm(x, g):
    return x * lax.rsqrt(jnp.mean(x * x, axis=-1, keepdims=True) + EPS) * g


def _dot(a, b):
    return jnp.dot(a, b, preferred_element_type=F32)


def _dot_nt(a, b):
    return lax.dot_general(a, b, (((1,), (1,)), ((), ())), preferred_element_type=F32)


def _rope_lane_patterns():
    half = ROT_DIM // 2
    d = jnp.arange(LANES) % HEAD_DIM
    inv_freq = ROPE_THETA ** (-jnp.arange(0, ROT_DIM, 2, dtype=F32) / ROT_DIM)
    freq = jnp.where(d < ROT_DIM, inv_freq[d % half], 0.0)
    m_lo = jnp.where(d < half, -1.0, 0.0)
    m_hi = jnp.where((d >= half) & (d < ROT_DIM), 1.0, 0.0)
    return jnp.stack([freq, m_lo, m_hi]).astype(F32)


def _rope_tables(pos_col, pat):
    ang = pos_col.astype(F32) * pat[0:1]
    sin = jnp.sin(ang)
    return jnp.cos(ang), sin * pat[1:2], sin * pat[2:3]


def _rope(x, cos, sin_lo, sin_hi):
    half = ROT_DIM // 2
    return (x * cos + pltpu.roll(x, LANES - half, axis=1) * sin_lo
            + pltpu.roll(x, half, axis=1) * sin_hi)


N_ROPE_COLS = (N_HEADS + 2 * N_KV_HEADS) * HEAD_DIM
KVC_COLS = 2 * N_KV_HEADS * HEAD_DIM


def _inproj_kernel(x_ref, g_ref, pos_ref, pat_ref, w_ref, wv_ref,
                   qk_ref, kvc_ref, u_ref, ng_ref, vt_ref, *, ssm_width):
    h = _rmsnorm(x_ref[...], g_ref[...]).astype(BF16)
    p = _dot(h, w_ref[...])
    cos, sin_lo, sin_hi = _rope_tables(pos_ref[...], pat_ref[...])
    n_q_groups = N_HEADS * HEAD_DIM // LANES
    for gi in range(N_ROPE_COLS // LANES):
        r = _rope(p[:, gi * LANES:(gi + 1) * LANES], cos, sin_lo, sin_hi)
        if gi < n_q_groups:
            r = r * (HEAD_DIM ** -0.5)
        qk_ref[:, gi * LANES:(gi + 1) * LANES] = r.astype(BF16)
    c0 = N_ROPE_COLS
    kvc_ref[...] = p[:, c0:c0 + KVC_COLS]
    c0 += KVC_COLS
    u_ref[...] = p[:, c0:c0 + ssm_width]
    c0 += ssm_width
    ng_ref[...] = p[:, c0:c0 + LANES]
    vt_ref[...] = _dot_nt(wv_ref[...], h).astype(BF16)


def _inproj(x2, g, pos_col, pat, w_main, w_vt, ssm_width):
    seq, d_model = x2.shape
    tm = ROW_TILE
    n_main = w_main.shape[1]
    n_vt = w_vt.shape[0]
    row = lambda i: (i, 0)
    const = lambda i: (0, 0)
    return pl.pallas_call(
        functools.partial(_inproj_kernel, ssm_width=ssm_width),
        grid=(seq // tm,),
        in_specs=[
            pl.BlockSpec((tm, d_model), row),
            pl.BlockSpec((1, d_model), const),
            pl.BlockSpec((tm, 1), row),
            pl.BlockSpec((3, LANES), const),
            pl.BlockSpec((d_model, n_main), const),
            pl.BlockSpec((n_vt, d_model), const),
        ],
        out_specs=[
            pl.BlockSpec((tm, N_ROPE_COLS), row),
            pl.BlockSpec((tm, KVC_COLS), row),
            pl.BlockSpec((tm, ssm_width), row),
            pl.BlockSpec((tm, LANES), row),
            pl.BlockSpec((n_vt, tm), lambda i: (0, i)),
        ],
        out_shape=[
            jax.ShapeDtypeStruct((seq, N_ROPE_COLS), BF16),
            jax.ShapeDtypeStruct((seq, KVC_COLS), F32),
            jax.ShapeDtypeStruct((seq, ssm_width), F32),
            jax.ShapeDtypeStruct((seq, LANES), F32),
            jax.ShapeDtypeStruct((n_vt, seq), BF16),
        ],
        compiler_params=pltpu.CompilerParams(
            dimension_semantics=("arbitrary",), vmem_limit_bytes=VMEM_LIMIT_BYTES),
        name="inproj",
    )(x2, g, pos_col, pat, w_main, w_vt)


def _compress_kernel(h_ref, pe_ref, w1_ref, w2_ref, pos_ref, pat_ref, out_ref):
    hh = h_ref[0]
    rows, half_w = hh.shape
    a = _dot((hh + pe_ref[0, 0:1]).astype(BF16), w1_ref[0, :half_w])
    b = _dot((hh + pe_ref[0, 1:2]).astype(BF16), w1_ref[0, half_w:])
    pre = a + pltpu.roll(b, rows - 1, axis=0)
    z = _dot(jax.nn.gelu(pre).astype(BF16), w2_ref[0])
    cos, sin_lo, sin_hi = _rope_tables(pos_ref[...], pat_ref[0])
    out_ref[0] = _rope(z, cos, sin_lo, sin_hi)


def _compress(h4, pe4, w1, w2, cmp_pos_col, pat2):
    n, rows, half_w = h4.shape
    hidden = w1.shape[2]
    per = n // 2
    return pl.pallas_call(
        _compress_kernel,
        grid=(n,),
        in_specs=[
            pl.BlockSpec((1, rows, half_w), lambda i: (i, 0, 0)),
            pl.BlockSpec((1, 2, half_w), lambda i: (i, 0, 0)),
            pl.BlockSpec((1, 2 * half_w, hidden), lambda i: (i // per, 0, 0)),
            pl.BlockSpec((1, hidden, LANES), lambda i: (i // per, 0, 0)),
            pl.BlockSpec((rows, 1), lambda i: (0, 0)),
            pl.BlockSpec((1, 3, LANES), lambda i: (i // per, 0, 0)),
        ],
        out_specs=pl.BlockSpec((1, rows, LANES), lambda i: (i, 0, 0)),
        out_shape=jax.ShapeDtypeStruct((n, rows, LANES), F32),
        compiler_params=pltpu.CompilerParams(
            dimension_semantics=("arbitrary",), vmem_limit_bytes=VMEM_LIMIT_BYTES),
        name="compress",
    )(h4, pe4, w1, w2, cmp_pos_col, pat2)


def _softmax_cols(s, mask):
    s = jnp.where(mask, s, NEG_INF)
    m = jnp.max(s, axis=0, keepdims=True)
    e = jnp.where(mask, jnp.exp(s - m), 0.0)
    inv = 1.0 / jnp.maximum(jnp.sum(e, axis=0, keepdims=True), 1e-30)
    return e, inv


def _attn_kernel(q_ref, kc_ref, vct_ref, ks_ref, vst_ref, kw_ref, vwt_ref, ng_ref,
                 out_ref, sel_ref, *, n_slc, topk):
    c = pl.program_id(1)
    qw = Q_PER_KV * Q_TILE
    q = q_ref[0, 0]
    t0 = c * Q_TILE
    t_lane = t0 + (lax.broadcasted_iota(jnp.int32, (1, qw), 1) & (Q_TILE - 1))

    n_rows = CMP_PER_SLC * n_slc
    s_c = _dot_nt(kc_ref[0], q)
    row = lax.broadcasted_iota(jnp.int32, (n_rows, 1), 0)
    n_idx = CMP_PER_SLC * (row % n_slc) + row // n_slc
    mask_c = (n_idx * CMP_STRIDE + (CMP_LEN - 1)) <= t_lane
    e_c, inv_c = _softmax_cols(s_c, mask_c)
    o_c = _dot(vct_ref[0], e_c.astype(BF16)) * inv_c

    p_c = e_c * inv_c
    psum = p_c[:, 0:Q_TILE]
    for g in range(1, Q_PER_KV):
        psum = psum + p_c[:, g * Q_TILE:(g + 1) * Q_TILE]
    j_idx = lax.broadcasted_iota(jnp.int32, (n_slc, 1), 0)
    straddle = psum[3 * n_slc:4 * n_slc]
    from_prev = jnp.where(j_idx == 0, 0.0, pltpu.roll(straddle, 1, axis=0))
    imp = (psum[0:n_slc] + psum[n_slc:2 * n_slc] + psum[2 * n_slc:3 * n_slc]
           + 0.5 * straddle + 0.5 * from_prev)

    tq = t0 + lax.broadcasted_iota(jnp.int32, (1, Q_TILE), 1)
    forced = (j_idx == 0) | (j_idx == tq // SLC_LEN)
    valid = (j_idx * SLC_LEN) <= tq
    score = jnp.where(valid, jnp.where(forced, FORCE_SCORE, imp), NEG_INF)
    j_f = j_idx.astype(F32)

    def pick(_, carry):
        score, sel = carry
        best = jnp.max(score, axis=0, keepdims=True)
        first = jnp.min(jnp.where(score == best, j_f, float(n_slc)), axis=0, keepdims=True)
        hit = j_f == first
        return jnp.where(hit, -jnp.inf, score), jnp.where(hit, 1.0, sel)

    _, sel = lax.fori_loop(0, topk, pick, (score, jnp.zeros((n_slc, Q_TILE), F32)))
    sel_ref[...] = jnp.where(valid, sel, 0.0)

    blocks_per_tile = KV_TILE // SLC_LEN

    def kv_tile(j, carry, causal):
        m, l, acc = carry
        k0 = pl.multiple_of(j * KV_TILE, KV_TILE)
        s = _dot_nt(ks_ref[0, pl.ds(k0, KV_TILE), :], q)
        selrows = sel_ref[pl.ds(pl.multiple_of(j * blocks_per_tile, blocks_per_tile), blocks_per_tile), :]
        blk = jnp.concatenate(
            [jnp.broadcast_to(selrows[b:b + 1, :], (SLC_LEN, Q_TILE)) for b in range(blocks_per_tile)], axis=0)
        mask = jnp.concatenate([blk] * Q_PER_KV, axis=1) > 0.5
        if causal:
            tok = k0 + lax.broadcasted_iota(jnp.int32, (KV_TILE, 1), 0)
            mask = mask & (tok <= t_lane)
        s = jnp.where(mask, s, NEG_INF)
        m_new = jnp.maximum(m, jnp.max(s, axis=0, keepdims=True))
        p = jnp.exp(s - m_new)
        alpha = jnp.exp(m - m_new)
        l = alpha * l + jnp.sum(p, axis=0, keepdims=True)
        acc = alpha * acc + _dot(vst_ref[0, :, pl.ds(k0, KV_TILE)], p.astype(BF16))
        return m_new, l, acc

    n_full = t0 // KV_TILE
    init = (jnp.full((1, qw), NEG_INF, F32), jnp.zeros((1, qw), F32), jnp.zeros((HEAD_DIM, qw), F32))
    carry = lax.fori_loop(0, n_full, lambda j, cr: kv_tile(j, cr, False), init)
    _, l_s, acc_s = kv_tile(n_full, carry, True)
    o_s = acc_s * (1.0 / jnp.maximum(l_s, 1e-30))

    span = WINDOW + Q_TILE
    w0 = pl.multiple_of(jnp.maximum(t0 - WINDOW, 0), Q_TILE)
    s_w = _dot_nt(kw_ref[0, pl.ds(w0, span), :], q)
    diff = t_lane - (w0 + lax.broadcasted_iota(jnp.int32, (span, 1), 0))
    e_w, inv_w = _softmax_cols(s_w, (diff >= 0) & (diff < WINDOW))
    o_w = _dot(vwt_ref[0, :, pl.ds(w0, span)], e_w.astype(BF16)) * inv_w

    gates = jax.nn.sigmoid(ng_ref[0, 0])
    o = gates[0:1] * o_c + gates[1:2] * o_s + gates[2:3] * o_w
    pairs = []
    for g in range(0, Q_PER_KV, 2):
        two = jnp.concatenate([o[:, g * Q_TILE:(g + 1) * Q_TILE], o[:, (g + 1) * Q_TILE:(g + 2) * Q_TILE]], axis=0)
        pairs.append(two.T)
    out_ref[...] = jnp.concatenate(pairs, axis=1).astype(out_ref.dtype)


def _attention(q4, kc, vct, ks, vst, kw, vwt, ngt, seq):
    n_slc = seq // SLC_LEN
    nb = seq // Q_TILE
    qw = Q_PER_KV * Q_TILE
    n_rows = CMP_PER_SLC * n_slc
    topk = min(SLC_TOPK, n_slc)
    head = lambda k, c: (k, 0, 0)
    return pl.pallas_call(
        functools.partial(_attn_kernel, n_slc=n_slc, topk=topk),
        grid=(N_KV_HEADS, nb),
        in_specs=[
            pl.BlockSpec((1, 1, qw, HEAD_DIM), lambda k, c: (k, c, 0, 0)),
            pl.BlockSpec((1, n_rows, HEAD_DIM), head),
            pl.BlockSpec((1, HEAD_DIM, n_rows), head),
            pl.BlockSpec((1, seq, HEAD_DIM), head),
            pl.BlockSpec((1, HEAD_DIM, seq), head),
            pl.BlockSpec((1, seq, HEAD_DIM), head),
            pl.BlockSpec((1, HEAD_DIM, seq), head),
            pl.BlockSpec((1, 1, 3, qw), lambda k, c: (k, c, 0, 0)),
        ],
        out_specs=pl.BlockSpec((Q_TILE, Q_PER_KV * HEAD_DIM), lambda k, c: (c, k)),
        out_shape=jax.ShapeDtypeStruct((seq, N_HEADS * HEAD_DIM), BF16),
        scratch_shapes=[pltpu.VMEM((n_slc, Q_TILE), F32)],
        compiler_params=pltpu.CompilerParams(
            dimension_semantics=("arbitrary", "arbitrary"), vmem_limit_bytes=VMEM_LIMIT_BYTES),
        name="nsa_attention",
    )(q4, kc, vct, ks, vst, kw, vwt, ngt)


def _discretize_kernel(are_ref, aim_ref, ldt_ref, bre_ref, bim_ref, ar_ref, ai_ref, br_ref, bi_ref):
    a_re, a_im = are_ref[...], aim_ref[...]
    dt = jnp.exp(ldt_ref[...])
    mag = jnp.exp(a_re * dt)
    ang = a_im * dt
    ab_re, ab_im = mag * jnp.cos(ang), mag * jnp.sin(ang)
    n_re, n_im = ab_re - 1.0, ab_im
    den = a_re * a_re + a_im * a_im
    c_re = (n_re * a_re + n_im * a_im) / den
    c_im = (n_im * a_re - n_re * a_im) / den
    b_re, b_im = bre_ref[...], bim_ref[...]
    ar_ref[...] = ab_re
    ai_ref[...] = ab_im
    br_ref[...] = c_re * b_re - c_im * b_im
    bi_ref[...] = c_re * b_im + c_im * b_re


def _discretize(a_re, a_im, log_dt, b_re, b_im):
    g, p, h = b_re.shape
    col = jax.ShapeDtypeStruct((g, p, 1), F32)
    full = jax.ShapeDtypeStruct((g, p, h), F32)
    return pl.pallas_call(
        _discretize_kernel, out_shape=[col, col, full, full], name="ssm_discretize",
    )(a_re[..., None], a_im[..., None], log_dt[:, None, None], b_re, b_im)


def _cmul(ar, ai, br, bi):
    return ar * br - ai * bi, ar * bi + ai * br


def _s5_kernel(u_ref, bblk_ref, a_ref, cblk_ref, d_ref, wglu_ref, y_ref, xr_ref, xi_ref, st_ref):
    seg = SCAN_SEGMENTS
    steps = xr_ref.shape[0]
    n_state = xr_ref.shape[2]
    n_blk, blk_in, blk_state2 = bblk_ref.shape
    blk_state = blk_state2 // 2

    @pl.when(pl.program_id(0) == 0)
    def _():
        st_ref[...] = jnp.zeros_like(st_ref)

    u = u_ref[...]
    ub = u.astype(BF16)
    for k in range(n_blk):
        bu = _dot(ub[:, k * blk_in:(k + 1) * blk_in], bblk_ref[k])
        xr_ref[:, :, k * blk_state:(k + 1) * blk_state] = bu[:, :blk_state].reshape(steps, seg, blk_state)
        xi_ref[:, :, k * blk_state:(k + 1) * blk_state] = bu[:, blk_state:].reshape(steps, seg, blk_state)

    lb = SCAN_LANE_BLOCK
    for b0 in range(0, n_state, lb):
        lanes = slice(b0, b0 + lb)
        a_re = a_ref[0:1, lanes]
        a_im = a_ref[1:2, lanes]
        ar8 = jnp.broadcast_to(a_re, (seg, lb))
        ai8 = jnp.broadcast_to(a_im, (seg, lb))

        def local(j, carry):
            xr, xi = carry
            pr, pi = _cmul(ar8, ai8, xr, xi)
            xr = pr + xr_ref[j, :, lanes]
            xi = pi + xi_ref[j, :, lanes]
            xr_ref[j, :, lanes] = xr
            xi_ref[j, :, lanes] = xi
            return xr, xi

        end_r, end_i = lax.fori_loop(0, steps, local, (jnp.zeros((seg, lb), F32), jnp.zeros((seg, lb), F32)))

        pr, pi = a_re, a_im
        for _ in range(steps.bit_length() - 1):
            pr, pi = _cmul(pr, pi, pr, pi)
        cr, ci = st_ref[0:1, lanes], st_ref[1:2, lanes]
        in_r, in_i = [], []
        for s in range(seg):
            in_r.append(cr)
            in_i.append(ci)
            qr, qi = _cmul(pr, pi, cr, ci)
            cr, ci = qr + end_r[s:s + 1], qi + end_i[s:s + 1]
        st_ref[0:1, lanes] = cr
        st_ref[1:2, lanes] = ci
        in_r = jnp.concatenate(in_r, axis=0)
        in_i = jnp.concatenate(in_i, axis=0)

        def fix(j, carry):
            wr, wi = carry
            fr, fi = _cmul(wr, wi, in_r, in_i)
            xr_ref[j, :, lanes] = xr_ref[j, :, lanes] + fr
            xi_ref[j, :, lanes] = xi_ref[j, :, lanes] + fi
            return _cmul(wr, wi, ar8, ai8)

        lax.fori_loop(0, steps, fix, (ar8, ai8))

    rows = steps * seg
    y = d_ref[...] * u
    for k in range(n_blk):
        st = slice(k * blk_state, (k + 1) * blk_state)
        xs = jnp.concatenate([xr_ref[:, :, st].reshape(rows, blk_state),
                              xi_ref[:, :, st].reshape(rows, blk_state)], axis=1).astype(BF16)
        yk = _dot(xs, cblk_ref[k])
        if k == 0:
            ys = [yk]
        else:
            ys.append(yk)
    y = y + jnp.concatenate(ys, axis=1)
    y = jax.nn.gelu(y)
    y = y * jax.nn.sigmoid(_dot(y.astype(BF16), wglu_ref[...]))
    y_ref[...] = y.astype(y_ref.dtype)


def _s5(u_perm, bblk, a_lane, cblk, d_row, w_glu):
    seq, width = u_perm.shape
    tm = ROW_TILE
    steps = tm // SCAN_SEGMENTS
    n_state = a_lane.shape[1]
    const2 = lambda i: (0, 0)
    const3 = lambda i: (0, 0, 0)
    return pl.pallas_call(
        _s5_kernel,
        grid=(seq // tm,),
        in_specs=[
            pl.BlockSpec((tm, width), lambda i: (i, 0)),
            pl.BlockSpec(bblk.shape, const3),
            pl.BlockSpec(a_lane.shape, const2),
            pl.BlockSpec(cblk.shape, const3),
            pl.BlockSpec((1, width), const2),
            pl.BlockSpec((width, width), const2),
        ],
        out_specs=pl.BlockSpec((tm, width), lambda i: (i, 0)),
        out_shape=jax.ShapeDtypeStruct((seq, width), BF16),
        scratch_shapes=[
            pltpu.VMEM((steps, SCAN_SEGMENTS, n_state), F32),
            pltpu.VMEM((steps, SCAN_SEGMENTS, n_state), F32),
            pltpu.VMEM((2, n_state), F32),
        ],
        compiler_params=pltpu.CompilerParams(
            dimension_semantics=("arbitrary",), vmem_limit_bytes=VMEM_LIMIT_BYTES),
        name="s5_scan",
    )(u_perm, bblk, a_lane, cblk, d_row, w_glu)


def _ffn_kernel(x_ref, ya_ref, yb_ref, gmix_ref, wg_ref, wpa_ref, wpb_ref, wout_ref, gffn_ref,
                wup_ref, cw_ref, cb_ref, wdown_ref, gfin_ref, o_ref, tail_ref, *, d_ff, chunk):
    tm, d_model = x_ref.shape

    @pl.when(pl.program_id(0) == 0)
    def _():
        tail_ref[...] = jnp.zeros_like(tail_ref)

    x = x_ref[...]
    h = _rmsnorm(x, gmix_ref[...]).astype(BF16)
    gates = jax.nn.sigmoid(_dot(h, wg_ref[...]))
    merged = (gates[:, :d_model] * _dot(ya_ref[...], wpa_ref[...])
              + gates[:, d_model:] * _dot(yb_ref[...], wpb_ref[...]))
    x1 = x + _dot(merged.astype(BF16), wout_ref[...])
    h2 = _rmsnorm(x1, gffn_ref[...]).astype(BF16)

    def conv_cols(c0):
        cols = slice(c0, c0 + chunk)
        up = _dot(h2, wup_ref[:, cols])
        ext = jnp.concatenate([tail_ref[:, cols], up], axis=0)
        tail_ref[:, cols] = up[tm - SUBLANES:, :]
        prev1 = pltpu.roll(ext, 1, axis=0)[SUBLANES:]
        prev2 = pltpu.roll(ext, 2, axis=0)[SUBLANES:]
        return cb_ref[:, cols] + cw_ref[0:1, cols] * prev2 + cw_ref[1:2, cols] * prev1 + cw_ref[2:3, cols] * up

    acc = x1
    for k in range(d_ff // chunk):
        gate = conv_cols(k * chunk)
        val = conv_cols(d_ff + k * chunk)
        act = (jax.nn.silu(gate) * val).astype(BF16)
        acc = acc + _dot(act, wdown_ref[k * chunk:(k + 1) * chunk, :])
    o_ref[...] = _rmsnorm(acc, gfin_ref[...])


def _ffn(x2, ya, yb, g_mix, w_g, w_pa, w_pb, w_out, g_ffn, w_up, conv_w, conv_b, w_down, g_fin):
    seq, d_model = x2.shape
    d_ff = w_down.shape[0]
    tm = FFN_ROW_TILE
    chunk = d_ff // 2
    assert chunk % LANES == 0
    row = lambda i: (i, 0)

    def resident(arr):
        return pl.BlockSpec(arr.shape, lambda i: (0, 0), pipeline_mode=pl.Buffered(1))

    return pl.pallas_call(
        functools.partial(_ffn_kernel, d_ff=d_ff, chunk=chunk),
        grid=(seq // tm,),
        in_specs=[
            pl.BlockSpec((tm, d_model), row),
            pl.BlockSpec((tm, ya.shape[1]), row),
            pl.BlockSpec((tm, yb.shape[1]), row),
            resident(g_mix), resident(w_g), resident(w_pa), resident(w_pb), resident(w_out), resident(g_ffn),
            resident(w_up), resident(conv_w), resident(conv_b), resident(w_down), resident(g_fin),
        ],
        out_specs=pl.BlockSpec((tm, d_model), row),
        out_shape=jax.ShapeDtypeStruct((seq, d_model), F32),
        scratch_shapes=[pltpu.VMEM((SUBLANES, 2 * d_ff), F32)],
        compiler_params=pltpu.CompilerParams(
            dimension_semantics=("arbitrary",), vmem_limit_bytes=VMEM_LIMIT_BYTES),
        name="merge_ffn",
    )(x2, ya, yb, g_mix, w_g, w_pa, w_pb, w_out, g_ffn, w_up, conv_w, conv_b, w_down, g_fin)


def _block_diag(m, n_blk):
    g, r, c = m.shape
    per = g // n_blk
    eye = jnp.eye(per, dtype=m.dtype)
    m = m.reshape(n_blk, per, r, c)
    return jnp.einsum("bgrc,gk->bgrkc", m, eye).reshape(n_blk, per * r, per * c)


def kernel(x, positions, norm_mix, w_in, cmp_k_pe, cmp_k_w1, cmp_k_w2, cmp_v_pe, cmp_v_w1, cmp_v_w2, ssm_a_re, ssm_a_im, ssm_b_re, ssm_b_im, ssm_c_re, ssm_c_im, ssm_d, ssm_log_dt, ssm_w_glu, w_proj_a, w_proj_b, w_out, norm_ffn, w_up, conv_w, conv_b, w_down, norm_final):
    bsz, seq, d_model = x.shape
    assert bsz == 1 and norm_mix.shape[0] == 1
    assert seq % max(ROW_TILE, KV_TILE) == 0 and seq >= WINDOW + Q_TILE
    attn_w = N_HEADS * HEAD_DIM
    kv_w = N_KV_HEADS * HEAD_DIM
    ssm_width = ssm_d.shape[1]
    n_groups = ssm_width // SSM_GROUP
    x2 = x[0]
    pos = positions[0]

    sizes = (attn_w,) + (kv_w,) * 6 + (3 * N_HEADS, ssm_width, d_model, d_model)
    offs = [0]
    for s in sizes:
        offs.append(offs[-1] + s)
    w = w_in[0]
    col = lambda i: w[:, offs[i]:offs[i + 1]]
    w_q, w_kc, w_vc, w_ks, w_vs, w_kw, w_vw, w_ng, w_u, w_ga, w_gb = [col(i) for i in range(11)]
    w_ng = jnp.pad(w_ng, ((0, 0), (0, LANES - w_ng.shape[1])))
    w_main = jnp.concatenate([w_q, w_ks, w_kw, w_kc, w_vc, w_u, w_ng], axis=1).astype(BF16)
    w_vt = jnp.concatenate([w_vs, w_vw], axis=1).T.astype(BF16)
    w_gate = jnp.concatenate([w_ga, w_gb], axis=1).astype(BF16)

    pat = _rope_lane_patterns()
    qk, kvc, u, ng, vt = _inproj(x2, norm_mix, pos[:, None], pat, w_main, w_vt, ssm_width)

    n_half = seq // CMP_STRIDE
    n_cmp = (seq - CMP_LEN) // CMP_STRIDE + 1
    h4 = kvc.reshape(seq, 2 * N_KV_HEADS, HEAD_DIM).transpose(1, 0, 2).reshape(
        2 * N_KV_HEADS, n_half, CMP_STRIDE * HEAD_DIM)
    halves = lambda pe: pe.reshape(2, CMP_STRIDE * HEAD_DIM)
    pe4 = jnp.stack([halves(cmp_k_pe[0])] * N_KV_HEADS + [halves(cmp_v_pe[0])] * N_KV_HEADS)
    w1 = jnp.stack([cmp_k_w1[0], cmp_v_w1[0]]).astype(BF16)
    w2 = jnp.pad(jnp.stack([cmp_k_w2[0], cmp_v_w2[0]]), ((0, 0), (0, 0), (0, LANES - HEAD_DIM))).astype(BF16)
    cmp_end = jnp.arange(n_cmp, dtype=jnp.int32) * CMP_STRIDE + CMP_LEN - 1
    cmp_pos = jnp.pad(pos[cmp_end], (0, n_half - n_cmp))
    pat2 = jnp.stack([pat, jnp.zeros_like(pat)])
    cmp = _compress(h4, pe4, w1, w2, cmp_pos[:, None], pat2)[:, :, :HEAD_DIM]
    n_slc = seq // SLC_LEN
    cmp = cmp.reshape(2 * N_KV_HEADS, n_slc, CMP_PER_SLC, HEAD_DIM).transpose(0, 2, 1, 3).reshape(
        2 * N_KV_HEADS, n_half, HEAD_DIM).astype(BF16)
    kc = cmp[:N_KV_HEADS]
    vct = cmp[N_KV_HEADS:].transpose(0, 2, 1)

    nb = seq // Q_TILE
    q4 = qk[:, :attn_w].reshape(nb, Q_TILE, N_KV_HEADS, Q_PER_KV, HEAD_DIM).transpose(2, 0, 3, 1, 4).reshape(
        N_KV_HEADS, nb, Q_PER_KV * Q_TILE, HEAD_DIM)
    ks = qk[:, attn_w:attn_w + kv_w].reshape(seq, N_KV_HEADS, HEAD_DIM).transpose(1, 0, 2)
    kw = qk[:, attn_w + kv_w:].reshape(seq, N_KV_HEADS, HEAD_DIM).transpose(1, 0, 2)
    vst = vt[:kv_w].reshape(N_KV_HEADS, HEAD_DIM, seq)
    vwt = vt[kv_w:].reshape(N_KV_HEADS, HEAD_DIM, seq)
    ngt = ng[:, :3 * N_HEADS].reshape(nb, Q_TILE, N_KV_HEADS, Q_PER_KV, 3).transpose(2, 0, 4, 3, 1).reshape(
        N_KV_HEADS, nb, 3, Q_PER_KV * Q_TILE)
    y_a = _attention(q4, kc, vct, ks, vst, kw, vwt, ngt, seq)

    ab_re, ab_im, bb_re, bb_im = _discretize(ssm_a_re[0], ssm_a_im[0], ssm_log_dt[0], ssm_b_re[0], ssm_b_im[0])
    n_blk = 2
    a_lane = jnp.stack([ab_re.reshape(-1), ab_im.reshape(-1)])
    to_in = lambda b: _block_diag(b.transpose(0, 2, 1), n_blk)
    bblk = jnp.concatenate([to_in(bb_re), to_in(bb_im)], axis=2).astype(BF16)
    to_out = lambda cm: _block_diag(cm.transpose(0, 2, 1), n_blk)
    cblk = jnp.concatenate([to_out(ssm_c_re[0]), -to_out(ssm_c_im[0])], axis=1).astype(BF16)
    steps = ROW_TILE // SCAN_SEGMENTS
    nt = seq // ROW_TILE
    u_perm = u.reshape(nt, SCAN_SEGMENTS, steps, ssm_width).transpose(0, 2, 1, 3).reshape(seq, ssm_width)
    y_b = _s5(u_perm, bblk, a_lane, cblk, ssm_d, ssm_w_glu[0].astype(BF16))
    y_b = y_b.reshape(nt, steps, SCAN_SEGMENTS, ssm_width).transpose(0, 2, 1, 3).reshape(seq, ssm_width)

    out = _ffn(x2, y_a, y_b, norm_mix, w_gate, w_proj_a[0].astype(BF16), w_proj_b[0].astype(BF16),
               w_out[0].astype(BF16), norm_ffn, w_up[0].astype(BF16), conv_w[0], conv_b, w_down[0].astype(BF16),
               norm_final[None, :])
    return out[None]
```

```python
import functools

import jax
import jax.numpy as jnp
from jax import lax
from jax.experimental import pallas as pl
from jax.experimental.pallas import tpu as pltpu

F32 = jnp.float32
BF16 = jnp.bfloat16

N_HEADS = 8
N_KV_HEADS = 2
HEAD_DIM = 64
Q_PER_KV = N_HEADS // N_KV_HEADS
ROT_DIM = HEAD_DIM // 4
ROPE_THETA = 500000.0
CMP_LEN = 32
CMP_STRIDE = 16
CMP_PER_SLC = 4
SLC_LEN = 64
SLC_TOPK = 16
WINDOW = 512
SSM_GROUP = 16
SSM_STATE = 64
CONV_WIDTH = 3
EPS = 1e-6
NEG_INF = -1e30
FORCE_SCORE = 1e4
LOG2E = 1.4426950408889634

LANES = 128
SUBLANES = 8
BF16_ROWS = 16
VMEM_LIMIT_BYTES = 56 * 1024 * 1024

ROW_TILE = 512
FFN_ROW_TILE = 256
Q_TILE = 128
KV_TILE = 512
SCAN_SEGMENTS = SUBLANES
SCAN_LANE_BLOCK = 512


def _rmsnorm(x, g):
    return x * lax.rsqrt(jnp.mean(x * x, axis=-1, keepdims=True) + EPS) * g


def _dot(a, b):
    return jnp.dot(a, b, preferred_element_type=F32)


def _dot_nt(a, b):
    return lax.dot_general(a, b, (((1,), (1,)), ((), ())), preferred_element_type=F32)


def _rope_lane_patterns():
    half = ROT_DIM // 2
    d = jnp.arange(LANES) % HEAD_DIM
    inv_freq = ROPE_THETA ** (-jnp.arange(0, ROT_DIM, 2, dtype=F32) / ROT_DIM)
    freq = jnp.where(d < ROT_DIM, inv_freq[d % half], 0.0)
    m_lo = jnp.where(d < half, -1.0, 0.0)
    m_hi = jnp.where((d >= half) & (d < ROT_DIM), 1.0, 0.0)
    return jnp.stack([freq, m_lo, m_hi]).astype(F32)


def _rope_tables(pos_col, pat):
    ang = pos_col.astype(F32) * pat[0:1]
    sin = jnp.sin(ang)
    return jnp.cos(ang), sin * pat[1:2], sin * pat[2:3]


def _rope(x, cos, sin_lo, sin_hi):
    half = ROT_DIM // 2
    return (x * cos + pltpu.roll(x, LANES - half, axis=1) * sin_lo
            + pltpu.roll(x, half, axis=1) * sin_hi)


KVC_COLS = 2 * N_KV_HEADS * HEAD_DIM
V_ROWS = HEAD_DIM + BF16_ROWS
GATE_ROWS = 16


def _rope_t(xh, cos, sin):
    half = ROT_DIM // 2
    x1, x2 = xh[0:half], xh[half:ROT_DIM]
    return jnp.concatenate([x1 * cos - x2 * sin, x2 * cos + x1 * sin, xh[ROT_DIM:]], axis=0)


def _inproj_kernel(x_ref, g_ref, pos_ref, invf_ref, wn_ref, wt_ref,
                   qt_ref, ks_ref, kw_ref, vst_ref, vwt_ref, ngt_ref, kvc_ref, u_ref):
    tm = x_ref.shape[0]
    h = _rmsnorm(x_ref[...], g_ref[...]).astype(BF16)
    pn = _dot(h, wn_ref[...])
    kvc_ref[...] = pn[:, :KVC_COLS]
    u_ref[...] = pn[:, KVC_COLS:]

    pt = _dot_nt(wt_ref[...], h)
    ang = invf_ref[...] * pos_ref[...].astype(F32)
    cos, sin = jnp.cos(ang), jnp.sin(ang)
    q_scale = (HEAD_DIM ** -0.5) * LOG2E
    for hd in range(N_HEADS):
        rows = slice(hd * HEAD_DIM, (hd + 1) * HEAD_DIM)
        qt_ref[rows, :] = (_rope_t(pt[rows], cos, sin) * q_scale).astype(BF16)

    kv_w = N_KV_HEADS * HEAD_DIM
    r_ks = N_HEADS * HEAD_DIM
    r_kw, r_vs, r_vw, r_ng = r_ks + kv_w, r_ks + 2 * kv_w, r_ks + 3 * kv_w, r_ks + 4 * kv_w
    key = pl.program_id(0) * tm + lax.broadcasted_iota(jnp.int32, (BF16_ROWS, tm), 1)
    blk = lax.broadcasted_iota(jnp.int32, (BF16_ROWS, tm), 0)
    onehot = jnp.where((key % KV_TILE) // SLC_LEN == blk, 1.0, 0.0)
    ones_rows = jnp.where(blk == 0, 1.0, 0.0)
    pad_k = jnp.zeros((LANES - HEAD_DIM - BF16_ROWS, tm), F32)
    pad_w = jnp.zeros((LANES - HEAD_DIM, tm), F32)
    for kh in range(N_KV_HEADS):
        off = kh * HEAD_DIM
        k_s = _rope_t(pt[r_ks + off:r_ks + off + HEAD_DIM], cos, sin)
        ks_ref[kh] = jnp.concatenate([k_s, onehot, pad_k], axis=0).T.astype(BF16)
        k_w = _rope_t(pt[r_kw + off:r_kw + off + HEAD_DIM], cos, sin)
        kw_ref[kh] = jnp.concatenate([k_w, pad_w], axis=0).T.astype(BF16)
        vst_ref[kh] = jnp.concatenate([pt[r_vs + off:r_vs + off + HEAD_DIM], ones_rows], axis=0).astype(BF16)
        vwt_ref[kh] = jnp.concatenate([pt[r_vw + off:r_vw + off + HEAD_DIM], ones_rows], axis=0).astype(BF16)
    ngt_ref[...] = pt[r_ng:r_ng + N_KV_HEADS * GATE_ROWS]


def _inproj(x2, g, pos_row, inv_freq, w_nat, w_t):
    seq, d_model = x2.shape
    tm = ROW_TILE
    n_nat = w_nat.shape[1]
    n_t = w_t.shape[0]
    attn_w = N_HEADS * HEAD_DIM
    row = lambda i: (i, 0)
    col = lambda i: (0, i)
    const = lambda i: (0, 0)
    head_rows = lambda i: (0, i, 0)
    head_cols = lambda i: (0, 0, i)
    return pl.pallas_call(
        _inproj_kernel,
        grid=(seq // tm,),
        in_specs=[
            pl.BlockSpec((tm, d_model), row),
            pl.BlockSpec((1, d_model), const),
            pl.BlockSpec((1, tm), col),
            pl.BlockSpec((ROT_DIM // 2, 1), const),
            pl.BlockSpec((d_model, n_nat), const),
            pl.BlockSpec((n_t, d_model), const),
        ],
        out_specs=[
            pl.BlockSpec((attn_w, tm), col),
            pl.BlockSpec((N_KV_HEADS, tm, LANES), head_rows),
            pl.BlockSpec((N_KV_HEADS, tm, LANES), head_rows),
            pl.BlockSpec((N_KV_HEADS, V_ROWS, tm), head_cols),
            pl.BlockSpec((N_KV_HEADS, V_ROWS, tm), head_cols),
            pl.BlockSpec((N_KV_HEADS * GATE_ROWS, tm), col),
            pl.BlockSpec((tm, KVC_COLS), row),
            pl.BlockSpec((tm, n_nat - KVC_COLS), row),
        ],
        out_shape=[
            jax.ShapeDtypeStruct((attn_w, seq), BF16),
            jax.ShapeDtypeStruct((N_KV_HEADS, seq, LANES), BF16),
            jax.ShapeDtypeStruct((N_KV_HEADS, seq, LANES), BF16),
            jax.ShapeDtypeStruct((N_KV_HEADS, V_ROWS, seq), BF16),
            jax.ShapeDtypeStruct((N_KV_HEADS, V_ROWS, seq), BF16),
            jax.ShapeDtypeStruct((N_KV_HEADS * GATE_ROWS, seq), F32),
            jax.ShapeDtypeStruct((seq, KVC_COLS), F32),
            jax.ShapeDtypeStruct((seq, n_nat - KVC_COLS), F32),
        ],
        compiler_params=pltpu.CompilerParams(
            dimension_semantics=("arbitrary",), vmem_limit_bytes=VMEM_LIMIT_BYTES),
        name="inproj",
    )(x2, g, pos_row, inv_freq, w_nat, w_t)


def _compress_kernel(h_ref, pe_ref, w1_ref, w2_ref, pos_ref, pat_ref, out_ref):
    hh = h_ref[0]
    rows, half_w = hh.shape
    a = _dot((hh + pe_ref[0, 0:1]).astype(BF16), w1_ref[0, :half_w])
    b = _dot((hh + pe_ref[0, 1:2]).astype(BF16), w1_ref[0, half_w:])
    pre = a + pltpu.roll(b, rows - 1, axis=0)
    z = _dot(jax.nn.gelu(pre).astype(BF16), w2_ref[0])
    cos, sin_lo, sin_hi = _rope_tables(pos_ref[...], pat_ref[0])
    out_ref[0] = _rope(z, cos, sin_lo, sin_hi)


def _compress(h4, pe4, w1, w2, cmp_pos_col, pat2):
    n, rows, half_w = h4.shape
    hidden = w1.shape[2]
    per = n // 2
    return pl.pallas_call(
        _compress_kernel,
        grid=(n,),
        in_specs=[
            pl.BlockSpec((1, rows, half_w), lambda i: (i, 0, 0)),
            pl.BlockSpec((1, 2, half_w), lambda i: (i, 0, 0)),
            pl.BlockSpec((1, 2 * half_w, hidden), lambda i: (i // per, 0, 0)),
            pl.BlockSpec((1, hidden, LANES), lambda i: (i // per, 0, 0)),
            pl.BlockSpec((rows, 1), lambda i: (0, 0)),
            pl.BlockSpec((1, 3, LANES), lambda i: (i // per, 0, 0)),
        ],
        out_specs=pl.BlockSpec((1, rows, LANES), lambda i: (i, 0, 0)),
        out_shape=jax.ShapeDtypeStruct((n, rows, LANES), F32),
        compiler_params=pltpu.CompilerParams(
            dimension_semantics=("arbitrary",), vmem_limit_bytes=VMEM_LIMIT_BYTES),
        name="compress",
    )(h4, pe4, w1, w2, cmp_pos_col, pat2)


def _attn_kernel(qt_ref, kc_ref, vct_ref, ks_ref, vst_ref, kw_ref, vwt_ref, ng_ref,
                 out_ref, bias_ref, s_ref, acc_ref, m_ref, *, n_slc, topk):
    c = pl.program_id(1)
    qw = Q_PER_KV * Q_TILE
    t0 = c * Q_TILE
    t_lane = t0 + (lax.broadcasted_iota(jnp.int32, (1, qw), 1) & (Q_TILE - 1))
    qt4 = qt_ref[...]
    qt = jnp.concatenate([qt4[g * HEAD_DIM:(g + 1) * HEAD_DIM] for g in range(Q_PER_KV)], axis=1)

    def lanes4(x):
        return jnp.concatenate([x] * Q_PER_KV, axis=1)

    n_rows = CMP_PER_SLC * n_slc
    s_c = _dot(kc_ref[0], qt)
    row = lax.broadcasted_iota(jnp.int32, (n_rows, 1), 0)
    n_idx = CMP_PER_SLC * (row % n_slc) + row // n_slc
    mask_c = (n_idx * CMP_STRIDE + (CMP_LEN - 1)) <= t_lane
    s_c = jnp.where(mask_c, s_c, NEG_INF)
    m_c = jnp.max(s_c, axis=0, keepdims=True)
    e_c = jnp.where(mask_c, jnp.exp2(s_c - m_c), 0.0)
    inv_c = 1.0 / jnp.maximum(jnp.sum(e_c, axis=0, keepdims=True), 1e-30)
    o_c = _dot(vct_ref[0], e_c.astype(BF16)) * inv_c

    p_c = e_c * inv_c
    psum = p_c[:, 0:Q_TILE]
    for g in range(1, Q_PER_KV):
        psum = psum + p_c[:, g * Q_TILE:(g + 1) * Q_TILE]
    j_idx = lax.broadcasted_iota(jnp.int32, (n_slc, 1), 0)
    straddle = psum[3 * n_slc:4 * n_slc]
    from_prev = jnp.where(j_idx == 0, 0.0, pltpu.roll(straddle, 1, axis=0))
    imp = (psum[0:n_slc] + psum[n_slc:2 * n_slc] + psum[2 * n_slc:3 * n_slc]
           + 0.5 * straddle + 0.5 * from_prev)

    tq = t0 + lax.broadcasted_iota(jnp.int32, (1, Q_TILE), 1)
    forced = (j_idx == 0) | (j_idx == tq // SLC_LEN)
    valid = (j_idx * SLC_LEN) <= tq
    score = jnp.where(valid, jnp.where(forced, FORCE_SCORE, imp), NEG_INF)
    j_f = j_idx.astype(F32)
    sel = jnp.zeros((n_slc, Q_TILE), F32)
    for _ in range(topk):
        best = jnp.max(score, axis=0, keepdims=True)
        first = jnp.min(jnp.where(score == best, j_f, float(n_slc)), axis=0, keepdims=True)
        hit = j_f == first
        score = jnp.where(hit, -jnp.inf, score)
        sel = jnp.where(hit, 1.0, sel)
    blocks_per_tile = KV_TILE // SLC_LEN
    n_tiles = n_slc // blocks_per_tile
    bias = jnp.where(valid & (sel > 0.5), 0.0, NEG_INF)
    bias_ref[:, 0:blocks_per_tile, :] = bias.reshape(n_tiles, blocks_per_tile, Q_TILE)
    bias_ref[:, blocks_per_tile:, :] = jnp.zeros((n_tiles, BF16_ROWS - blocks_per_tile, Q_TILE), F32)

    zeros_tail = jnp.zeros((LANES - HEAD_DIM - BF16_ROWS, qw), BF16)

    def scores(slot, j):
        k0 = pl.multiple_of(j * KV_TILE, KV_TILE)
        rhs = jnp.concatenate([qt, lanes4(bias_ref[j]).astype(BF16), zeros_tail], axis=0)
        s_ref[slot] = _dot(ks_ref[0, pl.ds(k0, KV_TILE), :], rhs)

    def accumulate(s, j):
        k0 = pl.multiple_of(j * KV_TILE, KV_TILE)
        m_old = m_ref[...]
        m_new = jnp.maximum(m_old, jnp.max(s, axis=0, keepdims=True))
        p = jnp.exp2(s - m_new).astype(BF16)
        acc_ref[...] = jnp.exp2(m_old - m_new) * acc_ref[...] + _dot(vst_ref[0, :, pl.ds(k0, KV_TILE)], p)
        m_ref[...] = m_new

    m_ref[...] = jnp.full((1, qw), NEG_INF, F32)
    acc_ref[...] = jnp.zeros((V_ROWS, qw), F32)
    n_full = t0 // KV_TILE
    scores(0, 0)

    def pair(i, _):
        j = 2 * i
        scores(1, j + 1)
        accumulate(s_ref[0], j)
        scores(0, j + 2)
        accumulate(s_ref[1], j + 1)
        return 0

    lax.fori_loop(0, n_full // 2, pair, 0)
    odd = n_full % 2

    @pl.when(odd == 1)
    def _():
        scores(1, n_full)
        accumulate(s_ref[0], n_full - 1)

    tok = n_full * KV_TILE + lax.broadcasted_iota(jnp.int32, (KV_TILE, 1), 0)
    accumulate(jnp.where(tok <= t_lane, s_ref[odd], NEG_INF), n_full)
    acc_s = acc_ref[...]
    o_s = acc_s[:HEAD_DIM] * (1.0 / jnp.maximum(acc_s[HEAD_DIM:HEAD_DIM + 1], 1e-30))

    span = WINDOW + Q_TILE
    w0 = pl.multiple_of(jnp.maximum(t0 - WINDOW, 0), Q_TILE)
    rhs_w = jnp.concatenate([qt, jnp.zeros((LANES - HEAD_DIM, qw), BF16)], axis=0)
    s_w = _dot(kw_ref[0, pl.ds(w0, span), :], rhs_w)
    diff = t_lane - (w0 + lax.broadcasted_iota(jnp.int32, (span, 1), 0))
    s_w = jnp.where((diff >= 0) & (diff < WINDOW), s_w, NEG_INF)
    e_w = jnp.exp2(s_w - jnp.max(s_w, axis=0, keepdims=True)).astype(BF16)
    acc_w = _dot(vwt_ref[0, :, pl.ds(w0, span)], e_w)
    o_w = acc_w[:HEAD_DIM] * (1.0 / jnp.maximum(acc_w[HEAD_DIM:HEAD_DIM + 1], 1e-30))

    gates = jax.nn.sigmoid(ng_ref[...])

    def gate(b):
        return jnp.concatenate([gates[b * Q_PER_KV + g:b * Q_PER_KV + g + 1] for g in range(Q_PER_KV)], axis=1)

    o = gate(0) * o_c + gate(1) * o_s + gate(2) * o_w
    pairs = []
    for g in range(0, Q_PER_KV, 2):
        two = jnp.concatenate([o[:, g * Q_TILE:(g + 1) * Q_TILE], o[:, (g + 1) * Q_TILE:(g + 2) * Q_TILE]], axis=0)
        pairs.append(two.T)
    out_ref[...] = jnp.concatenate(pairs, axis=1).astype(out_ref.dtype)


def _attention(qt, kc, vct, ks, vst, kw, vwt, ngt, seq):
    n_slc = seq // SLC_LEN
    nb = seq // Q_TILE
    qw = Q_PER_KV * Q_TILE
    n_rows = CMP_PER_SLC * n_slc
    topk = min(SLC_TOPK, n_slc)
    head = lambda k, c: (k, 0, 0)
    return pl.pallas_call(
        functools.partial(_attn_kernel, n_slc=n_slc, topk=topk),
        grid=(N_KV_HEADS, nb),
        in_specs=[
            pl.BlockSpec((Q_PER_KV * HEAD_DIM, Q_TILE), lambda k, c: (k, c)),
            pl.BlockSpec((1, n_rows, HEAD_DIM), head),
            pl.BlockSpec((1, HEAD_DIM, n_rows), head),
            pl.BlockSpec((1, seq, LANES), head),
            pl.BlockSpec((1, V_ROWS, seq), head),
            pl.BlockSpec((1, seq, LANES), head),
            pl.BlockSpec((1, V_ROWS, seq), head),
            pl.BlockSpec((GATE_ROWS, Q_TILE), lambda k, c: (k, c)),
        ],
        out_specs=pl.BlockSpec((Q_TILE, Q_PER_KV * HEAD_DIM), lambda k, c: (c, k)),
        out_shape=jax.ShapeDtypeStruct((seq, N_HEADS * HEAD_DIM), BF16),
        scratch_shapes=[
            pltpu.VMEM((n_slc * SLC_LEN // KV_TILE, BF16_ROWS, Q_TILE), F32),
            pltpu.VMEM((2, KV_TILE, qw), F32),
            pltpu.VMEM((V_ROWS, qw), F32),
            pltpu.VMEM((1, qw), F32),
        ],
        compiler_params=pltpu.CompilerParams(
            dimension_semantics=("arbitrary", "arbitrary"), vmem_limit_bytes=VMEM_LIMIT_BYTES),
        name="nsa_attention",
    )(qt, kc, vct, ks, vst, kw, vwt, ngt)


def _discretize_kernel(are_ref, aim_ref, ldt_ref, bre_ref, bim_ref, ar_ref, ai_ref, br_ref, bi_ref):
    a_re, a_im = are_ref[...], aim_ref[...]
    dt = jnp.exp(ldt_ref[...])
    mag = jnp.exp(a_re * dt)
    ang = a_im * dt
    ab_re, ab_im = mag * jnp.cos(ang), mag * jnp.sin(ang)
    n_re, n_im = ab_re - 1.0, ab_im
    den = a_re * a_re + a_im * a_im
    c_re = (n_re * a_re + n_im * a_im) / den
    c_im = (n_im * a_re - n_re * a_im) / den
    b_re, b_im = bre_ref[...], bim_ref[...]
    ar_ref[...] = ab_re
    ai_ref[...] = ab_im
    br_ref[...] = c_re * b_re - c_im * b_im
    bi_ref[...] = c_re * b_im + c_im * b_re


def _discretize(a_re, a_im, log_dt, b_re, b_im):
    g, p, h = b_re.shape
    col = jax.ShapeDtypeStruct((g, p, 1), F32)
    full = jax.ShapeDtypeStruct((g, p, h), F32)
    return pl.pallas_call(
        _discretize_kernel, out_shape=[col, col, full, full], name="ssm_discretize",
    )(a_re[..., None], a_im[..., None], log_dt[:, None, None], b_re, b_im)


def _cmul(ar, ai, br, bi):
    return ar * br - ai * bi, ar * bi + ai * br


def _s5_kernel(u_ref, bblk_ref, a_ref, cblk_ref, d_ref, wglu_ref, y_ref, xr_ref, xi_ref, st_ref):
    seg = SCAN_SEGMENTS
    steps = xr_ref.shape[0]
    n_state = xr_ref.shape[2]
    n_blk, blk_in, blk_state2 = bblk_ref.shape
    blk_state = blk_state2 // 2

    @pl.when(pl.program_id(0) == 0)
    def _():
        st_ref[...] = jnp.zeros_like(st_ref)

    u = u_ref[...]
    ub = u.astype(BF16)
    for k in range(n_blk):
        bu = _dot(ub[:, k * blk_in:(k + 1) * blk_in], bblk_ref[k])
        xr_ref[:, :, k * blk_state:(k + 1) * blk_state] = bu[:, :blk_state].reshape(steps, seg, blk_state)
        xi_ref[:, :, k * blk_state:(k + 1) * blk_state] = bu[:, blk_state:].reshape(steps, seg, blk_state)

    lb = SCAN_LANE_BLOCK
    for b0 in range(0, n_state, lb):
        lanes = slice(b0, b0 + lb)
        a_re = a_ref[0:1, lanes]
        a_im = a_ref[1:2, lanes]
        ar8 = jnp.broadcast_to(a_re, (seg, lb))
        ai8 = jnp.broadcast_to(a_im, (seg, lb))

        def local(j, carry):
            xr, xi = carry
            pr, pi = _cmul(ar8, ai8, xr, xi)
            xr = pr + xr_ref[j, :, lanes]
            xi = pi + xi_ref[j, :, lanes]
            xr_ref[j, :, lanes] = xr
            xi_ref[j, :, lanes] = xi
            return xr, xi

        end_r, end_i = lax.fori_loop(0, steps, local, (jnp.zeros((seg, lb), F32), jnp.zeros((seg, lb), F32)))

        pr, pi = a_re, a_im
        for _ in range(steps.bit_length() - 1):
            pr, pi = _cmul(pr, pi, pr, pi)
        cr, ci = st_ref[0:1, lanes], st_ref[1:2, lanes]
        in_r, in_i = [], []
        for s in range(seg):
            in_r.append(cr)
            in_i.append(ci)
            qr, qi = _cmul(pr, pi, cr, ci)
            cr, ci = qr + end_r[s:s + 1], qi + end_i[s:s + 1]
        st_ref[0:1, lanes] = cr
        st_ref[1:2, lanes] = ci
        in_r = jnp.concatenate(in_r, axis=0)
        in_i = jnp.concatenate(in_i, axis=0)

        def fix(j, carry):
            wr, wi = carry
            fr, fi = _cmul(wr, wi, in_r, in_i)
            xr_ref[j, :, lanes] = xr_ref[j, :, lanes] + fr
            xi_ref[j, :, lanes] = xi_ref[j, :, lanes] + fi
            return _cmul(wr, wi, ar8, ai8)

        lax.fori_loop(0, steps, fix, (ar8, ai8))

    rows = steps * seg
    y = d_ref[...] * u
    for k in range(n_blk):
        st = slice(k * blk_state, (k + 1) * blk_state)
        xs = jnp.concatenate([xr_ref[:, :, st].reshape(rows, blk_state),
                              xi_ref[:, :, st].reshape(rows, blk_state)], axis=1).astype(BF16)
        yk = _dot(xs, cblk_ref[k])
        if k == 0:
            ys = [yk]
        else:
            ys.append(yk)
    y = y + jnp.concatenate(ys, axis=1)
    y = jax.nn.gelu(y)
    y = y * jax.nn.sigmoid(_dot(y.astype(BF16), wglu_ref[...]))
    y_ref[...] = y.astype(y_ref.dtype)


def _s5(u_perm, bblk, a_lane, cblk, d_row, w_glu):
    seq, width = u_perm.shape
    tm = ROW_TILE
    steps = tm // SCAN_SEGMENTS
    n_state = a_lane.shape[1]
    const2 = lambda i: (0, 0)
    const3 = lambda i: (0, 0, 0)
    return pl.pallas_call(
        _s5_kernel,
        grid=(seq // tm,),
        in_specs=[
            pl.BlockSpec((tm, width), lambda i: (i, 0)),
            pl.BlockSpec(bblk.shape, const3),
            pl.BlockSpec(a_lane.shape, const2),
            pl.BlockSpec(cblk.shape, const3),
            pl.BlockSpec((1, width), const2),
            pl.BlockSpec((width, width), const2),
        ],
        out_specs=pl.BlockSpec((tm, width), lambda i: (i, 0)),
        out_shape=jax.ShapeDtypeStruct((seq, width), BF16),
        scratch_shapes=[
            pltpu.VMEM((steps, SCAN_SEGMENTS, n_state), F32),
            pltpu.VMEM((steps, SCAN_SEGMENTS, n_state), F32),
            pltpu.VMEM((2, n_state), F32),
        ],
        compiler_params=pltpu.CompilerParams(
            dimension_semantics=("arbitrary",), vmem_limit_bytes=VMEM_LIMIT_BYTES),
        name="s5_scan",
    )(u_perm, bblk, a_lane, cblk, d_row, w_glu)


def _ffn_kernel(x_ref, ya_ref, yb_ref, gmix_ref, wg_ref, wpa_ref, wpb_ref, wout_ref, gffn_ref,
                wup_ref, cw_ref, cb_ref, wdown_ref, gfin_ref, o_ref, tail_ref, *, d_ff, chunk):
    tm, d_model = x_ref.shape

    @pl.when(pl.program_id(0) == 0)
    def _():
        tail_ref[...] = jnp.zeros_like(tail_ref)

    x = x_ref[...]
    h = _rmsnorm(x, gmix_ref[...]).astype(BF16)
    gates = jax.nn.sigmoid(_dot(h, wg_ref[...]))
    merged = (gates[:, :d_model] * _dot(ya_ref[...], wpa_ref[...])
              + gates[:, d_model:] * _dot(yb_ref[...], wpb_ref[...]))
    x1 = x + _dot(merged.astype(BF16), wout_ref[...])
    h2 = _rmsnorm(x1, gffn_ref[...]).astype(BF16)

    def conv_cols(c0):
        cols = slice(c0, c0 + chunk)
        up = _dot(h2, wup_ref[:, cols])
        ext = jnp.concatenate([tail_ref[:, cols], up], axis=0)
        tail_ref[:, cols] = up[tm - SUBLANES:, :]
        prev1 = pltpu.roll(ext, 1, axis=0)[SUBLANES:]
        prev2 = pltpu.roll(ext, 2, axis=0)[SUBLANES:]
        return cb_ref[:, cols] + cw_ref[0:1, cols] * prev2 + cw_ref[1:2, cols] * prev1 + cw_ref[2:3, cols] * up

    acc = x1
    for k in range(d_ff // chunk):
        gate = conv_cols(k * chunk)
        val = conv_cols(d_ff + k * chunk)
        act = (jax.nn.silu(gate) * val).astype(BF16)
        acc = acc + _dot(act, wdown_ref[k * chunk:(k + 1) * chunk, :])
    o_ref[...] = _rmsnorm(acc, gfin_ref[...])


def _ffn(x2, ya, yb, g_mix, w_g, w_pa, w_pb, w_out, g_ffn, w_up, conv_w, conv_b, w_down, g_fin):
    seq, d_model = x2.shape
    d_ff = w_down.shape[0]
    tm = FFN_ROW_TILE
    chunk = d_ff // 2
    assert chunk % LANES == 0
    row = lambda i: (i, 0)

    def resident(arr):
        return pl.BlockSpec(arr.shape, lambda i: (0, 0), pipeline_mode=pl.Buffered(1))

    return pl.pallas_call(
        functools.partial(_ffn_kernel, d_ff=d_ff, chunk=chunk),
        grid=(seq // tm,),
        in_specs=[
            pl.BlockSpec((tm, d_model), row),
            pl.BlockSpec((tm, ya.shape[1]), row),
            pl.BlockSpec((tm, yb.shape[1]), row),
            resident(g_mix), resident(w_g), resident(w_pa), resident(w_pb), resident(w_out), resident(g_ffn),
            resident(w_up), resident(conv_w), resident(conv_b), resident(w_down), resident(g_fin),
        ],
        out_specs=pl.BlockSpec((tm, d_model), row),
        out_shape=jax.ShapeDtypeStruct((seq, d_model), F32),
        scratch_shapes=[pltpu.VMEM((SUBLANES, 2 * d_ff), F32)],
        compiler_params=pltpu.CompilerParams(
            dimension_semantics=("arbitrary",), vmem_limit_bytes=VMEM_LIMIT_BYTES),
        name="merge_ffn",
    )(x2, ya, yb, g_mix, w_g, w_pa, w_pb, w_out, g_ffn, w_up, conv_w, conv_b, w_down, g_fin)


def _block_diag(m, n_blk):
    g, r, c = m.shape
    per = g // n_blk
    eye = jnp.eye(per, dtype=m.dtype)
    m = m.reshape(n_blk, per, r, c)
    return jnp.einsum("bgrc,gk->bgrkc", m, eye).reshape(n_blk, per * r, per * c)


def kernel(x, positions, norm_mix, w_in, cmp_k_pe, cmp_k_w1, cmp_k_w2, cmp_v_pe, cmp_v_w1, cmp_v_w2, ssm_a_re, ssm_a_im, ssm_b_re, ssm_b_im, ssm_c_re, ssm_c_im, ssm_d, ssm_log_dt, ssm_w_glu, w_proj_a, w_proj_b, w_out, norm_ffn, w_up, conv_w, conv_b, w_down, norm_final):
    bsz, seq, d_model = x.shape
    assert bsz == 1 and norm_mix.shape[0] == 1
    assert seq % (2 * KV_TILE) == 0 and ROW_TILE % KV_TILE == 0 and seq >= WINDOW + Q_TILE
    attn_w = N_HEADS * HEAD_DIM
    kv_w = N_KV_HEADS * HEAD_DIM
    ssm_width = ssm_d.shape[1]
    x2 = x[0]
    pos = positions[0]

    sizes = (attn_w,) + (kv_w,) * 6 + (3 * N_HEADS, ssm_width, d_model, d_model)
    offs = [0]
    for s in sizes:
        offs.append(offs[-1] + s)
    w = w_in[0]
    col = lambda i: w[:, offs[i]:offs[i + 1]]
    w_q, w_kc, w_vc, w_ks, w_vs, w_kw, w_vw, w_ng, w_u, w_ga, w_gb = [col(i) for i in range(11)]
    w_ng = w_ng.reshape(d_model, N_KV_HEADS, Q_PER_KV, 3).transpose(0, 1, 3, 2).reshape(d_model, N_KV_HEADS, 3 * Q_PER_KV)
    w_ng = jnp.pad(w_ng, ((0, 0), (0, 0), (0, GATE_ROWS - 3 * Q_PER_KV))).reshape(d_model, N_KV_HEADS * GATE_ROWS)
    w_t = jnp.concatenate([w_q, w_ks, w_kw, w_vs, w_vw, w_ng], axis=1).T.astype(BF16)
    w_nat = jnp.concatenate([w_kc, w_vc, w_u], axis=1).astype(BF16)
    w_gate = jnp.concatenate([w_ga, w_gb], axis=1).astype(BF16)

    inv_freq = ROPE_THETA ** (-jnp.arange(0, ROT_DIM, 2, dtype=F32) / ROT_DIM)
    qt, ks, kw, vst, vwt, ngt, kvc, u = _inproj(x2, norm_mix, pos[None, :], inv_freq[:, None], w_nat, w_t)

    n_half = seq // CMP_STRIDE
    n_cmp = (seq - CMP_LEN) // CMP_STRIDE + 1
    h4 = kvc.reshape(seq, 2 * N_KV_HEADS, HEAD_DIM).transpose(1, 0, 2).reshape(
        2 * N_KV_HEADS, n_half, CMP_STRIDE * HEAD_DIM)
    halves = lambda pe: pe.reshape(2, CMP_STRIDE * HEAD_DIM)
    pe4 = jnp.stack([halves(cmp_k_pe[0])] * N_KV_HEADS + [halves(cmp_v_pe[0])] * N_KV_HEADS)
    w1 = jnp.stack([cmp_k_w1[0], cmp_v_w1[0]]).astype(BF16)
    w2 = jnp.pad(jnp.stack([cmp_k_w2[0], cmp_v_w2[0]]), ((0, 0), (0, 0), (0, LANES - HEAD_DIM))).astype(BF16)
    cmp_end = jnp.arange(n_cmp, dtype=jnp.int32) * CMP_STRIDE + CMP_LEN - 1
    cmp_pos = jnp.pad(pos[cmp_end], (0, n_half - n_cmp))
    pat = _rope_lane_patterns()
    pat2 = jnp.stack([pat, jnp.zeros_like(pat)])
    cmp = _compress(h4, pe4, w1, w2, cmp_pos[:, None], pat2)[:, :, :HEAD_DIM]
    n_slc = seq // SLC_LEN
    cmp = cmp.reshape(2 * N_KV_HEADS, n_slc, CMP_PER_SLC, HEAD_DIM).transpose(0, 2, 1, 3).reshape(
        2 * N_KV_HEADS, n_half, HEAD_DIM).astype(BF16)
    kc = cmp[:N_KV_HEADS]
    vct = cmp[N_KV_HEADS:].transpose(0, 2, 1)

    y_a = _attention(qt, kc, vct, ks, vst, kw, vwt, ngt, seq)

    ab_re, ab_im, bb_re, bb_im = _discretize(ssm_a_re[0], ssm_a_im[0], ssm_log_dt[0], ssm_b_re[0], ssm_b_im[0])
    n_blk = 2
    a_lane = jnp.stack([ab_re.reshape(-1), ab_im.reshape(-1)])
    to_in = lambda b: _block_diag(b.transpose(0, 2, 1), n_blk)
    bblk = jnp.concatenate([to_in(bb_re), to_in(bb_im)], axis=2).astype(BF16)
    to_out = lambda cm: _block_diag(cm.transpose(0, 2, 1), n_blk)
    cblk = jnp.concatenate([to_out(ssm_c_re[0]), -to_out(ssm_c_im[0])], axis=1).astype(BF16)
    steps = ROW_TILE // SCAN_SEGMENTS
    nt = seq // ROW_TILE
    u_perm = u.reshape(nt, SCAN_SEGMENTS, steps, ssm_width).transpose(0, 2, 1, 3).reshape(seq, ssm_width)
    y_b = _s5(u_perm, bblk, a_lane, cblk, ssm_d, ssm_w_glu[0].astype(BF16))
    y_b = y_b.reshape(nt, steps, SCAN_SEGMENTS, ssm_width).transpose(0, 2, 1, 3).reshape(seq, ssm_width)

    out = _ffn(x2, y_a, y_b, norm_mix, w_gate, w_proj_a[0].astype(BF16), w_proj_b[0].astype(BF16),
               w_out[0].astype(BF16), norm_ffn, w_up[0].astype(BF16), conv_w[0], conv_b, w_down[0].astype(BF16),
               norm_final[None, :])
    return out[None]
```

```python
import functools

import jax
import jax.numpy as jnp
from jax import lax
from jax.experimental import pallas as pl
from jax.experimental.pallas import tpu as pltpu

F32 = jnp.float32
BF16 = jnp.bfloat16

N_HEADS = 8
N_KV_HEADS = 2
HEAD_DIM = 64
Q_PER_KV = N_HEADS // N_KV_HEADS
ROT_DIM = HEAD_DIM // 4
ROPE_THETA = 500000.0
CMP_LEN = 32
CMP_STRIDE = 16
CMP_PER_SLC = 4
SLC_LEN = 64
SLC_TOPK = 16
WINDOW = 512
SSM_GROUP = 16
SSM_STATE = 64
CONV_WIDTH = 3
EPS = 1e-6
NEG_INF = -1e30
FORCE_SCORE = 1e4
LOG2E = 1.4426950408889634

LANES = 128
SUBLANES = 8
BF16_ROWS = 16
VMEM_LIMIT_BYTES = 56 * 1024 * 1024

ROW_TILE = 512
FFN_ROW_TILE = 256
Q_TILE = 256
KV_TILE = 512
SCAN_SEGMENTS = SUBLANES
SCAN_LANE_BLOCK = 512


def _rmsnorm(x, g):
    return x * lax.rsqrt(jnp.mean(x * x, axis=-1, keepdims=True) + EPS) * g


def _dot(a, b):
    return jnp.dot(a, b, preferred_element_type=F32)


def _dot_nt(a, b):
    return lax.dot_general(a, b, (((1,), (1,)), ((), ())), preferred_element_type=F32)


def _rope_lane_patterns():
    half = ROT_DIM // 2
    d = jnp.arange(LANES) % HEAD_DIM
    inv_freq = ROPE_THETA ** (-jnp.arange(0, ROT_DIM, 2, dtype=F32) / ROT_DIM)
    freq = jnp.where(d < ROT_DIM, inv_freq[d % half], 0.0)
    m_lo = jnp.where(d < half, -1.0, 0.0)
    m_hi = jnp.where((d >= half) & (d < ROT_DIM), 1.0, 0.0)
    return jnp.stack([freq, m_lo, m_hi]).astype(F32)


def _rope_tables(pos_col, pat):
    ang = pos_col.astype(F32) * pat[0:1]
    sin = jnp.sin(ang)
    return jnp.cos(ang), sin * pat[1:2], sin * pat[2:3]


def _rope(x, cos, sin_lo, sin_hi):
    half = ROT_DIM // 2
    return (x * cos + pltpu.roll(x, LANES - half, axis=1) * sin_lo
            + pltpu.roll(x, half, axis=1) * sin_hi)


KVC_COLS = 2 * N_KV_HEADS * HEAD_DIM
V_ROWS = HEAD_DIM + BF16_ROWS
GATE_ROWS = 16


def _rope_t(xh, cos, sin):
    half = ROT_DIM // 2
    x1, x2 = xh[0:half], xh[half:ROT_DIM]
    return jnp.concatenate([x1 * cos - x2 * sin, x2 * cos + x1 * sin, xh[ROT_DIM:]], axis=0)


def _inproj_kernel(x_ref, g_ref, pos_ref, invf_ref, wn_ref, wt_ref,
                   qt_ref, ks_ref, kw_ref, vst_ref, vwt_ref, ngt_ref, kvc_ref, u_ref):
    tm = x_ref.shape[0]
    h = _rmsnorm(x_ref[...], g_ref[...]).astype(BF16)
    pn = _dot(h, wn_ref[...])
    kvc_ref[...] = pn[:, :KVC_COLS]
    u_ref[...] = pn[:, KVC_COLS:]

    pt = _dot_nt(wt_ref[...], h)
    ang = invf_ref[...] * pos_ref[...].astype(F32)
    cos, sin = jnp.cos(ang), jnp.sin(ang)
    q_scale = (HEAD_DIM ** -0.5) * LOG2E
    for hd in range(N_HEADS):
        rows = slice(hd * HEAD_DIM, (hd + 1) * HEAD_DIM)
        qt_ref[rows, :] = (_rope_t(pt[rows], cos, sin) * q_scale).astype(BF16)

    kv_w = N_KV_HEADS * HEAD_DIM
    r_ks = N_HEADS * HEAD_DIM
    r_kw, r_vs, r_vw, r_ng = r_ks + kv_w, r_ks + 2 * kv_w, r_ks + 3 * kv_w, r_ks + 4 * kv_w
    key = pl.program_id(0) * tm + lax.broadcasted_iota(jnp.int32, (BF16_ROWS, tm), 1)
    blk = lax.broadcasted_iota(jnp.int32, (BF16_ROWS, tm), 0)
    onehot = jnp.where((key % KV_TILE) // SLC_LEN == blk, 1.0, 0.0)
    ones_rows = jnp.where(blk == 0, 1.0, 0.0)
    pad_k = jnp.zeros((LANES - HEAD_DIM - BF16_ROWS, tm), F32)
    pad_w = jnp.zeros((LANES - HEAD_DIM, tm), F32)
    for kh in range(N_KV_HEADS):
        off = kh * HEAD_DIM
        k_s = _rope_t(pt[r_ks + off:r_ks + off + HEAD_DIM], cos, sin)
        ks_ref[kh] = jnp.concatenate([k_s, onehot, pad_k], axis=0).T.astype(BF16)
        k_w = _rope_t(pt[r_kw + off:r_kw + off + HEAD_DIM], cos, sin)
        kw_ref[kh] = jnp.concatenate([k_w, pad_w], axis=0).T.astype(BF16)
        vst_ref[kh] = jnp.concatenate([pt[r_vs + off:r_vs + off + HEAD_DIM], ones_rows], axis=0).astype(BF16)
        vwt_ref[kh] = jnp.concatenate([pt[r_vw + off:r_vw + off + HEAD_DIM], ones_rows], axis=0).astype(BF16)
    ngt_ref[...] = pt[r_ng:r_ng + N_KV_HEADS * GATE_ROWS]


def _inproj(x2, g, pos_row, inv_freq, w_nat, w_t):
    seq, d_model = x2.shape
    tm = ROW_TILE
    n_nat = w_nat.shape[1]
    n_t = w_t.shape[0]
    attn_w = N_HEADS * HEAD_DIM
    row = lambda i: (i, 0)
    col = lambda i: (0, i)
    const = lambda i: (0, 0)
    head_rows = lambda i: (0, i, 0)
    head_cols = lambda i: (0, 0, i)
    return pl.pallas_call(
        _inproj_kernel,
        grid=(seq // tm,),
        in_specs=[
            pl.BlockSpec((tm, d_model), row),
            pl.BlockSpec((1, d_model), const),
            pl.BlockSpec((1, tm), col),
            pl.BlockSpec((ROT_DIM // 2, 1), const),
            pl.BlockSpec((d_model, n_nat), const),
            pl.BlockSpec((n_t, d_model), const),
        ],
        out_specs=[
            pl.BlockSpec((attn_w, tm), col),
            pl.BlockSpec((N_KV_HEADS, tm, LANES), head_rows),
            pl.BlockSpec((N_KV_HEADS, tm, LANES), head_rows),
            pl.BlockSpec((N_KV_HEADS, V_ROWS, tm), head_cols),
            pl.BlockSpec((N_KV_HEADS, V_ROWS, tm), head_cols),
            pl.BlockSpec((N_KV_HEADS * GATE_ROWS, tm), col),
            pl.BlockSpec((tm, KVC_COLS), row),
            pl.BlockSpec((tm, n_nat - KVC_COLS), row),
        ],
        out_shape=[
            jax.ShapeDtypeStruct((attn_w, seq), BF16),
            jax.ShapeDtypeStruct((N_KV_HEADS, seq, LANES), BF16),
            jax.ShapeDtypeStruct((N_KV_HEADS, seq, LANES), BF16),
            jax.ShapeDtypeStruct((N_KV_HEADS, V_ROWS, seq), BF16),
            jax.ShapeDtypeStruct((N_KV_HEADS, V_ROWS, seq), BF16),
            jax.ShapeDtypeStruct((N_KV_HEADS * GATE_ROWS, seq), F32),
            jax.ShapeDtypeStruct((seq, KVC_COLS), F32),
            jax.ShapeDtypeStruct((seq, n_nat - KVC_COLS), F32),
        ],
        compiler_params=pltpu.CompilerParams(
            dimension_semantics=("arbitrary",), vmem_limit_bytes=VMEM_LIMIT_BYTES),
        name="inproj",
    )(x2, g, pos_row, inv_freq, w_nat, w_t)


def _compress_kernel(h_ref, pe_ref, w1_ref, w2_ref, pos_ref, pat_ref, out_ref):
    hh = h_ref[0]
    rows, half_w = hh.shape
    a = _dot((hh + pe_ref[0, 0:1]).astype(BF16), w1_ref[0, :half_w])
    b = _dot((hh + pe_ref[0, 1:2]).astype(BF16), w1_ref[0, half_w:])
    pre = a + pltpu.roll(b, rows - 1, axis=0)
    z = _dot(jax.nn.gelu(pre).astype(BF16), w2_ref[0])
    cos, sin_lo, sin_hi = _rope_tables(pos_ref[...], pat_ref[0])
    out_ref[0] = _rope(z, cos, sin_lo, sin_hi)


def _compress(h4, pe4, w1, w2, cmp_pos_col, pat2):
    n, rows, half_w = h4.shape
    hidden = w1.shape[2]
    per = n // 2
    return pl.pallas_call(
        _compress_kernel,
        grid=(n,),
        in_specs=[
            pl.BlockSpec((1, rows, half_w), lambda i: (i, 0, 0)),
            pl.BlockSpec((1, 2, half_w), lambda i: (i, 0, 0)),
            pl.BlockSpec((1, 2 * half_w, hidden), lambda i: (i // per, 0, 0)),
            pl.BlockSpec((1, hidden, LANES), lambda i: (i // per, 0, 0)),
            pl.BlockSpec((rows, 1), lambda i: (0, 0)),
            pl.BlockSpec((1, 3, LANES), lambda i: (i // per, 0, 0)),
        ],
        out_specs=pl.BlockSpec((1, rows, LANES), lambda i: (i, 0, 0)),
        out_shape=jax.ShapeDtypeStruct((n, rows, LANES), F32),
        compiler_params=pltpu.CompilerParams(
            dimension_semantics=("arbitrary",), vmem_limit_bytes=VMEM_LIMIT_BYTES),
        name="compress",
    )(h4, pe4, w1, w2, cmp_pos_col, pat2)


def _attn_kernel(qt_ref, kc_ref, vct_ref, ks_ref, vst_ref, kw_ref, vwt_ref, ng_ref,
                 out_ref, bias_ref, s_ref, acc_ref, m_ref, *, n_slc, topk):
    c = pl.program_id(1)
    qw = Q_PER_KV * Q_TILE
    t0 = c * Q_TILE
    t_lane = t0 + (lax.broadcasted_iota(jnp.int32, (1, qw), 1) & (Q_TILE - 1))
    qt4 = qt_ref[...]
    qt = jnp.concatenate([qt4[g * HEAD_DIM:(g + 1) * HEAD_DIM] for g in range(Q_PER_KV)], axis=1)

    def lanes4(x):
        return jnp.concatenate([x] * Q_PER_KV, axis=1)

    n_rows = CMP_PER_SLC * n_slc
    s_c = _dot(kc_ref[0], qt)
    row = lax.broadcasted_iota(jnp.int32, (n_rows, 1), 0)
    n_idx = CMP_PER_SLC * (row % n_slc) + row // n_slc
    mask_c = (n_idx * CMP_STRIDE + (CMP_LEN - 1)) <= t_lane
    s_c = jnp.where(mask_c, s_c, NEG_INF)
    m_c = jnp.max(s_c, axis=0, keepdims=True)
    e_c = jnp.where(mask_c, jnp.exp2(s_c - m_c), 0.0)
    inv_c = 1.0 / jnp.maximum(jnp.sum(e_c, axis=0, keepdims=True), 1e-30)
    o_c = _dot(vct_ref[0], e_c.astype(BF16)) * inv_c

    p_c = e_c * inv_c
    psum = p_c[:, 0:Q_TILE]
    for g in range(1, Q_PER_KV):
        psum = psum + p_c[:, g * Q_TILE:(g + 1) * Q_TILE]
    j_idx = lax.broadcasted_iota(jnp.int32, (n_slc, 1), 0)
    straddle = psum[3 * n_slc:4 * n_slc]
    from_prev = jnp.where(j_idx == 0, 0.0, pltpu.roll(straddle, 1, axis=0))
    imp = (psum[0:n_slc] + psum[n_slc:2 * n_slc] + psum[2 * n_slc:3 * n_slc]
           + 0.5 * straddle + 0.5 * from_prev)

    span = WINDOW + Q_TILE
    w0 = pl.multiple_of(jnp.maximum(t0 - WINDOW, 0), Q_TILE)
    rhs_w = jnp.concatenate([qt, jnp.zeros((LANES - HEAD_DIM, qw), BF16)], axis=0)
    s_w = _dot(kw_ref[0, pl.ds(w0, span), :], rhs_w)
    diff = t_lane - (w0 + lax.broadcasted_iota(jnp.int32, (span, 1), 0))
    s_w = jnp.where((diff >= 0) & (diff < WINDOW), s_w, NEG_INF)
    e_w = jnp.exp2(s_w - jnp.max(s_w, axis=0, keepdims=True)).astype(BF16)
    acc_w = _dot(vwt_ref[0, :, pl.ds(w0, span)], e_w)
    o_w = acc_w[:HEAD_DIM] * (1.0 / jnp.maximum(acc_w[HEAD_DIM:HEAD_DIM + 1], 1e-30))

    tq = t0 + lax.broadcasted_iota(jnp.int32, (1, Q_TILE), 1)
    forced = (j_idx == 0) | (j_idx == tq // SLC_LEN)
    valid = (j_idx * SLC_LEN) <= tq
    score = jnp.where(valid, jnp.where(forced, FORCE_SCORE, imp), NEG_INF)
    j_f = j_idx.astype(F32)
    sel = jnp.zeros((n_slc, Q_TILE), F32)
    for _ in range(topk):
        best = jnp.max(score, axis=0, keepdims=True)
        first = jnp.min(jnp.where(score == best, j_f, float(n_slc)), axis=0, keepdims=True)
        hit = j_f == first
        score = jnp.where(hit, -jnp.inf, score)
        sel = jnp.where(hit, 1.0, sel)
    blocks_per_tile = KV_TILE // SLC_LEN
    n_tiles = n_slc // blocks_per_tile
    bias = jnp.where(valid & (sel > 0.5), 0.0, NEG_INF)
    bias_ref[:, 0:blocks_per_tile, :] = bias.reshape(n_tiles, blocks_per_tile, Q_TILE)
    bias_ref[:, blocks_per_tile:, :] = jnp.zeros((n_tiles, BF16_ROWS - blocks_per_tile, Q_TILE), F32)

    zeros_tail = jnp.zeros((LANES - HEAD_DIM - BF16_ROWS, qw), BF16)

    def scores(slot, j):
        k0 = pl.multiple_of(j * KV_TILE, KV_TILE)
        rhs = jnp.concatenate([qt, lanes4(bias_ref[j]).astype(BF16), zeros_tail], axis=0)
        s_ref[slot] = _dot(ks_ref[0, pl.ds(k0, KV_TILE), :], rhs)

    def accumulate(s, j):
        k0 = pl.multiple_of(j * KV_TILE, KV_TILE)
        m_old = m_ref[...]
        m_new = jnp.maximum(m_old, jnp.max(s, axis=0, keepdims=True))
        p = jnp.exp2(s - m_new).astype(BF16)
        acc_ref[...] = jnp.exp2(m_old - m_new) * acc_ref[...] + _dot(vst_ref[0, :, pl.ds(k0, KV_TILE)], p)
        m_ref[...] = m_new

    m_ref[...] = jnp.full((1, qw), NEG_INF, F32)
    acc_ref[...] = jnp.zeros((V_ROWS, qw), F32)
    n_full = t0 // KV_TILE
    scores(0, 0)

    def pair(i, _):
        j = 2 * i
        scores(1, j + 1)
        accumulate(s_ref[0], j)
        scores(0, j + 2)
        accumulate(s_ref[1], j + 1)
        return 0

    lax.fori_loop(0, n_full // 2, pair, 0)
    odd = n_full % 2

    @pl.when(odd == 1)
    def _():
        scores(1, n_full)
        accumulate(s_ref[0], n_full - 1)

    tok = n_full * KV_TILE + lax.broadcasted_iota(jnp.int32, (KV_TILE, 1), 0)
    accumulate(jnp.where(tok <= t_lane, s_ref[odd], NEG_INF), n_full)
    acc_s = acc_ref[...]
    o_s = acc_s[:HEAD_DIM] * (1.0 / jnp.maximum(acc_s[HEAD_DIM:HEAD_DIM + 1], 1e-30))

    gates = jax.nn.sigmoid(ng_ref[...])

    def gate(b):
        return jnp.concatenate([gates[b * Q_PER_KV + g:b * Q_PER_KV + g + 1] for g in range(Q_PER_KV)], axis=1)

    o = gate(0) * o_c + gate(1) * o_s + gate(2) * o_w
    pairs = []
    for g in range(0, Q_PER_KV, 2):
        two = jnp.concatenate([o[:, g * Q_TILE:(g + 1) * Q_TILE], o[:, (g + 1) * Q_TILE:(g + 2) * Q_TILE]], axis=0)
        pairs.append(two.T)
    out_ref[...] = jnp.concatenate(pairs, axis=1).astype(out_ref.dtype)


def _attention(qt, kc, vct, ks, vst, kw, vwt, ngt, seq):
    n_slc = seq // SLC_LEN
    nb = seq // Q_TILE
    qw = Q_PER_KV * Q_TILE
    n_rows = CMP_PER_SLC * n_slc
    topk = min(SLC_TOPK, n_slc)
    head = lambda k, c: (k, 0, 0)
    return pl.pallas_call(
        functools.partial(_attn_kernel, n_slc=n_slc, topk=topk),
        grid=(N_KV_HEADS, nb),
        in_specs=[
            pl.BlockSpec((Q_PER_KV * HEAD_DIM, Q_TILE), lambda k, c: (k, c)),
            pl.BlockSpec((1, n_rows, HEAD_DIM), head),
            pl.BlockSpec((1, HEAD_DIM, n_rows), head),
            pl.BlockSpec((1, seq, LANES), head),
            pl.BlockSpec((1, V_ROWS, seq), head),
            pl.BlockSpec((1, seq, LANES), head),
            pl.BlockSpec((1, V_ROWS, seq), head),
            pl.BlockSpec((GATE_ROWS, Q_TILE), lambda k, c: (k, c)),
        ],
        out_specs=pl.BlockSpec((Q_TILE, Q_PER_KV * HEAD_DIM), lambda k, c: (c, k)),
        out_shape=jax.ShapeDtypeStruct((seq, N_HEADS * HEAD_DIM), BF16),
        scratch_shapes=[
            pltpu.VMEM((n_slc * SLC_LEN // KV_TILE, BF16_ROWS, Q_TILE), F32),
            pltpu.VMEM((2, KV_TILE, qw), F32),
            pltpu.VMEM((V_ROWS, qw), F32),
            pltpu.VMEM((1, qw), F32),
        ],
        compiler_params=pltpu.CompilerParams(
            dimension_semantics=("arbitrary", "arbitrary"), vmem_limit_bytes=VMEM_LIMIT_BYTES),
        name="nsa_attention",
    )(qt, kc, vct, ks, vst, kw, vwt, ngt)


def _discretize_kernel(are_ref, aim_ref, ldt_ref, bre_ref, bim_ref, ar_ref, ai_ref, br_ref, bi_ref):
    a_re, a_im = are_ref[...], aim_ref[...]
    dt = jnp.exp(ldt_ref[...])
    mag = jnp.exp(a_re * dt)
    ang = a_im * dt
    ab_re, ab_im = mag * jnp.cos(ang), mag * jnp.sin(ang)
    n_re, n_im = ab_re - 1.0, ab_im
    den = a_re * a_re + a_im * a_im
    c_re = (n_re * a_re + n_im * a_im) / den
    c_im = (n_im * a_re - n_re * a_im) / den
    b_re, b_im = bre_ref[...], bim_ref[...]
    ar_ref[...] = ab_re
    ai_ref[...] = ab_im
    br_ref[...] = c_re * b_re - c_im * b_im
    bi_ref[...] = c_re * b_im + c_im * b_re


def _discretize(a_re, a_im, log_dt, b_re, b_im):
    g, p, h = b_re.shape
    col = jax.ShapeDtypeStruct((g, p, 1), F32)
    full = jax.ShapeDtypeStruct((g, p, h), F32)
    return pl.pallas_call(
        _discretize_kernel, out_shape=[col, col, full, full], name="ssm_discretize",
    )(a_re[..., None], a_im[..., None], log_dt[:, None, None], b_re, b_im)


def _cmul(ar, ai, br, bi):
    return ar * br - ai * bi, ar * bi + ai * br


def _s5_kernel(u_ref, bblk_ref, a_ref, cblk_ref, d_ref, wglu_ref, y_ref, xr_ref, xi_ref, st_ref):
    seg = SCAN_SEGMENTS
    steps = xr_ref.shape[0]
    n_state = xr_ref.shape[2]
    n_blk, blk_in, blk_state2 = bblk_ref.shape
    blk_state = blk_state2 // 2

    @pl.when(pl.program_id(0) == 0)
    def _():
        st_ref[...] = jnp.zeros_like(st_ref)

    u = u_ref[...]
    ub = u.astype(BF16)
    for k in range(n_blk):
        bu = _dot(ub[:, k * blk_in:(k + 1) * blk_in], bblk_ref[k])
        xr_ref[:, :, k * blk_state:(k + 1) * blk_state] = bu[:, :blk_state].reshape(steps, seg, blk_state)
        xi_ref[:, :, k * blk_state:(k + 1) * blk_state] = bu[:, blk_state:].reshape(steps, seg, blk_state)

    lb = SCAN_LANE_BLOCK
    for b0 in range(0, n_state, lb):
        lanes = slice(b0, b0 + lb)
        a_re = a_ref[0:1, lanes]
        a_im = a_ref[1:2, lanes]
        ar8 = jnp.broadcast_to(a_re, (seg, lb))
        ai8 = jnp.broadcast_to(a_im, (seg, lb))

        def local(j, carry):
            xr, xi = carry
            pr, pi = _cmul(ar8, ai8, xr, xi)
            xr = pr + xr_ref[j, :, lanes]
            xi = pi + xi_ref[j, :, lanes]
            xr_ref[j, :, lanes] = xr
            xi_ref[j, :, lanes] = xi
            return xr, xi

        end_r, end_i = lax.fori_loop(0, steps, local, (jnp.zeros((seg, lb), F32), jnp.zeros((seg, lb), F32)))

        pr, pi = a_re, a_im
        for _ in range(steps.bit_length() - 1):
            pr, pi = _cmul(pr, pi, pr, pi)
        cr, ci = st_ref[0:1, lanes], st_ref[1:2, lanes]
        in_r, in_i = [], []
        for s in range(seg):
            in_r.append(cr)
            in_i.append(ci)
            qr, qi = _cmul(pr, pi, cr, ci)
            cr, ci = qr + end_r[s:s + 1], qi + end_i[s:s + 1]
        st_ref[0:1, lanes] = cr
        st_ref[1:2, lanes] = ci
        in_r = jnp.concatenate(in_r, axis=0)
        in_i = jnp.concatenate(in_i, axis=0)

        def fix(j, carry):
            wr, wi = carry
            fr, fi = _cmul(wr, wi, in_r, in_i)
            xr_ref[j, :, lanes] = xr_ref[j, :, lanes] + fr
            xi_ref[j, :, lanes] = xi_ref[j, :, lanes] + fi
            return _cmul(wr, wi, ar8, ai8)

        lax.fori_loop(0, steps, fix, (ar8, ai8))

    rows = steps * seg
    y = d_ref[...] * u
    for k in range(n_blk):
        st = slice(k * blk_state, (k + 1) * blk_state)
        xs = jnp.concatenate([xr_ref[:, :, st].reshape(rows, blk_state),
                              xi_ref[:, :, st].reshape(rows, blk_state)], axis=1).astype(BF16)
        yk = _dot(xs, cblk_ref[k])
        if k == 0:
            ys = [yk]
        else:
            ys.append(yk)
    y = y + jnp.concatenate(ys, axis=1)
    y = jax.nn.gelu(y)
    y = y * jax.nn.sigmoid(_dot(y.astype(BF16), wglu_ref[...]))
    y_ref[...] = y.astype(y_ref.dtype)


def _s5(u_perm, bblk, a_lane, cblk, d_row, w_glu):
    seq, width = u_perm.shape
    tm = ROW_TILE
    steps = tm // SCAN_SEGMENTS
    n_state = a_lane.shape[1]
    const2 = lambda i: (0, 0)
    const3 = lambda i: (0, 0, 0)
    return pl.pallas_call(
        _s5_kernel,
        grid=(seq // tm,),
        in_specs=[
            pl.BlockSpec((tm, width), lambda i: (i, 0)),
            pl.BlockSpec(bblk.shape, const3),
            pl.BlockSpec(a_lane.shape, const2),
            pl.BlockSpec(cblk.shape, const3),
            pl.BlockSpec((1, width), const2),
            pl.BlockSpec((width, width), const2),
        ],
        out_specs=pl.BlockSpec((tm, width), lambda i: (i, 0)),
        out_shape=jax.ShapeDtypeStruct((seq, width), BF16),
        scratch_shapes=[
            pltpu.VMEM((steps, SCAN_SEGMENTS, n_state), F32),
            pltpu.VMEM((steps, SCAN_SEGMENTS, n_state), F32),
            pltpu.VMEM((2, n_state), F32),
        ],
        compiler_params=pltpu.CompilerParams(
            dimension_semantics=("arbitrary",), vmem_limit_bytes=VMEM_LIMIT_BYTES),
        name="s5_scan",
    )(u_perm, bblk, a_lane, cblk, d_row, w_glu)


def _ffn_kernel(x_ref, ya_ref, yb_ref, gmix_ref, wg_ref, wpa_ref, wpb_ref, wout_ref, gffn_ref,
                wup_ref, cw_ref, cb_ref, wdown_ref, gfin_ref, o_ref, tail_ref, *, d_ff, chunk):
    tm, d_model = x_ref.shape

    @pl.when(pl.program_id(0) == 0)
    def _():
        tail_ref[...] = jnp.zeros_like(tail_ref)

    x = x_ref[...]
    h = _rmsnorm(x, gmix_ref[...]).astype(BF16)
    gates = jax.nn.sigmoid(_dot(h, wg_ref[...]))
    merged = (gates[:, :d_model] * _dot(ya_ref[...], wpa_ref[...])
              + gates[:, d_model:] * _dot(yb_ref[...], wpb_ref[...]))
    x1 = x + _dot(merged.astype(BF16), wout_ref[...])
    h2 = _rmsnorm(x1, gffn_ref[...]).astype(BF16)

    def conv_cols(c0):
        cols = slice(c0, c0 + chunk)
        up = _dot(h2, wup_ref[:, cols])
        ext = jnp.concatenate([tail_ref[:, cols], up], axis=0)
        tail_ref[:, cols] = up[tm - SUBLANES:, :]
        prev1 = pltpu.roll(ext, 1, axis=0)[SUBLANES:]
        prev2 = pltpu.roll(ext, 2, axis=0)[SUBLANES:]
        return cb_ref[:, cols] + cw_ref[0:1, cols] * prev2 + cw_ref[1:2, cols] * prev1 + cw_ref[2:3, cols] * up

    acc = x1
    for k in range(d_ff // chunk):
        gate = conv_cols(k * chunk)
        val = conv_cols(d_ff + k * chunk)
        act = (jax.nn.silu(gate) * val).astype(BF16)
        acc = acc + _dot(act, wdown_ref[k * chunk:(k + 1) * chunk, :])
    o_ref[...] = _rmsnorm(acc, gfin_ref[...])


def _ffn(x2, ya, yb, g_mix, w_g, w_pa, w_pb, w_out, g_ffn, w_up, conv_w, conv_b, w_down, g_fin):
    seq, d_model = x2.shape
    d_ff = w_down.shape[0]
    tm = FFN_ROW_TILE
    chunk = d_ff // 2
    assert chunk % LANES == 0
    row = lambda i: (i, 0)

    def resident(arr):
        return pl.BlockSpec(arr.shape, lambda i: (0, 0), pipeline_mode=pl.Buffered(1))

    return pl.pallas_call(
        functools.partial(_ffn_kernel, d_ff=d_ff, chunk=chunk),
        grid=(seq // tm,),
        in_specs=[
            pl.BlockSpec((tm, d_model), row),
            pl.BlockSpec((tm, ya.shape[1]), row),
            pl.BlockSpec((tm, yb.shape[1]), row),
            resident(g_mix), resident(w_g), resident(w_pa), resident(w_pb), resident(w_out), resident(g_ffn),
            resident(w_up), resident(conv_w), resident(conv_b), resident(w_down), resident(g_fin),
        ],
        out_specs=pl.BlockSpec((tm, d_model), row),
        out_shape=jax.ShapeDtypeStruct((seq, d_model), F32),
        scratch_shapes=[pltpu.VMEM((SUBLANES, 2 * d_ff), F32)],
        compiler_params=pltpu.CompilerParams(
            dimension_semantics=("arbitrary",), vmem_limit_bytes=VMEM_LIMIT_BYTES),
        name="merge_ffn",
    )(x2, ya, yb, g_mix, w_g, w_pa, w_pb, w_out, g_ffn, w_up, conv_w, conv_b, w_down, g_fin)


def _block_diag(m, n_blk):
    g, r, c = m.shape
    per = g // n_blk
    eye = jnp.eye(per, dtype=m.dtype)
    m = m.reshape(n_blk, per, r, c)
    return jnp.einsum("bgrc,gk->bgrkc", m, eye).reshape(n_blk, per * r, per * c)


def kernel(x, positions, norm_mix, w_in, cmp_k_pe, cmp_k_w1, cmp_k_w2, cmp_v_pe, cmp_v_w1, cmp_v_w2, ssm_a_re, ssm_a_im, ssm_b_re, ssm_b_im, ssm_c_re, ssm_c_im, ssm_d, ssm_log_dt, ssm_w_glu, w_proj_a, w_proj_b, w_out, norm_ffn, w_up, conv_w, conv_b, w_down, norm_final):
    bsz, seq, d_model = x.shape
    assert bsz == 1 and norm_mix.shape[0] == 1
    assert seq % (2 * KV_TILE) == 0 and ROW_TILE % KV_TILE == 0 and seq >= WINDOW + Q_TILE
    attn_w = N_HEADS * HEAD_DIM
    kv_w = N_KV_HEADS * HEAD_DIM
    ssm_width = ssm_d.shape[1]
    x2 = x[0]
    pos = positions[0]

    sizes = (attn_w,) + (kv_w,) * 6 + (3 * N_HEADS, ssm_width, d_model, d_model)
    offs = [0]
    for s in sizes:
        offs.append(offs[-1] + s)
    w = w_in[0]
    col = lambda i: w[:, offs[i]:offs[i + 1]]
    w_q, w_kc, w_vc, w_ks, w_vs, w_kw, w_vw, w_ng, w_u, w_ga, w_gb = [col(i) for i in range(11)]
    w_ng = w_ng.reshape(d_model, N_KV_HEADS, Q_PER_KV, 3).transpose(0, 1, 3, 2).reshape(d_model, N_KV_HEADS, 3 * Q_PER_KV)
    w_ng = jnp.pad(w_ng, ((0, 0), (0, 0), (0, GATE_ROWS - 3 * Q_PER_KV))).reshape(d_model, N_KV_HEADS * GATE_ROWS)
    w_t = jnp.concatenate([w_q, w_ks, w_kw, w_vs, w_vw, w_ng], axis=1).T.astype(BF16)
    w_nat = jnp.concatenate([w_kc, w_vc, w_u], axis=1).astype(BF16)
    w_gate = jnp.concatenate([w_ga, w_gb], axis=1).astype(BF16)

    inv_freq = ROPE_THETA ** (-jnp.arange(0, ROT_DIM, 2, dtype=F32) / ROT_DIM)
    qt, ks, kw, vst, vwt, ngt, kvc, u = _inproj(x2, norm_mix, pos[None, :], inv_freq[:, None], w_nat, w_t)

    n_half = seq // CMP_STRIDE
    n_cmp = (seq - CMP_LEN) // CMP_STRIDE + 1
    h4 = kvc.reshape(seq, 2 * N_KV_HEADS, HEAD_DIM).transpose(1, 0, 2).reshape(
        2 * N_KV_HEADS, n_half, CMP_STRIDE * HEAD_DIM)
    halves = lambda pe: pe.reshape(2, CMP_STRIDE * HEAD_DIM)
    pe4 = jnp.stack([halves(cmp_k_pe[0])] * N_KV_HEADS + [halves(cmp_v_pe[0])] * N_KV_HEADS)
    w1 = jnp.stack([cmp_k_w1[0], cmp_v_w1[0]]).astype(BF16)
    w2 = jnp.pad(jnp.stack([cmp_k_w2[0], cmp_v_w2[0]]), ((0, 0), (0, 0), (0, LANES - HEAD_DIM))).astype(BF16)
    cmp_end = jnp.arange(n_cmp, dtype=jnp.int32) * CMP_STRIDE + CMP_LEN - 1
    cmp_pos = jnp.pad(pos[cmp_end], (0, n_half - n_cmp))
    pat = _rope_lane_patterns()
    pat2 = jnp.stack([pat, jnp.zeros_like(pat)])
    cmp = _compress(h4, pe4, w1, w2, cmp_pos[:, None], pat2)[:, :, :HEAD_DIM]
    n_slc = seq // SLC_LEN
    cmp = cmp.reshape(2 * N_KV_HEADS, n_slc, CMP_PER_SLC, HEAD_DIM).transpose(0, 2, 1, 3).reshape(
        2 * N_KV_HEADS, n_half, HEAD_DIM).astype(BF16)
    kc = cmp[:N_KV_HEADS]
    vct = cmp[N_KV_HEADS:].transpose(0, 2, 1)

    y_a = _attention(qt, kc, vct, ks, vst, kw, vwt, ngt, seq)

    ab_re, ab_im, bb_re, bb_im = _discretize(ssm_a_re[0], ssm_a_im[0], ssm_log_dt[0], ssm_b_re[0], ssm_b_im[0])
    n_blk = 2
    a_lane = jnp.stack([ab_re.reshape(-1), ab_im.reshape(-1)])
    to_in = lambda b: _block_diag(b.transpose(0, 2, 1), n_blk)
    bblk = jnp.concatenate([to_in(bb_re), to_in(bb_im)], axis=2).astype(BF16)
    to_out = lambda cm: _block_diag(cm.transpose(0, 2, 1), n_blk)
    cblk = jnp.concatenate([to_out(ssm_c_re[0]), -to_out(ssm_c_im[0])], axis=1).astype(BF16)
    steps = ROW_TILE // SCAN_SEGMENTS
    nt = seq // ROW_TILE
    u_perm = u.reshape(nt, SCAN_SEGMENTS, steps, ssm_width).transpose(0, 2, 1, 3).reshape(seq, ssm_width)
    y_b = _s5(u_perm, bblk, a_lane, cblk, ssm_d, ssm_w_glu[0].astype(BF16))
    y_b = y_b.reshape(nt, steps, SCAN_SEGMENTS, ssm_width).transpose(0, 2, 1, 3).reshape(seq, ssm_width)

    out = _ffn(x2, y_a, y_b, norm_mix, w_gate, w_proj_a[0].astype(BF16), w_proj_b[0].astype(BF16),
               w_out[0].astype(BF16), norm_ffn, w_up[0].astype(BF16), conv_w[0], conv_b, w_down[0].astype(BF16),
               norm_final[None, :])
    return out[None]
```

```python
import functools

import jax
import jax.numpy as jnp
from jax import lax
from jax.experimental import pallas as pl
from jax.experimental.pallas import tpu as pltpu

F32 = jnp.float32
BF16 = jnp.bfloat16

N_HEADS = 8
N_KV_HEADS = 2
HEAD_DIM = 64
Q_PER_KV = N_HEADS // N_KV_HEADS
ROT_DIM = HEAD_DIM // 4
ROPE_THETA = 500000.0
CMP_LEN = 32
CMP_STRIDE = 16
CMP_PER_SLC = 4
SLC_LEN = 64
SLC_TOPK = 16
WINDOW = 512
SSM_GROUP = 16
SSM_STATE = 64
CONV_WIDTH = 3
EPS = 1e-6
NEG_INF = -1e30
FORCE_SCORE = 1e4
LOG2E = 1.4426950408889634

LANES = 128
SUBLANES = 8
BF16_ROWS = 16
VMEM_LIMIT_BYTES = 56 * 1024 * 1024

ROW_TILE = 512
FFN_ROW_TILE = 256
FFN_COL_CHUNK = 1408
Q_TILE = 256
KV_TILE = 512
SCORE_SLOTS = 2
SCAN_SEGMENTS = SUBLANES
SCAN_LANE_BLOCK = 512


def _rmsnorm(x, g):
    return x * lax.rsqrt(jnp.mean(x * x, axis=-1, keepdims=True) + EPS) * g


def _dot(a, b):
    return jnp.dot(a, b, preferred_element_type=F32)


def _dot_nt(a, b):
    return lax.dot_general(a, b, (((1,), (1,)), ((), ())), preferred_element_type=F32)


def _rope_lane_patterns():
    half = ROT_DIM // 2
    d = jnp.arange(LANES) % HEAD_DIM
    inv_freq = ROPE_THETA ** (-jnp.arange(0, ROT_DIM, 2, dtype=F32) / ROT_DIM)
    freq = jnp.where(d < ROT_DIM, inv_freq[d % half], 0.0)
    m_lo = jnp.where(d < half, -1.0, 0.0)
    m_hi = jnp.where((d >= half) & (d < ROT_DIM), 1.0, 0.0)
    return jnp.stack([freq, m_lo, m_hi]).astype(F32)


def _rope_tables(pos_col, pat):
    ang = pos_col.astype(F32) * pat[0:1]
    sin = jnp.sin(ang)
    return jnp.cos(ang), sin * pat[1:2], sin * pat[2:3]


def _rope(x, cos, sin_lo, sin_hi):
    half = ROT_DIM // 2
    return (x * cos + pltpu.roll(x, LANES - half, axis=1) * sin_lo
            + pltpu.roll(x, half, axis=1) * sin_hi)


KVC_COLS = 2 * N_KV_HEADS * HEAD_DIM
V_ROWS = HEAD_DIM + BF16_ROWS
GATE_ROWS = 16


def _rope_t(xh, cos, sin):
    half = ROT_DIM // 2
    x1, x2 = xh[0:half], xh[half:ROT_DIM]
    return jnp.concatenate([x1 * cos - x2 * sin, x2 * cos + x1 * sin, xh[ROT_DIM:]], axis=0)


def _inproj_kernel(x_ref, g_ref, pos_ref, invf_ref, wn_ref, wt_ref,
                   qt_ref, ks_ref, kw_ref, vst_ref, vwt_ref, ngt_ref, kvc_ref, u_ref):
    tm = x_ref.shape[0]
    h = _rmsnorm(x_ref[...], g_ref[...]).astype(BF16)
    pn = _dot(h, wn_ref[...])
    kvc_ref[...] = pn[:, :KVC_COLS]
    u_ref[...] = pn[:, KVC_COLS:]

    pt = _dot_nt(wt_ref[...], h)
    ang = invf_ref[...] * pos_ref[...].astype(F32)
    cos, sin = jnp.cos(ang), jnp.sin(ang)
    q_scale = (HEAD_DIM ** -0.5) * LOG2E
    for hd in range(N_HEADS):
        rows = slice(hd * HEAD_DIM, (hd + 1) * HEAD_DIM)
        qt_ref[rows, :] = (_rope_t(pt[rows], cos, sin) * q_scale).astype(BF16)

    kv_w = N_KV_HEADS * HEAD_DIM
    r_ks = N_HEADS * HEAD_DIM
    r_kw, r_vs, r_vw, r_ng = r_ks + kv_w, r_ks + 2 * kv_w, r_ks + 3 * kv_w, r_ks + 4 * kv_w
    key = pl.program_id(0) * tm + lax.broadcasted_iota(jnp.int32, (BF16_ROWS, tm), 1)
    blk = lax.broadcasted_iota(jnp.int32, (BF16_ROWS, tm), 0)
    onehot = jnp.where((key % KV_TILE) // SLC_LEN == blk, 1.0, 0.0)
    ones_rows = jnp.where(blk == 0, 1.0, 0.0)
    pad_k = jnp.zeros((LANES - HEAD_DIM - BF16_ROWS, tm), F32)
    pad_w = jnp.zeros((LANES - HEAD_DIM, tm), F32)
    for kh in range(N_KV_HEADS):
        off = kh * HEAD_DIM
        k_s = _rope_t(pt[r_ks + off:r_ks + off + HEAD_DIM], cos, sin)
        ks_ref[kh] = jnp.concatenate([k_s, onehot, pad_k], axis=0).T.astype(BF16)
        k_w = _rope_t(pt[r_kw + off:r_kw + off + HEAD_DIM], cos, sin)
        kw_ref[kh] = jnp.concatenate([k_w, pad_w], axis=0).T.astype(BF16)
        vst_ref[kh] = jnp.concatenate([pt[r_vs + off:r_vs + off + HEAD_DIM], ones_rows], axis=0).astype(BF16)
        vwt_ref[kh] = jnp.concatenate([pt[r_vw + off:r_vw + off + HEAD_DIM], ones_rows], axis=0).astype(BF16)
    ngt_ref[...] = pt[r_ng:r_ng + N_KV_HEADS * GATE_ROWS]


def _inproj(x2, g, pos_row, inv_freq, w_nat, w_t):
    seq, d_model = x2.shape
    tm = ROW_TILE
    n_nat = w_nat.shape[1]
    n_t = w_t.shape[0]
    attn_w = N_HEADS * HEAD_DIM
    row = lambda i: (i, 0)
    col = lambda i: (0, i)
    const = lambda i: (0, 0)
    head_rows = lambda i: (0, i, 0)
    head_cols = lambda i: (0, 0, i)
    return pl.pallas_call(
        _inproj_kernel,
        grid=(seq // tm,),
        in_specs=[
            pl.BlockSpec((tm, d_model), row),
            pl.BlockSpec((1, d_model), const),
            pl.BlockSpec((1, tm), col),
            pl.BlockSpec((ROT_DIM // 2, 1), const),
            pl.BlockSpec((d_model, n_nat), const),
            pl.BlockSpec((n_t, d_model), const),
        ],
        out_specs=[
            pl.BlockSpec((attn_w, tm), col),
            pl.BlockSpec((N_KV_HEADS, tm, LANES), head_rows),
            pl.BlockSpec((N_KV_HEADS, tm, LANES), head_rows),
            pl.BlockSpec((N_KV_HEADS, V_ROWS, tm), head_cols),
            pl.BlockSpec((N_KV_HEADS, V_ROWS, tm), head_cols),
            pl.BlockSpec((N_KV_HEADS * GATE_ROWS, tm), col),
            pl.BlockSpec((tm, KVC_COLS), row),
            pl.BlockSpec((tm, n_nat - KVC_COLS), row),
        ],
        out_shape=[
            jax.ShapeDtypeStruct((attn_w, seq), BF16),
            jax.ShapeDtypeStruct((N_KV_HEADS, seq, LANES), BF16),
            jax.ShapeDtypeStruct((N_KV_HEADS, seq, LANES), BF16),
            jax.ShapeDtypeStruct((N_KV_HEADS, V_ROWS, seq), BF16),
            jax.ShapeDtypeStruct((N_KV_HEADS, V_ROWS, seq), BF16),
            jax.ShapeDtypeStruct((N_KV_HEADS * GATE_ROWS, seq), F32),
            jax.ShapeDtypeStruct((seq, KVC_COLS), F32),
            jax.ShapeDtypeStruct((seq, n_nat - KVC_COLS), F32),
        ],
        compiler_params=pltpu.CompilerParams(
            dimension_semantics=("arbitrary",), vmem_limit_bytes=VMEM_LIMIT_BYTES),
        name="inproj",
    )(x2, g, pos_row, inv_freq, w_nat, w_t)


def _compress_kernel(h_ref, pe_ref, w1_ref, w2_ref, pos_ref, pat_ref, out_ref):
    hh = h_ref[0]
    rows, half_w = hh.shape
    a = _dot((hh + pe_ref[0, 0:1]).astype(BF16), w1_ref[0, :half_w])
    b = _dot((hh + pe_ref[0, 1:2]).astype(BF16), w1_ref[0, half_w:])
    pre = a + pltpu.roll(b, rows - 1, axis=0)
    z = _dot(jax.nn.gelu(pre).astype(BF16), w2_ref[0])
    cos, sin_lo, sin_hi = _rope_tables(pos_ref[...], pat_ref[0])
    out_ref[0] = _rope(z, cos, sin_lo, sin_hi)


def _compress(h4, pe4, w1, w2, cmp_pos_col, pat2):
    n, rows, half_w = h4.shape
    hidden = w1.shape[2]
    per = n // 2
    return pl.pallas_call(
        _compress_kernel,
        grid=(n,),
        in_specs=[
            pl.BlockSpec((1, rows, half_w), lambda i: (i, 0, 0)),
            pl.BlockSpec((1, 2, half_w), lambda i: (i, 0, 0)),
            pl.BlockSpec((1, 2 * half_w, hidden), lambda i: (i // per, 0, 0)),
            pl.BlockSpec((1, hidden, LANES), lambda i: (i // per, 0, 0)),
            pl.BlockSpec((rows, 1), lambda i: (0, 0)),
            pl.BlockSpec((1, 3, LANES), lambda i: (i // per, 0, 0)),
        ],
        out_specs=pl.BlockSpec((1, rows, LANES), lambda i: (i, 0, 0)),
        out_shape=jax.ShapeDtypeStruct((n, rows, LANES), F32),
        compiler_params=pltpu.CompilerParams(
            dimension_semantics=("arbitrary",), vmem_limit_bytes=VMEM_LIMIT_BYTES),
        name="compress",
    )(h4, pe4, w1, w2, cmp_pos_col, pat2)


def _attn_kernel(qt_ref, kc_ref, vct_ref, ks_ref, vst_ref, kw_ref, vwt_ref, ng_ref, wb_ref,
                 out_ref, bias_ref, s_ref, acc_ref, m_ref, *, n_slc, topk):
    c = pl.program_id(1)
    qw = Q_PER_KV * Q_TILE
    t0 = c * Q_TILE
    t_lane = t0 + (lax.broadcasted_iota(jnp.int32, (1, qw), 1) & (Q_TILE - 1))
    qt4 = qt_ref[...]
    qt = jnp.concatenate([qt4[g * HEAD_DIM:(g + 1) * HEAD_DIM] for g in range(Q_PER_KV)], axis=1)

    def lanes4(x):
        return jnp.concatenate([x] * Q_PER_KV, axis=1)

    n_rows = CMP_PER_SLC * n_slc
    s_c = _dot(kc_ref[0], qt)
    row = lax.broadcasted_iota(jnp.int32, (n_rows, 1), 0)
    n_idx = CMP_PER_SLC * (row % n_slc) + row // n_slc
    first_q = (n_idx * CMP_STRIDE + (CMP_LEN - 1) - t0).astype(F32)
    q_lane = (lax.broadcasted_iota(jnp.int32, (1, qw), 1) & (Q_TILE - 1)).astype(F32)
    s_c = jnp.where(first_q <= q_lane, s_c, NEG_INF)
    m_c = jnp.max(s_c, axis=0, keepdims=True)
    e_c = jnp.exp2(s_c - m_c)
    inv_c = jnp.where(m_c > 0.5 * NEG_INF, 1.0 / jnp.maximum(jnp.sum(e_c, axis=0, keepdims=True), 1e-30), 0.0)
    o_c = _dot(vct_ref[0], e_c.astype(BF16)) * inv_c

    p_c = e_c * inv_c
    psum = p_c[:, 0:Q_TILE]
    for g in range(1, Q_PER_KV):
        psum = psum + p_c[:, g * Q_TILE:(g + 1) * Q_TILE]
    j_idx = lax.broadcasted_iota(jnp.int32, (n_slc, 1), 0)
    straddle = psum[3 * n_slc:4 * n_slc]
    from_prev = jnp.where(j_idx == 0, 0.0, pltpu.roll(straddle, 1, axis=0))
    imp = (psum[0:n_slc] + psum[n_slc:2 * n_slc] + psum[2 * n_slc:3 * n_slc]
           + 0.5 * straddle + 0.5 * from_prev)

    span = WINDOW + Q_TILE
    w0 = pl.multiple_of(jnp.maximum(t0 - WINDOW, 0), Q_TILE)
    rhs_w = jnp.concatenate([qt, jnp.zeros((LANES - HEAD_DIM, qw), BF16)], axis=0)
    wb_off = pl.multiple_of(WINDOW - (t0 - w0), Q_TILE)
    s_w = _dot(kw_ref[0, pl.ds(w0, span), :], rhs_w) + lanes4(wb_ref[pl.ds(wb_off, span), :])
    e_w = jnp.exp2(s_w - jnp.max(s_w, axis=0, keepdims=True)).astype(BF16)
    acc_w = _dot(vwt_ref[0, :, pl.ds(w0, span)], e_w)
    o_w = acc_w[:HEAD_DIM] * (1.0 / jnp.maximum(acc_w[HEAD_DIM:HEAD_DIM + 1], 1e-30))

    tq = t0 + lax.broadcasted_iota(jnp.int32, (1, Q_TILE), 1)
    forced = (j_idx == 0) | (j_idx == tq // SLC_LEN)
    valid = (j_idx * SLC_LEN) <= tq
    score = jnp.where(valid, jnp.where(forced, FORCE_SCORE, imp), NEG_INF)
    j_f = j_idx.astype(F32)
    for _ in range(topk):
        best = jnp.max(score, axis=0, keepdims=True)
        first = jnp.min(jnp.where(score == best, j_f, float(n_slc)), axis=0, keepdims=True)
        score = jnp.where(j_f == first, -jnp.inf, score)
    blocks_per_tile = KV_TILE // SLC_LEN
    n_tiles = n_slc // blocks_per_tile
    bias = jnp.where(valid & (score == -jnp.inf), 0.0, NEG_INF)
    bias_ref[:, 0:blocks_per_tile, :] = bias.reshape(n_tiles, blocks_per_tile, Q_TILE)
    bias_ref[:, blocks_per_tile:, :] = jnp.zeros((n_tiles, BF16_ROWS - blocks_per_tile, Q_TILE), F32)

    zeros_tail = jnp.zeros((LANES - HEAD_DIM - BF16_ROWS, qw), BF16)

    def scores(slot, j):
        k0 = pl.multiple_of(j * KV_TILE, KV_TILE)
        rhs = jnp.concatenate([qt, lanes4(bias_ref[j]).astype(BF16), zeros_tail], axis=0)
        s_ref[slot] = _dot(ks_ref[0, pl.ds(k0, KV_TILE), :], rhs)

    def accumulate(s, j):
        k0 = pl.multiple_of(j * KV_TILE, KV_TILE)
        m_old = m_ref[...]
        m_new = jnp.maximum(m_old, jnp.max(s, axis=0, keepdims=True))
        p = jnp.exp2(s - m_new).astype(BF16)
        acc_ref[...] = jnp.exp2(m_old - m_new) * acc_ref[...] + _dot(vst_ref[0, :, pl.ds(k0, KV_TILE)], p)
        m_ref[...] = m_new

    m_ref[...] = jnp.full((1, qw), NEG_INF, F32)
    acc_ref[...] = jnp.zeros((V_ROWS, qw), F32)
    n_full = t0 // KV_TILE
    scores(0, 0)

    def pair(i, _):
        j = 2 * i
        scores(1, j + 1)
        accumulate(s_ref[0], j)
        scores(0, j + 2)
        accumulate(s_ref[1], j + 1)
        return 0

    lax.fori_loop(0, n_full // 2, pair, 0)
    odd = n_full % 2

    @pl.when(odd == 1)
    def _():
        scores(1, n_full)
        accumulate(s_ref[0], n_full - 1)

    tok = n_full * KV_TILE + lax.broadcasted_iota(jnp.int32, (KV_TILE, 1), 0)
    accumulate(jnp.where(tok <= t_lane, s_ref[odd], NEG_INF), n_full)
    acc_s = acc_ref[...]
    o_s = acc_s[:HEAD_DIM] * (1.0 / jnp.maximum(acc_s[HEAD_DIM:HEAD_DIM + 1], 1e-30))

    gates = jax.nn.sigmoid(ng_ref[...])

    def gate(b):
        return jnp.concatenate([gates[b * Q_PER_KV + g:b * Q_PER_KV + g + 1] for g in range(Q_PER_KV)], axis=1)

    o = gate(0) * o_c + gate(1) * o_s + gate(2) * o_w
    pairs = []
    for g in range(0, Q_PER_KV, 2):
        two = jnp.concatenate([o[:, g * Q_TILE:(g + 1) * Q_TILE], o[:, (g + 1) * Q_TILE:(g + 2) * Q_TILE]], axis=0)
        pairs.append(two.T)
    out_ref[...] = jnp.concatenate(pairs, axis=1).astype(out_ref.dtype)


def _window_bias_table():
    r = jnp.arange(2 * WINDOW + Q_TILE, dtype=jnp.int32)[:, None]
    q = jnp.arange(Q_TILE, dtype=jnp.int32)[None, :]
    return jnp.where((r > q) & (r <= WINDOW + q), 0.0, NEG_INF).astype(F32)


def _attention(qt, kc, vct, ks, vst, kw, vwt, ngt, seq):
    wb = _window_bias_table()
    n_slc = seq // SLC_LEN
    nb = seq // Q_TILE
    qw = Q_PER_KV * Q_TILE
    n_rows = CMP_PER_SLC * n_slc
    topk = min(SLC_TOPK, n_slc)

    def per_head(block):
        return pl.BlockSpec(block, lambda k, c: (k, 0, 0), pipeline_mode=pl.Buffered(1))

    return pl.pallas_call(
        functools.partial(_attn_kernel, n_slc=n_slc, topk=topk),
        grid=(N_KV_HEADS, nb),
        in_specs=[
            pl.BlockSpec((Q_PER_KV * HEAD_DIM, Q_TILE), lambda k, c: (k, c)),
            per_head((1, n_rows, HEAD_DIM)),
            per_head((1, HEAD_DIM, n_rows)),
            per_head((1, seq, LANES)),
            per_head((1, V_ROWS, seq)),
            per_head((1, seq, LANES)),
            per_head((1, V_ROWS, seq)),
            pl.BlockSpec((GATE_ROWS, Q_TILE), lambda k, c: (k, c)),
            pl.BlockSpec(wb.shape, lambda k, c: (0, 0), pipeline_mode=pl.Buffered(1)),
        ],
        out_specs=pl.BlockSpec((Q_TILE, Q_PER_KV * HEAD_DIM), lambda k, c: (c, k)),
        out_shape=jax.ShapeDtypeStruct((seq, N_HEADS * HEAD_DIM), BF16),
        scratch_shapes=[
            pltpu.VMEM((n_slc * SLC_LEN // KV_TILE, BF16_ROWS, Q_TILE), F32),
            pltpu.VMEM((SCORE_SLOTS, KV_TILE, qw), F32),
            pltpu.VMEM((V_ROWS, qw), F32),
            pltpu.VMEM((1, qw), F32),
        ],
        compiler_params=pltpu.CompilerParams(
            dimension_semantics=("arbitrary", "arbitrary"), vmem_limit_bytes=VMEM_LIMIT_BYTES),
        name="nsa_attention",
    )(qt, kc, vct, ks, vst, kw, vwt, ngt, wb)


def _discretize_kernel(are_ref, aim_ref, ldt_ref, bre_ref, bim_ref, ar_ref, ai_ref, br_ref, bi_ref):
    a_re, a_im = are_ref[...], aim_ref[...]
    dt = jnp.exp(ldt_ref[...])
    mag = jnp.exp(a_re * dt)
    ang = a_im * dt
    ab_re, ab_im = mag * jnp.cos(ang), mag * jnp.sin(ang)
    n_re, n_im = ab_re - 1.0, ab_im
    den = a_re * a_re + a_im * a_im
    c_re = (n_re * a_re + n_im * a_im) / den
    c_im = (n_im * a_re - n_re * a_im) / den
    b_re, b_im = bre_ref[...], bim_ref[...]
    ar_ref[...] = ab_re
    ai_ref[...] = ab_im
    br_ref[...] = c_re * b_re - c_im * b_im
    bi_ref[...] = c_re * b_im + c_im * b_re


def _discretize(a_re, a_im, log_dt, b_re, b_im):
    g, p, h = b_re.shape
    col = jax.ShapeDtypeStruct((g, p, 1), F32)
    full = jax.ShapeDtypeStruct((g, p, h), F32)
    return pl.pallas_call(
        _discretize_kernel, out_shape=[col, col, full, full], name="ssm_discretize",
    )(a_re[..., None], a_im[..., None], log_dt[:, None, None], b_re, b_im)


def _cmul(ar, ai, br, bi):
    return ar * br - ai * bi, ar * bi + ai * br


def _s5_kernel(u_ref, bblk_ref, a_ref, cblk_ref, d_ref, wglu_ref, y_ref, xr_ref, xi_ref, st_ref, pw_ref):
    seg = SCAN_SEGMENTS
    steps = xr_ref.shape[0]
    n_state = xr_ref.shape[2]
    n_blk, blk_in, blk_state2 = bblk_ref.shape
    blk_state = blk_state2 // 2

    @pl.when(pl.program_id(0) == 0)
    def _():
        st_ref[...] = jnp.zeros_like(st_ref)
        wr, wi = a_ref[0:1, :], a_ref[1:2, :]
        for j in range(steps):
            pw_ref[j, 0] = jnp.broadcast_to(wr, (seg, n_state))
            pw_ref[j, 1] = jnp.broadcast_to(wi, (seg, n_state))
            wr, wi = _cmul(wr, wi, a_ref[0:1, :], a_ref[1:2, :])

    u = u_ref[...]
    ub = u.astype(BF16)
    for k in range(n_blk):
        bu = _dot(ub[:, k * blk_in:(k + 1) * blk_in], bblk_ref[k])
        xr_ref[:, :, k * blk_state:(k + 1) * blk_state] = bu[:, :blk_state].reshape(steps, seg, blk_state)
        xi_ref[:, :, k * blk_state:(k + 1) * blk_state] = bu[:, blk_state:].reshape(steps, seg, blk_state)

    lb = SCAN_LANE_BLOCK
    for b0 in range(0, n_state, lb):
        lanes = slice(b0, b0 + lb)
        a_re = a_ref[0:1, lanes]
        a_im = a_ref[1:2, lanes]
        ar8 = jnp.broadcast_to(a_re, (seg, lb))
        ai8 = jnp.broadcast_to(a_im, (seg, lb))

        def local(j, carry):
            xr, xi = carry
            pr, pi = _cmul(ar8, ai8, xr, xi)
            xr = pr + xr_ref[j, :, lanes]
            xi = pi + xi_ref[j, :, lanes]
            xr_ref[j, :, lanes] = xr
            xi_ref[j, :, lanes] = xi
            return xr, xi

        end_r, end_i = lax.fori_loop(0, steps, local, (jnp.zeros((seg, lb), F32), jnp.zeros((seg, lb), F32)))

        pr, pi = a_re, a_im
        for _ in range(steps.bit_length() - 1):
            pr, pi = _cmul(pr, pi, pr, pi)
        cr, ci = st_ref[0:1, lanes], st_ref[1:2, lanes]
        in_r, in_i = [], []
        for s in range(seg):
            in_r.append(cr)
            in_i.append(ci)
            qr, qi = _cmul(pr, pi, cr, ci)
            cr, ci = qr + end_r[s:s + 1], qi + end_i[s:s + 1]
        st_ref[0:1, lanes] = cr
        st_ref[1:2, lanes] = ci
        in_r = jnp.concatenate(in_r, axis=0)
        in_i = jnp.concatenate(in_i, axis=0)

        def fix(j, carry):
            fr, fi = _cmul(pw_ref[j, 0, :, lanes], pw_ref[j, 1, :, lanes], in_r, in_i)
            xr_ref[j, :, lanes] = xr_ref[j, :, lanes] + fr
            xi_ref[j, :, lanes] = xi_ref[j, :, lanes] + fi
            return carry

        lax.fori_loop(0, steps, fix, 0)

    rows = steps * seg
    y = d_ref[...] * u
    for k in range(n_blk):
        st = slice(k * blk_state, (k + 1) * blk_state)
        xs = jnp.concatenate([xr_ref[:, :, st].reshape(rows, blk_state),
                              xi_ref[:, :, st].reshape(rows, blk_state)], axis=1).astype(BF16)
        yk = _dot(xs, cblk_ref[k])
        if k == 0:
            ys = [yk]
        else:
            ys.append(yk)
    y = y + jnp.concatenate(ys, axis=1)
    y = jax.nn.gelu(y)
    y = y * jax.nn.sigmoid(_dot(y.astype(BF16), wglu_ref[...]))
    y_ref[...] = y.astype(y_ref.dtype)


def _s5(u_perm, bblk, a_lane, cblk, d_row, w_glu):
    seq, width = u_perm.shape
    tm = ROW_TILE
    steps = tm // SCAN_SEGMENTS
    n_state = a_lane.shape[1]
    const2 = lambda i: (0, 0)
    const3 = lambda i: (0, 0, 0)
    return pl.pallas_call(
        _s5_kernel,
        grid=(seq // tm,),
        in_specs=[
            pl.BlockSpec((tm, width), lambda i: (i, 0)),
            pl.BlockSpec(bblk.shape, const3),
            pl.BlockSpec(a_lane.shape, const2),
            pl.BlockSpec(cblk.shape, const3),
            pl.BlockSpec((1, width), const2),
            pl.BlockSpec((width, width), const2),
        ],
        out_specs=pl.BlockSpec((tm, width), lambda i: (i, 0)),
        out_shape=jax.ShapeDtypeStruct((seq, width), BF16),
        scratch_shapes=[
            pltpu.VMEM((steps, SCAN_SEGMENTS, n_state), F32),
            pltpu.VMEM((steps, SCAN_SEGMENTS, n_state), F32),
            pltpu.VMEM((2, n_state), F32),
            pltpu.VMEM((steps, 2, SCAN_SEGMENTS, n_state), F32),
        ],
        compiler_params=pltpu.CompilerParams(
            dimension_semantics=("arbitrary",), vmem_limit_bytes=VMEM_LIMIT_BYTES),
        name="s5_scan",
    )(u_perm, bblk, a_lane, cblk, d_row, w_glu)


def _ffn_kernel(x_ref, ya_ref, yb_ref, gmix_ref, wg_ref, wpa_ref, wpb_ref, wout_ref, gffn_ref,
                wup_ref, cw_ref, cb_ref, wdown_ref, gfin_ref, o_ref, tail_ref, *, d_ff, chunk):
    tm, d_model = x_ref.shape

    @pl.when(pl.program_id(0) == 0)
    def _():
        tail_ref[...] = jnp.zeros_like(tail_ref)

    x = x_ref[...]
    h = _rmsnorm(x, gmix_ref[...]).astype(BF16)
    gates = jax.nn.sigmoid(_dot(h, wg_ref[...]))
    merged = (gates[:, :d_model] * _dot(ya_ref[...], wpa_ref[...])
              + gates[:, d_model:] * _dot(yb_ref[...], wpb_ref[...]))
    x1 = x + _dot(merged.astype(BF16), wout_ref[...])
    h2 = _rmsnorm(x1, gffn_ref[...]).astype(BF16)

    def conv_cols(c0):
        cols = slice(c0, c0 + chunk)
        up = _dot(h2, wup_ref[:, cols])
        ext = jnp.concatenate([tail_ref[:, cols], up], axis=0)
        tail_ref[:, cols] = up[tm - SUBLANES:, :]
        prev1 = pltpu.roll(ext, 1, axis=0)[SUBLANES:]
        prev2 = pltpu.roll(ext, 2, axis=0)[SUBLANES:]
        return cb_ref[:, cols] + cw_ref[0:1, cols] * prev2 + cw_ref[1:2, cols] * prev1 + cw_ref[2:3, cols] * up

    acc = x1
    for k in range(d_ff // chunk):
        gate = conv_cols(k * chunk)
        val = conv_cols(d_ff + k * chunk)
        act = (jax.nn.silu(gate) * val).astype(BF16)
        acc = acc + _dot(act, wdown_ref[k * chunk:(k + 1) * chunk, :])
    o_ref[...] = _rmsnorm(acc, gfin_ref[...])


def _ffn(x2, ya, yb, g_mix, w_g, w_pa, w_pb, w_out, g_ffn, w_up, conv_w, conv_b, w_down, g_fin):
    seq, d_model = x2.shape
    d_ff = w_down.shape[0]
    tm = FFN_ROW_TILE
    chunk = FFN_COL_CHUNK
    assert d_ff % chunk == 0 and chunk % LANES == 0
    row = lambda i: (i, 0)

    def resident(arr):
        return pl.BlockSpec(arr.shape, lambda i: (0, 0), pipeline_mode=pl.Buffered(1))

    return pl.pallas_call(
        functools.partial(_ffn_kernel, d_ff=d_ff, chunk=chunk),
        grid=(seq // tm,),
        in_specs=[
            pl.BlockSpec((tm, d_model), row),
            pl.BlockSpec((tm, ya.shape[1]), row),
            pl.BlockSpec((tm, yb.shape[1]), row),
            resident(g_mix), resident(w_g), resident(w_pa), resident(w_pb), resident(w_out), resident(g_ffn),
            resident(w_up), resident(conv_w), resident(conv_b), resident(w_down), resident(g_fin),
        ],
        out_specs=pl.BlockSpec((tm, d_model), row),
        out_shape=jax.ShapeDtypeStruct((seq, d_model), F32),
        scratch_shapes=[pltpu.VMEM((SUBLANES, 2 * d_ff), F32)],
        compiler_params=pltpu.CompilerParams(
            dimension_semantics=("arbitrary",), vmem_limit_bytes=VMEM_LIMIT_BYTES),
        name="merge_ffn",
    )(x2, ya, yb, g_mix, w_g, w_pa, w_pb, w_out, g_ffn, w_up, conv_w, conv_b, w_down, g_fin)


def _block_diag(m, n_blk):
    g, r, c = m.shape
    per = g // n_blk
    eye = jnp.eye(per, dtype=m.dtype)
    m = m.reshape(n_blk, per, r, c)
    return jnp.einsum("bgrc,gk->bgrkc", m, eye).reshape(n_blk, per * r, per * c)


def kernel(x, positions, norm_mix, w_in, cmp_k_pe, cmp_k_w1, cmp_k_w2, cmp_v_pe, cmp_v_w1, cmp_v_w2, ssm_a_re, ssm_a_im, ssm_b_re, ssm_b_im, ssm_c_re, ssm_c_im, ssm_d, ssm_log_dt, ssm_w_glu, w_proj_a, w_proj_b, w_out, norm_ffn, w_up, conv_w, conv_b, w_down, norm_final):
    bsz, seq, d_model = x.shape
    assert bsz == 1 and norm_mix.shape[0] == 1
    assert seq % (2 * KV_TILE) == 0 and ROW_TILE % KV_TILE == 0 and seq >= WINDOW + Q_TILE
    attn_w = N_HEADS * HEAD_DIM
    kv_w = N_KV_HEADS * HEAD_DIM
    ssm_width = ssm_d.shape[1]
    x2 = x[0]
    pos = positions[0]

    sizes = (attn_w,) + (kv_w,) * 6 + (3 * N_HEADS, ssm_width, d_model, d_model)
    offs = [0]
    for s in sizes:
        offs.append(offs[-1] + s)
    w = w_in[0]
    col = lambda i: w[:, offs[i]:offs[i + 1]]
    w_q, w_kc, w_vc, w_ks, w_vs, w_kw, w_vw, w_ng, w_u, w_ga, w_gb = [col(i) for i in range(11)]
    w_ng = w_ng.reshape(d_model, N_KV_HEADS, Q_PER_KV, 3).transpose(0, 1, 3, 2).reshape(d_model, N_KV_HEADS, 3 * Q_PER_KV)
    w_ng = jnp.pad(w_ng, ((0, 0), (0, 0), (0, GATE_ROWS - 3 * Q_PER_KV))).reshape(d_model, N_KV_HEADS * GATE_ROWS)
    w_t = jnp.concatenate([w_q, w_ks, w_kw, w_vs, w_vw, w_ng], axis=1).T.astype(BF16)
    w_nat = jnp.concatenate([w_kc, w_vc, w_u], axis=1).astype(BF16)
    w_gate = jnp.concatenate([w_ga, w_gb], axis=1).astype(BF16)

    inv_freq = ROPE_THETA ** (-jnp.arange(0, ROT_DIM, 2, dtype=F32) / ROT_DIM)
    qt, ks, kw, vst, vwt, ngt, kvc, u = _inproj(x2, norm_mix, pos[None, :], inv_freq[:, None], w_nat, w_t)

    n_half = seq // CMP_STRIDE
    n_cmp = (seq - CMP_LEN) // CMP_STRIDE + 1
    h4 = kvc.reshape(seq, 2 * N_KV_HEADS, HEAD_DIM).transpose(1, 0, 2).reshape(
        2 * N_KV_HEADS, n_half, CMP_STRIDE * HEAD_DIM)
    halves = lambda pe: pe.reshape(2, CMP_STRIDE * HEAD_DIM)
    pe4 = jnp.stack([halves(cmp_k_pe[0])] * N_KV_HEADS + [halves(cmp_v_pe[0])] * N_KV_HEADS)
    w1 = jnp.stack([cmp_k_w1[0], cmp_v_w1[0]]).astype(BF16)
    w2 = jnp.pad(jnp.stack([cmp_k_w2[0], cmp_v_w2[0]]), ((0, 0), (0, 0), (0, LANES - HEAD_DIM))).astype(BF16)
    cmp_end = jnp.arange(n_cmp, dtype=jnp.int32) * CMP_STRIDE + CMP_LEN - 1
    cmp_pos = jnp.pad(pos[cmp_end], (0, n_half - n_cmp))
    pat = _rope_lane_patterns()
    pat2 = jnp.stack([pat, jnp.zeros_like(pat)])
    cmp = _compress(h4, pe4, w1, w2, cmp_pos[:, None], pat2)[:, :, :HEAD_DIM]
    n_slc = seq // SLC_LEN
    cmp = cmp.reshape(2 * N_KV_HEADS, n_slc, CMP_PER_SLC, HEAD_DIM).transpose(0, 2, 1, 3).reshape(
        2 * N_KV_HEADS, n_half, HEAD_DIM).astype(BF16)
    kc = cmp[:N_KV_HEADS]
    vct = cmp[N_KV_HEADS:].transpose(0, 2, 1)

    y_a = _attention(qt, kc, vct, ks, vst, kw, vwt, ngt, seq)

    ab_re, ab_im, bb_re, bb_im = _discretize(ssm_a_re[0], ssm_a_im[0], ssm_log_dt[0], ssm_b_re[0], ssm_b_im[0])
    n_blk = 2
    a_lane = jnp.stack([ab_re.reshape(-1), ab_im.reshape(-1)])
    to_in = lambda b: _block_diag(b.transpose(0, 2, 1), n_blk)
    bblk = jnp.concatenate([to_in(bb_re), to_in(bb_im)], axis=2).astype(BF16)
    to_out = lambda cm: _block_diag(cm.transpose(0, 2, 1), n_blk)
    cblk = jnp.concatenate([to_out(ssm_c_re[0]), -to_out(ssm_c_im[0])], axis=1).astype(BF16)
    steps = ROW_TILE // SCAN_SEGMENTS
    nt = seq // ROW_TILE
    u_perm = u.reshape(nt, SCAN_SEGMENTS, steps, ssm_width).transpose(0, 2, 1, 3).reshape(seq, ssm_width)
    y_b = _s5(u_perm, bblk, a_lane, cblk, ssm_d, ssm_w_glu[0].astype(BF16))
    y_b = y_b.reshape(nt, steps, SCAN_SEGMENTS, ssm_width).transpose(0, 2, 1, 3).reshape(seq, ssm_width)

    out = _ffn(x2, y_a, y_b, norm_mix, w_gate, w_proj_a[0].astype(BF16), w_proj_b[0].astype(BF16),
               w_out[0].astype(BF16), norm_ffn, w_up[0].astype(BF16), conv_w[0], conv_b, w_down[0].astype(BF16),
               norm_final[None, :])
    return out[None]
```

```python
import functools

import jax
import jax.numpy as jnp
from jax import lax
from jax.experimental import pallas as pl
from jax.experimental.pallas import tpu as pltpu

F32 = jnp.float32
BF16 = jnp.bfloat16

N_HEADS = 8
N_KV_HEADS = 2
HEAD_DIM = 64
Q_PER_KV = N_HEADS // N_KV_HEADS
ROT_DIM = HEAD_DIM // 4
ROPE_THETA = 500000.0
CMP_LEN = 32
CMP_STRIDE = 16
CMP_PER_SLC = 4
SLC_LEN = 64
SLC_TOPK = 16
WINDOW = 512
SSM_GROUP = 16
SSM_STATE = 64
CONV_WIDTH = 3
EPS = 1e-6
NEG_INF = -1e30
FORCE_SCORE = 1e4
LOG2E = 1.4426950408889634

LANES = 128
SUBLANES = 8
BF16_ROWS = 16
VMEM_LIMIT_BYTES = 56 * 1024 * 1024

ROW_TILE = 512
FFN_ROW_TILE = 256
FFN_COL_CHUNK = 1408
Q_TILE = 256
KV_TILE = 512
SCORE_SLOTS = 2
FRONT_CLASSES = 4
SCAN_SEGMENTS = SUBLANES
SCAN_LANE_BLOCK = 512


def _rmsnorm(x, g):
    return x * lax.rsqrt(jnp.mean(x * x, axis=-1, keepdims=True) + EPS) * g


def _dot(a, b):
    return jnp.dot(a, b, preferred_element_type=F32)


def _dot_nt(a, b):
    return lax.dot_general(a, b, (((1,), (1,)), ((), ())), preferred_element_type=F32)


def _rope_lane_patterns():
    half = ROT_DIM // 2
    d = jnp.arange(LANES) % HEAD_DIM
    inv_freq = ROPE_THETA ** (-jnp.arange(0, ROT_DIM, 2, dtype=F32) / ROT_DIM)
    freq = jnp.where(d < ROT_DIM, inv_freq[d % half], 0.0)
    m_lo = jnp.where(d < half, -1.0, 0.0)
    m_hi = jnp.where((d >= half) & (d < ROT_DIM), 1.0, 0.0)
    return jnp.stack([freq, m_lo, m_hi]).astype(F32)


def _rope_tables(pos_col, pat):
    ang = pos_col.astype(F32) * pat[0:1]
    sin = jnp.sin(ang)
    return jnp.cos(ang), sin * pat[1:2], sin * pat[2:3]


def _rope(x, cos, sin_lo, sin_hi):
    half = ROT_DIM // 2
    return (x * cos + pltpu.roll(x, LANES - half, axis=1) * sin_lo
            + pltpu.roll(x, half, axis=1) * sin_hi)


KVC_COLS = 2 * N_KV_HEADS * HEAD_DIM
V_ROWS = HEAD_DIM + BF16_ROWS
GATE_ROWS = 16


def _rope_t(xh, cos, sin):
    half = ROT_DIM // 2
    x1, x2 = xh[0:half], xh[half:ROT_DIM]
    return jnp.concatenate([x1 * cos - x2 * sin, x2 * cos + x1 * sin, xh[ROT_DIM:]], axis=0)


def _inproj_kernel(x_ref, g_ref, pos_ref, invf_ref, wn_ref, wt_ref,
                   qt_ref, ks_ref, kw_ref, vst_ref, vwt_ref, ngt_ref, kvc_ref, u_ref):
    tm = x_ref.shape[0]
    h = _rmsnorm(x_ref[...], g_ref[...]).astype(BF16)
    pn = _dot(h, wn_ref[...])
    kvc_ref[...] = pn[:, :KVC_COLS]
    u_ref[...] = pn[:, KVC_COLS:]

    pt = _dot_nt(wt_ref[...], h)
    ang = invf_ref[...] * pos_ref[...].astype(F32)
    cos, sin = jnp.cos(ang), jnp.sin(ang)
    q_scale = (HEAD_DIM ** -0.5) * LOG2E
    for hd in range(N_HEADS):
        rows = slice(hd * HEAD_DIM, (hd + 1) * HEAD_DIM)
        qt_ref[rows, :] = (_rope_t(pt[rows], cos, sin) * q_scale).astype(BF16)

    kv_w = N_KV_HEADS * HEAD_DIM
    r_ks = N_HEADS * HEAD_DIM
    r_kw, r_vs, r_vw, r_ng = r_ks + kv_w, r_ks + 2 * kv_w, r_ks + 3 * kv_w, r_ks + 4 * kv_w
    key = pl.program_id(0) * tm + lax.broadcasted_iota(jnp.int32, (BF16_ROWS, tm), 1)
    blk = lax.broadcasted_iota(jnp.int32, (BF16_ROWS, tm), 0)
    onehot = jnp.where((key % KV_TILE) // SLC_LEN == blk, 1.0, 0.0)
    ones_rows = jnp.where(blk == 0, 1.0, 0.0)
    pad_k = jnp.zeros((LANES - HEAD_DIM - BF16_ROWS, tm), F32)
    pad_w = jnp.zeros((LANES - HEAD_DIM, tm), F32)
    for kh in range(N_KV_HEADS):
        off = kh * HEAD_DIM
        k_s = _rope_t(pt[r_ks + off:r_ks + off + HEAD_DIM], cos, sin)
        ks_ref[kh] = jnp.concatenate([k_s, onehot, pad_k], axis=0).T.astype(BF16)
        k_w = _rope_t(pt[r_kw + off:r_kw + off + HEAD_DIM], cos, sin)
        kw_ref[kh] = jnp.concatenate([k_w, pad_w], axis=0).T.astype(BF16)
        vst_ref[kh] = jnp.concatenate([pt[r_vs + off:r_vs + off + HEAD_DIM], ones_rows], axis=0).astype(BF16)
        vwt_ref[kh] = jnp.concatenate([pt[r_vw + off:r_vw + off + HEAD_DIM], ones_rows], axis=0).astype(BF16)
    ngt_ref[...] = pt[r_ng:r_ng + N_KV_HEADS * GATE_ROWS]


def _inproj(x2, g, pos_row, inv_freq, w_nat, w_t):
    seq, d_model = x2.shape
    tm = ROW_TILE
    n_nat = w_nat.shape[1]
    n_t = w_t.shape[0]
    attn_w = N_HEADS * HEAD_DIM
    row = lambda i: (i, 0)
    col = lambda i: (0, i)
    const = lambda i: (0, 0)
    head_rows = lambda i: (0, i, 0)
    head_cols = lambda i: (0, 0, i)
    return pl.pallas_call(
        _inproj_kernel,
        grid=(seq // tm,),
        in_specs=[
            pl.BlockSpec((tm, d_model), row),
            pl.BlockSpec((1, d_model), const),
            pl.BlockSpec((1, tm), col),
            pl.BlockSpec((ROT_DIM // 2, 1), const),
            pl.BlockSpec((d_model, n_nat), const),
            pl.BlockSpec((n_t, d_model), const),
        ],
        out_specs=[
            pl.BlockSpec((attn_w, tm), col),
            pl.BlockSpec((N_KV_HEADS, tm, LANES), head_rows),
            pl.BlockSpec((N_KV_HEADS, tm, LANES), head_rows),
            pl.BlockSpec((N_KV_HEADS, V_ROWS, tm), head_cols),
            pl.BlockSpec((N_KV_HEADS, V_ROWS, tm), head_cols),
            pl.BlockSpec((N_KV_HEADS * GATE_ROWS, tm), col),
            pl.BlockSpec((tm, KVC_COLS), row),
            pl.BlockSpec((tm, n_nat - KVC_COLS), row),
        ],
        out_shape=[
            jax.ShapeDtypeStruct((attn_w, seq), BF16),
            jax.ShapeDtypeStruct((N_KV_HEADS, seq, LANES), BF16),
            jax.ShapeDtypeStruct((N_KV_HEADS, seq, LANES), BF16),
            jax.ShapeDtypeStruct((N_KV_HEADS, V_ROWS, seq), BF16),
            jax.ShapeDtypeStruct((N_KV_HEADS, V_ROWS, seq), BF16),
            jax.ShapeDtypeStruct((N_KV_HEADS * GATE_ROWS, seq), F32),
            jax.ShapeDtypeStruct((seq, KVC_COLS), F32),
            jax.ShapeDtypeStruct((seq, n_nat - KVC_COLS), F32),
        ],
        compiler_params=pltpu.CompilerParams(
            dimension_semantics=("arbitrary",), vmem_limit_bytes=VMEM_LIMIT_BYTES),
        name="inproj",
    )(x2, g, pos_row, inv_freq, w_nat, w_t)


def _compress_kernel(h_ref, pe_ref, w1_ref, w2_ref, pos_ref, pat_ref, out_ref):
    hh = h_ref[0]
    rows, half_w = hh.shape
    a = _dot((hh + pe_ref[0, 0:1]).astype(BF16), w1_ref[0, :half_w])
    b = _dot((hh + pe_ref[0, 1:2]).astype(BF16), w1_ref[0, half_w:])
    pre = a + pltpu.roll(b, rows - 1, axis=0)
    z = _dot(jax.nn.gelu(pre).astype(BF16), w2_ref[0])
    cos, sin_lo, sin_hi = _rope_tables(pos_ref[...], pat_ref[0])
    out_ref[0] = _rope(z, cos, sin_lo, sin_hi)


def _compress(h4, pe4, w1, w2, cmp_pos_col, pat2):
    n, rows, half_w = h4.shape
    hidden = w1.shape[2]
    per = n // 2
    return pl.pallas_call(
        _compress_kernel,
        grid=(n,),
        in_specs=[
            pl.BlockSpec((1, rows, half_w), lambda i: (i, 0, 0)),
            pl.BlockSpec((1, 2, half_w), lambda i: (i, 0, 0)),
            pl.BlockSpec((1, 2 * half_w, hidden), lambda i: (i // per, 0, 0)),
            pl.BlockSpec((1, hidden, LANES), lambda i: (i // per, 0, 0)),
            pl.BlockSpec((rows, 1), lambda i: (0, 0)),
            pl.BlockSpec((1, 3, LANES), lambda i: (i // per, 0, 0)),
        ],
        out_specs=pl.BlockSpec((1, rows, LANES), lambda i: (i, 0, 0)),
        out_shape=jax.ShapeDtypeStruct((n, rows, LANES), F32),
        compiler_params=pltpu.CompilerParams(
            dimension_semantics=("arbitrary",), vmem_limit_bytes=VMEM_LIMIT_BYTES),
        name="compress",
    )(h4, pe4, w1, w2, cmp_pos_col, pat2)


def _attn_kernel(qt_ref, kc_ref, vct_ref, ks_ref, vst_ref, kw_ref, vwt_ref, ng_ref, wb_ref,
                 out_ref, bias_ref, s_ref, acc_ref, m_ref, oc_ref, ow_ref, *, n_slc, topk):
    c = pl.program_id(1)
    qw = Q_PER_KV * Q_TILE
    t0 = c * Q_TILE
    t_lane = t0 + (lax.broadcasted_iota(jnp.int32, (1, qw), 1) & (Q_TILE - 1))
    qt4 = qt_ref[...]
    qt = jnp.concatenate([qt4[g * HEAD_DIM:(g + 1) * HEAD_DIM] for g in range(Q_PER_KV)], axis=1)

    def lanes4(x):
        return jnp.concatenate([x] * Q_PER_KV, axis=1)

    blocks_per_tile = KV_TILE // SLC_LEN
    q_lane = (lax.broadcasted_iota(jnp.int32, (1, qw), 1) & (Q_TILE - 1)).astype(F32)
    tq = t0 + lax.broadcasted_iota(jnp.int32, (1, Q_TILE), 1)

    def front(nj):
        rows = CMP_PER_SLC * nj
        kc = jnp.concatenate([kc_ref[0, r * n_slc:r * n_slc + nj, :] for r in range(CMP_PER_SLC)], axis=0)
        s_c = _dot(kc, qt)
        row = lax.broadcasted_iota(jnp.int32, (rows, 1), 0)
        n_idx = CMP_PER_SLC * (row % nj) + row // nj
        first_q = (n_idx * CMP_STRIDE + (CMP_LEN - 1) - t0).astype(F32)
        s_c = jnp.where(first_q <= q_lane, s_c, NEG_INF)
        m_c = jnp.max(s_c, axis=0, keepdims=True)
        e_c = jnp.exp2(s_c - m_c)
        inv_c = jnp.where(m_c > 0.5 * NEG_INF, 1.0 / jnp.maximum(jnp.sum(e_c, axis=0, keepdims=True), 1e-30), 0.0)
        e_b = e_c.astype(BF16)
        if nj < n_slc:
            gap = jnp.zeros((n_slc - nj, qw), BF16)
            e_b = jnp.concatenate(
                [piece for r in range(CMP_PER_SLC) for piece in (e_b[r * nj:(r + 1) * nj], gap)], axis=0)
        oc_ref[...] = _dot(vct_ref[0], e_b) * inv_c

        p_c = e_c * inv_c
        psum = p_c[:, 0:Q_TILE]
        for g in range(1, Q_PER_KV):
            psum = psum + p_c[:, g * Q_TILE:(g + 1) * Q_TILE]
        j_idx = lax.broadcasted_iota(jnp.int32, (nj, 1), 0)
        straddle = psum[3 * nj:4 * nj]
        from_prev = jnp.where(j_idx == 0, 0.0, pltpu.roll(straddle, 1, axis=0))
        imp = (psum[0:nj] + psum[nj:2 * nj] + psum[2 * nj:3 * nj]
               + 0.5 * straddle + 0.5 * from_prev)

        span = WINDOW + Q_TILE
        w0 = pl.multiple_of(jnp.maximum(t0 - WINDOW, 0), Q_TILE)
        rhs_w = jnp.concatenate([qt, jnp.zeros((LANES - HEAD_DIM, qw), BF16)], axis=0)
        wb_off = pl.multiple_of(WINDOW - (t0 - w0), Q_TILE)
        s_w = _dot(kw_ref[0, pl.ds(w0, span), :], rhs_w) + lanes4(wb_ref[pl.ds(wb_off, span), :])
        e_w = jnp.exp2(s_w - jnp.max(s_w, axis=0, keepdims=True)).astype(BF16)
        acc_w = _dot(vwt_ref[0, :, pl.ds(w0, span)], e_w)
        ow_ref[...] = acc_w[:HEAD_DIM] * (1.0 / jnp.maximum(acc_w[HEAD_DIM:HEAD_DIM + 1], 1e-30))

        forced = (j_idx == 0) | (j_idx == tq // SLC_LEN)
        valid = (j_idx * SLC_LEN) <= tq
        score = jnp.where(valid, jnp.where(forced, FORCE_SCORE, imp), NEG_INF)
        j_f = j_idx.astype(F32)
        for _ in range(min(topk, nj)):
            best = jnp.max(score, axis=0, keepdims=True)
            first = jnp.min(jnp.where(score == best, j_f, float(nj)), axis=0, keepdims=True)
            score = jnp.where(j_f == first, -jnp.inf, score)
        tiles = nj // blocks_per_tile
        bias = jnp.where(valid & (score == -jnp.inf), 0.0, NEG_INF)
        bias_ref[0:tiles, 0:blocks_per_tile, :] = bias.reshape(tiles, blocks_per_tile, Q_TILE)
        if blocks_per_tile < BF16_ROWS:
            bias_ref[0:tiles, blocks_per_tile:, :] = jnp.zeros((tiles, BF16_ROWS - blocks_per_tile, Q_TILE), F32)

    j_vis = (t0 + Q_TILE) // SLC_LEN
    step = n_slc // FRONT_CLASSES
    for k in range(FRONT_CLASSES):
        pl.when((j_vis > k * step) & (j_vis <= (k + 1) * step))(functools.partial(front, (k + 1) * step))

    zeros_tail = jnp.zeros((LANES - HEAD_DIM - BF16_ROWS, qw), BF16)

    def scores(slot, j):
        k0 = pl.multiple_of(j * KV_TILE, KV_TILE)
        rhs = jnp.concatenate([qt, lanes4(bias_ref[j]).astype(BF16), zeros_tail], axis=0)
        s_ref[slot] = _dot(ks_ref[0, pl.ds(k0, KV_TILE), :], rhs)

    def accumulate(s, j):
        k0 = pl.multiple_of(j * KV_TILE, KV_TILE)
        m_old = m_ref[...]
        m_new = jnp.maximum(m_old, jnp.max(s, axis=0, keepdims=True))
        p = jnp.exp2(s - m_new).astype(BF16)
        acc_ref[...] = jnp.exp2(m_old - m_new) * acc_ref[...] + _dot(vst_ref[0, :, pl.ds(k0, KV_TILE)], p)
        m_ref[...] = m_new

    m_ref[...] = jnp.full((1, qw), NEG_INF, F32)
    acc_ref[...] = jnp.zeros((V_ROWS, qw), F32)
    n_full = t0 // KV_TILE
    scores(0, 0)

    def pair(i, _):
        j = 2 * i
        scores(1, j + 1)
        accumulate(s_ref[0], j)
        scores(0, j + 2)
        accumulate(s_ref[1], j + 1)
        return 0

    lax.fori_loop(0, n_full // 2, pair, 0)
    odd = n_full % 2

    @pl.when(odd == 1)
    def _():
        scores(1, n_full)
        accumulate(s_ref[0], n_full - 1)

    tok = n_full * KV_TILE + lax.broadcasted_iota(jnp.int32, (KV_TILE, 1), 0)
    accumulate(jnp.where(tok <= t_lane, s_ref[odd], NEG_INF), n_full)
    acc_s = acc_ref[...]
    o_s = acc_s[:HEAD_DIM] * (1.0 / jnp.maximum(acc_s[HEAD_DIM:HEAD_DIM + 1], 1e-30))

    gates = jax.nn.sigmoid(ng_ref[...])

    def gate(b):
        return jnp.concatenate([gates[b * Q_PER_KV + g:b * Q_PER_KV + g + 1] for g in range(Q_PER_KV)], axis=1)

    o = gate(0) * oc_ref[...] + gate(1) * o_s + gate(2) * ow_ref[...]
    pairs = []
    for g in range(0, Q_PER_KV, 2):
        two = jnp.concatenate([o[:, g * Q_TILE:(g + 1) * Q_TILE], o[:, (g + 1) * Q_TILE:(g + 2) * Q_TILE]], axis=0)
        pairs.append(two.T)
    out_ref[...] = jnp.concatenate(pairs, axis=1).astype(out_ref.dtype)


def _window_bias_table():
    r = jnp.arange(2 * WINDOW + Q_TILE, dtype=jnp.int32)[:, None]
    q = jnp.arange(Q_TILE, dtype=jnp.int32)[None, :]
    return jnp.where((r > q) & (r <= WINDOW + q), 0.0, NEG_INF).astype(F32)


def _attention(qt, kc, vct, ks, vst, kw, vwt, ngt, seq):
    wb = _window_bias_table()
    n_slc = seq // SLC_LEN
    nb = seq // Q_TILE
    qw = Q_PER_KV * Q_TILE
    n_rows = CMP_PER_SLC * n_slc
    topk = min(SLC_TOPK, n_slc)

    def per_head(block):
        return pl.BlockSpec(block, lambda k, c: (k, 0, 0), pipeline_mode=pl.Buffered(1))

    return pl.pallas_call(
        functools.partial(_attn_kernel, n_slc=n_slc, topk=topk),
        grid=(N_KV_HEADS, nb),
        in_specs=[
            pl.BlockSpec((Q_PER_KV * HEAD_DIM, Q_TILE), lambda k, c: (k, c)),
            per_head((1, n_rows, HEAD_DIM)),
            per_head((1, HEAD_DIM, n_rows)),
            per_head((1, seq, LANES)),
            per_head((1, V_ROWS, seq)),
            per_head((1, seq, LANES)),
            per_head((1, V_ROWS, seq)),
            pl.BlockSpec((GATE_ROWS, Q_TILE), lambda k, c: (k, c)),
            pl.BlockSpec(wb.shape, lambda k, c: (0, 0), pipeline_mode=pl.Buffered(1)),
        ],
        out_specs=pl.BlockSpec((Q_TILE, Q_PER_KV * HEAD_DIM), lambda k, c: (c, k)),
        out_shape=jax.ShapeDtypeStruct((seq, N_HEADS * HEAD_DIM), BF16),
        scratch_shapes=[
            pltpu.VMEM((n_slc * SLC_LEN // KV_TILE, BF16_ROWS, Q_TILE), F32),
            pltpu.VMEM((SCORE_SLOTS, KV_TILE, qw), F32),
            pltpu.VMEM((V_ROWS, qw), F32),
            pltpu.VMEM((1, qw), F32),
            pltpu.VMEM((HEAD_DIM, qw), F32),
            pltpu.VMEM((HEAD_DIM, qw), F32),
        ],
        compiler_params=pltpu.CompilerParams(
            dimension_semantics=("arbitrary", "arbitrary"), vmem_limit_bytes=VMEM_LIMIT_BYTES),
        name="nsa_attention",
    )(qt, kc, vct, ks, vst, kw, vwt, ngt, wb)


def _discretize_kernel(are_ref, aim_ref, ldt_ref, bre_ref, bim_ref, ar_ref, ai_ref, br_ref, bi_ref):
    a_re, a_im = are_ref[...], aim_ref[...]
    dt = jnp.exp(ldt_ref[...])
    mag = jnp.exp(a_re * dt)
    ang = a_im * dt
    ab_re, ab_im = mag * jnp.cos(ang), mag * jnp.sin(ang)
    n_re, n_im = ab_re - 1.0, ab_im
    den = a_re * a_re + a_im * a_im
    c_re = (n_re * a_re + n_im * a_im) / den
    c_im = (n_im * a_re - n_re * a_im) / den
    b_re, b_im = bre_ref[...], bim_ref[...]
    ar_ref[...] = ab_re
    ai_ref[...] = ab_im
    br_ref[...] = c_re * b_re - c_im * b_im
    bi_ref[...] = c_re * b_im + c_im * b_re


def _discretize(a_re, a_im, log_dt, b_re, b_im):
    g, p, h = b_re.shape
    col = jax.ShapeDtypeStruct((g, p, 1), F32)
    full = jax.ShapeDtypeStruct((g, p, h), F32)
    return pl.pallas_call(
        _discretize_kernel, out_shape=[col, col, full, full], name="ssm_discretize",
    )(a_re[..., None], a_im[..., None], log_dt[:, None, None], b_re, b_im)


def _cmul(ar, ai, br, bi):
    return ar * br - ai * bi, ar * bi + ai * br


def _s5_kernel(u_ref, bblk_ref, a_ref, cblk_ref, d_ref, wglu_ref, y_ref, xr_ref, xi_ref, st_ref, pw_ref):
    seg = SCAN_SEGMENTS
    steps = xr_ref.shape[0]
    n_state = xr_ref.shape[2]
    n_blk, blk_in, blk_state2 = bblk_ref.shape
    blk_state = blk_state2 // 2

    @pl.when(pl.program_id(0) == 0)
    def _():
        st_ref[...] = jnp.zeros_like(st_ref)
        wr, wi = a_ref[0:1, :], a_ref[1:2, :]
        for j in range(steps):
            pw_ref[j, 0] = jnp.broadcast_to(wr, (seg, n_state))
            pw_ref[j, 1] = jnp.broadcast_to(wi, (seg, n_state))
            wr, wi = _cmul(wr, wi, a_ref[0:1, :], a_ref[1:2, :])

    u = u_ref[...]
    ub = u.astype(BF16)
    for k in range(n_blk):
        bu = _dot(ub[:, k * blk_in:(k + 1) * blk_in], bblk_ref[k])
        xr_ref[:, :, k * blk_state:(k + 1) * blk_state] = bu[:, :blk_state].reshape(steps, seg, blk_state)
        xi_ref[:, :, k * blk_state:(k + 1) * blk_state] = bu[:, blk_state:].reshape(steps, seg, blk_state)

    lb = SCAN_LANE_BLOCK
    for b0 in range(0, n_state, lb):
        lanes = slice(b0, b0 + lb)
        a_re = a_ref[0:1, lanes]
        a_im = a_ref[1:2, lanes]
        ar8 = jnp.broadcast_to(a_re, (seg, lb))
        ai8 = jnp.broadcast_to(a_im, (seg, lb))

        def local(j, carry):
            xr, xi = carry
            pr, pi = _cmul(ar8, ai8, xr, xi)
            xr = pr + xr_ref[j, :, lanes]
            xi = pi + xi_ref[j, :, lanes]
            xr_ref[j, :, lanes] = xr
            xi_ref[j, :, lanes] = xi
            return xr, xi

        end_r, end_i = lax.fori_loop(0, steps, local, (jnp.zeros((seg, lb), F32), jnp.zeros((seg, lb), F32)))

        pr, pi = a_re, a_im
        for _ in range(steps.bit_length() - 1):
            pr, pi = _cmul(pr, pi, pr, pi)
        cr, ci = st_ref[0:1, lanes], st_ref[1:2, lanes]
        in_r, in_i = [], []
        for s in range(seg):
            in_r.append(cr)
            in_i.append(ci)
            qr, qi = _cmul(pr, pi, cr, ci)
            cr, ci = qr + end_r[s:s + 1], qi + end_i[s:s + 1]
        st_ref[0:1, lanes] = cr
        st_ref[1:2, lanes] = ci
        in_r = jnp.concatenate(in_r, axis=0)
        in_i = jnp.concatenate(in_i, axis=0)

        def fix(j, carry):
            fr, fi = _cmul(pw_ref[j, 0, :, lanes], pw_ref[j, 1, :, lanes], in_r, in_i)
            xr_ref[j, :, lanes] = xr_ref[j, :, lanes] + fr
            xi_ref[j, :, lanes] = xi_ref[j, :, lanes] + fi
            return carry

        lax.fori_loop(0, steps, fix, 0)

    rows = steps * seg
    y = d_ref[...] * u
    for k in range(n_blk):
        st = slice(k * blk_state, (k + 1) * blk_state)
        xs = jnp.concatenate([xr_ref[:, :, st].reshape(rows, blk_state),
                              xi_ref[:, :, st].reshape(rows, blk_state)], axis=1).astype(BF16)
        yk = _dot(xs, cblk_ref[k])
        if k == 0:
            ys = [yk]
        else:
            ys.append(yk)
    y = y + jnp.concatenate(ys, axis=1)
    y = jax.nn.gelu(y)
    y = y * jax.nn.sigmoid(_dot(y.astype(BF16), wglu_ref[...]))
    y_ref[...] = y.astype(y_ref.dtype)


def _s5(u_perm, bblk, a_lane, cblk, d_row, w_glu):
    seq, width = u_perm.shape
    tm = ROW_TILE
    steps = tm // SCAN_SEGMENTS
    n_state = a_lane.shape[1]
    const2 = lambda i: (0, 0)
    const3 = lambda i: (0, 0, 0)
    return pl.pallas_call(
        _s5_kernel,
        grid=(seq // tm,),
        in_specs=[
            pl.BlockSpec((tm, width), lambda i: (i, 0)),
            pl.BlockSpec(bblk.shape, const3),
            pl.BlockSpec(a_lane.shape, const2),
            pl.BlockSpec(cblk.shape, const3),
            pl.BlockSpec((1, width), const2),
            pl.BlockSpec((width, width), const2),
        ],
        out_specs=pl.BlockSpec((tm, width), lambda i: (i, 0)),
        out_shape=jax.ShapeDtypeStruct((seq, width), BF16),
        scratch_shapes=[
            pltpu.VMEM((steps, SCAN_SEGMENTS, n_state), F32),
            pltpu.VMEM((steps, SCAN_SEGMENTS, n_state), F32),
            pltpu.VMEM((2, n_state), F32),
            pltpu.VMEM((steps, 2, SCAN_SEGMENTS, n_state), F32),
        ],
        compiler_params=pltpu.CompilerParams(
            dimension_semantics=("arbitrary",), vmem_limit_bytes=VMEM_LIMIT_BYTES),
        name="s5_scan",
    )(u_perm, bblk, a_lane, cblk, d_row, w_glu)


def _ffn_kernel(x_ref, ya_ref, yb_ref, gmix_ref, wg_ref, wpa_ref, wpb_ref, wout_ref, gffn_ref,
                wup_ref, cw_ref, cb_ref, wdown_ref, gfin_ref, o_ref, tail_ref, *, d_ff, chunk):
    tm, d_model = x_ref.shape

    @pl.when(pl.program_id(0) == 0)
    def _():
        tail_ref[...] = jnp.zeros_like(tail_ref)

    x = x_ref[...]
    h = _rmsnorm(x, gmix_ref[...]).astype(BF16)
    gates = jax.nn.sigmoid(_dot(h, wg_ref[...]))
    merged = (gates[:, :d_model] * _dot(ya_ref[...], wpa_ref[...])
              + gates[:, d_model:] * _dot(yb_ref[...], wpb_ref[...]))
    x1 = x + _dot(merged.astype(BF16), wout_ref[...])
    h2 = _rmsnorm(x1, gffn_ref[...]).astype(BF16)

    def conv_cols(c0):
        cols = slice(c0, c0 + chunk)
        up = _dot(h2, wup_ref[:, cols])
        ext = jnp.concatenate([tail_ref[:, cols], up], axis=0)
        tail_ref[:, cols] = up[tm - SUBLANES:, :]
        prev1 = pltpu.roll(ext, 1, axis=0)[SUBLANES:]
        prev2 = pltpu.roll(ext, 2, axis=0)[SUBLANES:]
        return cb_ref[:, cols] + cw_ref[0:1, cols] * prev2 + cw_ref[1:2, cols] * prev1 + cw_ref[2:3, cols] * up

    acc = x1
    for k in range(d_ff // chunk):
        gate = conv_cols(k * chunk)
        val = conv_cols(d_ff + k * chunk)
        act = (jax.nn.silu(gate) * val).astype(BF16)
        acc = acc + _dot(act, wdown_ref[k * chunk:(k + 1) * chunk, :])
    o_ref[...] = _rmsnorm(acc, gfin_ref[...])


def _ffn(x2, ya, yb, g_mix, w_g, w_pa, w_pb, w_out, g_ffn, w_up, conv_w, conv_b, w_down, g_fin):
    seq, d_model = x2.shape
    d_ff = w_down.shape[0]
    tm = FFN_ROW_TILE
    chunk = FFN_COL_CHUNK
    assert d_ff % chunk == 0 and chunk % LANES == 0
    row = lambda i: (i, 0)

    def resident(arr):
        return pl.BlockSpec(arr.shape, lambda i: (0, 0), pipeline_mode=pl.Buffered(1))

    return pl.pallas_call(
        functools.partial(_ffn_kernel, d_ff=d_ff, chunk=chunk),
        grid=(seq // tm,),
        in_specs=[
            pl.BlockSpec((tm, d_model), row),
            pl.BlockSpec((tm, ya.shape[1]), row),
            pl.BlockSpec((tm, yb.shape[1]), row),
            resident(g_mix), resident(w_g), resident(w_pa), resident(w_pb), resident(w_out), resident(g_ffn),
            resident(w_up), resident(conv_w), resident(conv_b), resident(w_down), resident(g_fin),
        ],
        out_specs=pl.BlockSpec((tm, d_model), row),
        out_shape=jax.ShapeDtypeStruct((seq, d_model), F32),
        scratch_shapes=[pltpu.VMEM((SUBLANES, 2 * d_ff), F32)],
        compiler_params=pltpu.CompilerParams(
            dimension_semantics=("arbitrary",), vmem_limit_bytes=VMEM_LIMIT_BYTES),
        name="merge_ffn",
    )(x2, ya, yb, g_mix, w_g, w_pa, w_pb, w_out, g_ffn, w_up, conv_w, conv_b, w_down, g_fin)


def _block_diag(m, n_blk):
    g, r, c = m.shape
    per = g // n_blk
    eye = jnp.eye(per, dtype=m.dtype)
    m = m.reshape(n_blk, per, r, c)
    return jnp.einsum("bgrc,gk->bgrkc", m, eye).reshape(n_blk, per * r, per * c)


def kernel(x, positions, norm_mix, w_in, cmp_k_pe, cmp_k_w1, cmp_k_w2, cmp_v_pe, cmp_v_w1, cmp_v_w2, ssm_a_re, ssm_a_im, ssm_b_re, ssm_b_im, ssm_c_re, ssm_c_im, ssm_d, ssm_log_dt, ssm_w_glu, w_proj_a, w_proj_b, w_out, norm_ffn, w_up, conv_w, conv_b, w_down, norm_final):
    bsz, seq, d_model = x.shape
    assert bsz == 1 and norm_mix.shape[0] == 1
    assert seq % max(KV_TILE, ROW_TILE) == 0 and KV_TILE // SLC_LEN <= BF16_ROWS and seq >= WINDOW + Q_TILE
    attn_w = N_HEADS * HEAD_DIM
    kv_w = N_KV_HEADS * HEAD_DIM
    ssm_width = ssm_d.shape[1]
    x2 = x[0]
    pos = positions[0]

    sizes = (attn_w,) + (kv_w,) * 6 + (3 * N_HEADS, ssm_width, d_model, d_model)
    offs = [0]
    for s in sizes:
        offs.append(offs[-1] + s)
    w = w_in[0]
    col = lambda i: w[:, offs[i]:offs[i + 1]]
    w_q, w_kc, w_vc, w_ks, w_vs, w_kw, w_vw, w_ng, w_u, w_ga, w_gb = [col(i) for i in range(11)]
    w_ng = w_ng.reshape(d_model, N_KV_HEADS, Q_PER_KV, 3).transpose(0, 1, 3, 2).reshape(d_model, N_KV_HEADS, 3 * Q_PER_KV)
    w_ng = jnp.pad(w_ng, ((0, 0), (0, 0), (0, GATE_ROWS - 3 * Q_PER_KV))).reshape(d_model, N_KV_HEADS * GATE_ROWS)
    w_t = jnp.concatenate([w_q, w_ks, w_kw, w_vs, w_vw, w_ng], axis=1).T.astype(BF16)
    w_nat = jnp.concatenate([w_kc, w_vc, w_u], axis=1).astype(BF16)
    w_gate = jnp.concatenate([w_ga, w_gb], axis=1).astype(BF16)

    inv_freq = ROPE_THETA ** (-jnp.arange(0, ROT_DIM, 2, dtype=F32) / ROT_DIM)
    qt, ks, kw, vst, vwt, ngt, kvc, u = _inproj(x2, norm_mix, pos[None, :], inv_freq[:, None], w_nat, w_t)

    n_half = seq // CMP_STRIDE
    n_cmp = (seq - CMP_LEN) // CMP_STRIDE + 1
    h4 = kvc.reshape(seq, 2 * N_KV_HEADS, HEAD_DIM).transpose(1, 0, 2).reshape(
        2 * N_KV_HEADS, n_half, CMP_STRIDE * HEAD_DIM)
    halves = lambda pe: pe.reshape(2, CMP_STRIDE * HEAD_DIM)
    pe4 = jnp.stack([halves(cmp_k_pe[0])] * N_KV_HEADS + [halves(cmp_v_pe[0])] * N_KV_HEADS)
    w1 = jnp.stack([cmp_k_w1[0], cmp_v_w1[0]]).astype(BF16)
    w2 = jnp.pad(jnp.stack([cmp_k_w2[0], cmp_v_w2[0]]), ((0, 0), (0, 0), (0, LANES - HEAD_DIM))).astype(BF16)
    cmp_end = jnp.arange(n_cmp, dtype=jnp.int32) * CMP_STRIDE + CMP_LEN - 1
    cmp_pos = jnp.pad(pos[cmp_end], (0, n_half - n_cmp))
    pat = _rope_lane_patterns()
    pat2 = jnp.stack([pat, jnp.zeros_like(pat)])
    cmp = _compress(h4, pe4, w1, w2, cmp_pos[:, None], pat2)[:, :, :HEAD_DIM]
    n_slc = seq // SLC_LEN
    cmp = cmp.reshape(2 * N_KV_HEADS, n_slc, CMP_PER_SLC, HEAD_DIM).transpose(0, 2, 1, 3).reshape(
        2 * N_KV_HEADS, n_half, HEAD_DIM).astype(BF16)
    kc = cmp[:N_KV_HEADS]
    vct = cmp[N_KV_HEADS:].transpose(0, 2, 1)

    y_a = _attention(qt, kc, vct, ks, vst, kw, vwt, ngt, seq)

    ab_re, ab_im, bb_re, bb_im = _discretize(ssm_a_re[0], ssm_a_im[0], ssm_log_dt[0], ssm_b_re[0], ssm_b_im[0])
    n_blk = 2
    a_lane = jnp.stack([ab_re.reshape(-1), ab_im.reshape(-1)])
    to_in = lambda b: _block_diag(b.transpose(0, 2, 1), n_blk)
    bblk = jnp.concatenate([to_in(bb_re), to_in(bb_im)], axis=2).astype(BF16)
    to_out = lambda cm: _block_diag(cm.transpose(0, 2, 1), n_blk)
    cblk = jnp.concatenate([to_out(ssm_c_re[0]), -to_out(ssm_c_im[0])], axis=1).astype(BF16)
    steps = ROW_TILE // SCAN_SEGMENTS
    nt = seq // ROW_TILE
    u_perm = u.reshape(nt, SCAN_SEGMENTS, steps, ssm_width).transpose(0, 2, 1, 3).reshape(seq, ssm_width)
    y_b = _s5(u_perm, bblk, a_lane, cblk, ssm_d, ssm_w_glu[0].astype(BF16))
    y_b = y_b.reshape(nt, steps, SCAN_SEGMENTS, ssm_width).transpose(0, 2, 1, 3).reshape(seq, ssm_width)

    out = _ffn(x2, y_a, y_b, norm_mix, w_gate, w_proj_a[0].astype(BF16), w_proj_b[0].astype(BF16),
               w_out[0].astype(BF16), norm_ffn, w_up[0].astype(BF16), conv_w[0], conv_b, w_down[0].astype(BF16),
               norm_final[None, :])
    return out[None]
```

```python
import functools

import jax
import jax.numpy as jnp
from jax import lax
from jax.experimental import pallas as pl
from jax.experimental.pallas import tpu as pltpu

F32 = jnp.float32
BF16 = jnp.bfloat16

N_HEADS = 8
N_KV_HEADS = 2
HEAD_DIM = 64
Q_PER_KV = N_HEADS // N_KV_HEADS
ROT_DIM = HEAD_DIM // 4
ROPE_THETA = 500000.0
CMP_LEN = 32
CMP_STRIDE = 16
CMP_PER_SLC = 4
SLC_LEN = 64
SLC_TOPK = 16
WINDOW = 512
SSM_GROUP = 16
SSM_STATE = 64
CONV_WIDTH = 3
EPS = 1e-6
NEG_INF = -1e30
FORCE_SCORE = 1e4
LOG2E = 1.4426950408889634

LANES = 128
SUBLANES = 8
BF16_ROWS = 16
VMEM_LIMIT_BYTES = 56 * 1024 * 1024

ROW_TILE = 512
FFN_ROW_TILE = 256
FFN_COL_CHUNK = 1408
Q_TILE = 256
KV_TILE = 512
SCORE_SLOTS = 2
FRONT_CLASSES = 4
SCAN_SEGMENTS = SUBLANES
SCAN_LANE_BLOCK = 1024


def _rmsnorm(x, g):
    return x * lax.rsqrt(jnp.mean(x * x, axis=-1, keepdims=True) + EPS) * g


def _dot(a, b):
    return jnp.dot(a, b, preferred_element_type=F32)


def _dot_nt(a, b):
    return lax.dot_general(a, b, (((1,), (1,)), ((), ())), preferred_element_type=F32)


def _rope_lane_patterns():
    half = ROT_DIM // 2
    d = jnp.arange(LANES) % HEAD_DIM
    inv_freq = ROPE_THETA ** (-jnp.arange(0, ROT_DIM, 2, dtype=F32) / ROT_DIM)
    freq = jnp.where(d < ROT_DIM, inv_freq[d % half], 0.0)
    m_lo = jnp.where(d < half, -1.0, 0.0)
    m_hi = jnp.where((d >= half) & (d < ROT_DIM), 1.0, 0.0)
    return jnp.stack([freq, m_lo, m_hi]).astype(F32)


def _rope_tables(pos_col, pat):
    ang = pos_col.astype(F32) * pat[0:1]
    sin = jnp.sin(ang)
    return jnp.cos(ang), sin * pat[1:2], sin * pat[2:3]


def _rope(x, cos, sin_lo, sin_hi):
    half = ROT_DIM // 2
    return (x * cos + pltpu.roll(x, LANES - half, axis=1) * sin_lo
            + pltpu.roll(x, half, axis=1) * sin_hi)


KVC_COLS = 2 * N_KV_HEADS * HEAD_DIM
V_ROWS = HEAD_DIM + BF16_ROWS
GATE_ROWS = 16


def _rope_t(xh, cos, sin):
    half = ROT_DIM // 2
    x1, x2 = xh[0:half], xh[half:ROT_DIM]
    return jnp.concatenate([x1 * cos - x2 * sin, x2 * cos + x1 * sin, xh[ROT_DIM:]], axis=0)


def _inproj_kernel(x_ref, g_ref, pos_ref, invf_ref, wn_ref, wt_ref,
                   qt_ref, ks_ref, kw_ref, vst_ref, vwt_ref, ngt_ref, kvc_ref, u_ref):
    tm = x_ref.shape[0]
    h = _rmsnorm(x_ref[...], g_ref[...]).astype(BF16)
    pn = _dot(h, wn_ref[...])
    kvc_ref[...] = pn[:, :KVC_COLS]
    u_ref[...] = pn[:, KVC_COLS:]

    pt = _dot_nt(wt_ref[...], h)
    ang = invf_ref[...] * pos_ref[...].astype(F32)
    cos, sin = jnp.cos(ang), jnp.sin(ang)
    q_scale = (HEAD_DIM ** -0.5) * LOG2E
    for hd in range(N_HEADS):
        rows = slice(hd * HEAD_DIM, (hd + 1) * HEAD_DIM)
        qt_ref[rows, :] = (_rope_t(pt[rows], cos, sin) * q_scale).astype(BF16)

    kv_w = N_KV_HEADS * HEAD_DIM
    r_ks = N_HEADS * HEAD_DIM
    r_kw, r_vs, r_vw, r_ng = r_ks + kv_w, r_ks + 2 * kv_w, r_ks + 3 * kv_w, r_ks + 4 * kv_w
    key = pl.program_id(0) * tm + lax.broadcasted_iota(jnp.int32, (BF16_ROWS, tm), 1)
    blk = lax.broadcasted_iota(jnp.int32, (BF16_ROWS, tm), 0)
    onehot = jnp.where((key % KV_TILE) // SLC_LEN == blk, 1.0, 0.0)
    ones_rows = jnp.where(blk == 0, 1.0, 0.0)
    pad_k = jnp.zeros((LANES - HEAD_DIM - BF16_ROWS, tm), F32)
    pad_w = jnp.zeros((LANES - HEAD_DIM, tm), F32)
    for kh in range(N_KV_HEADS):
        off = kh * HEAD_DIM
        k_s = _rope_t(pt[r_ks + off:r_ks + off + HEAD_DIM], cos, sin)
        ks_ref[kh] = jnp.concatenate([k_s, onehot, pad_k], axis=0).T.astype(BF16)
        k_w = _rope_t(pt[r_kw + off:r_kw + off + HEAD_DIM], cos, sin)
        kw_ref[kh] = jnp.concatenate([k_w, pad_w], axis=0).T.astype(BF16)
        vst_ref[kh] = jnp.concatenate([pt[r_vs + off:r_vs + off + HEAD_DIM], ones_rows], axis=0).astype(BF16)
        vwt_ref[kh] = jnp.concatenate([pt[r_vw + off:r_vw + off + HEAD_DIM], ones_rows], axis=0).astype(BF16)
    ngt_ref[...] = pt[r_ng:r_ng + N_KV_HEADS * GATE_ROWS]


def _inproj(x2, g, pos_row, inv_freq, w_nat, w_t):
    seq, d_model = x2.shape
    tm = ROW_TILE
    n_nat = w_nat.shape[1]
    n_t = w_t.shape[0]
    attn_w = N_HEADS * HEAD_DIM
    row = lambda i: (i, 0)
    col = lambda i: (0, i)
    const = lambda i: (0, 0)
    head_rows = lambda i: (0, i, 0)
    head_cols = lambda i: (0, 0, i)
    return pl.pallas_call(
        _inproj_kernel,
        grid=(seq // tm,),
        in_specs=[
            pl.BlockSpec((tm, d_model), row),
            pl.BlockSpec((1, d_model), const),
            pl.BlockSpec((1, tm), col),
            pl.BlockSpec((ROT_DIM // 2, 1), const),
            pl.BlockSpec((d_model, n_nat), const),
            pl.BlockSpec((n_t, d_model), const),
        ],
        out_specs=[
            pl.BlockSpec((attn_w, tm), col),
            pl.BlockSpec((N_KV_HEADS, tm, LANES), head_rows),
            pl.BlockSpec((N_KV_HEADS, tm, LANES), head_rows),
            pl.BlockSpec((N_KV_HEADS, V_ROWS, tm), head_cols),
            pl.BlockSpec((N_KV_HEADS, V_ROWS, tm), head_cols),
            pl.BlockSpec((N_KV_HEADS * GATE_ROWS, tm), col),
            pl.BlockSpec((tm, KVC_COLS), row),
            pl.BlockSpec((tm, n_nat - KVC_COLS), row),
        ],
        out_shape=[
            jax.ShapeDtypeStruct((attn_w, seq), BF16),
            jax.ShapeDtypeStruct((N_KV_HEADS, seq, LANES), BF16),
            jax.ShapeDtypeStruct((N_KV_HEADS, seq, LANES), BF16),
            jax.ShapeDtypeStruct((N_KV_HEADS, V_ROWS, seq), BF16),
            jax.ShapeDtypeStruct((N_KV_HEADS, V_ROWS, seq), BF16),
            jax.ShapeDtypeStruct((N_KV_HEADS * GATE_ROWS, seq), F32),
            jax.ShapeDtypeStruct((seq, KVC_COLS), F32),
            jax.ShapeDtypeStruct((seq, n_nat - KVC_COLS), F32),
        ],
        compiler_params=pltpu.CompilerParams(
            dimension_semantics=("arbitrary",), vmem_limit_bytes=VMEM_LIMIT_BYTES),
        name="inproj",
    )(x2, g, pos_row, inv_freq, w_nat, w_t)


def _compress_kernel(h_ref, pe_ref, w1_ref, w2_ref, pos_ref, pat_ref, out_ref):
    hh = h_ref[0]
    rows, half_w = hh.shape
    a = _dot((hh + pe_ref[0, 0:1]).astype(BF16), w1_ref[0, :half_w])
    b = _dot((hh + pe_ref[0, 1:2]).astype(BF16), w1_ref[0, half_w:])
    pre = a + pltpu.roll(b, rows - 1, axis=0)
    z = _dot(jax.nn.gelu(pre).astype(BF16), w2_ref[0])
    cos, sin_lo, sin_hi = _rope_tables(pos_ref[...], pat_ref[0])
    out_ref[0] = _rope(z, cos, sin_lo, sin_hi)


def _compress(h4, pe4, w1, w2, cmp_pos_col, pat2):
    n, rows, half_w = h4.shape
    hidden = w1.shape[2]
    per = n // 2
    return pl.pallas_call(
        _compress_kernel,
        grid=(n,),
        in_specs=[
            pl.BlockSpec((1, rows, half_w), lambda i: (i, 0, 0)),
            pl.BlockSpec((1, 2, half_w), lambda i: (i, 0, 0)),
            pl.BlockSpec((1, 2 * half_w, hidden), lambda i: (i // per, 0, 0)),
            pl.BlockSpec((1, hidden, LANES), lambda i: (i // per, 0, 0)),
            pl.BlockSpec((rows, 1), lambda i: (0, 0)),
            pl.BlockSpec((1, 3, LANES), lambda i: (i // per, 0, 0)),
        ],
        out_specs=pl.BlockSpec((1, rows, LANES), lambda i: (i, 0, 0)),
        out_shape=jax.ShapeDtypeStruct((n, rows, LANES), F32),
        compiler_params=pltpu.CompilerParams(
            dimension_semantics=("arbitrary",), vmem_limit_bytes=VMEM_LIMIT_BYTES),
        name="compress",
    )(h4, pe4, w1, w2, cmp_pos_col, pat2)


def _attn_kernel(qt_ref, kc_ref, vct_ref, ks_ref, vst_ref, kw_ref, vwt_ref, ng_ref, wb_ref,
                 out_ref, bias_ref, s_ref, acc_ref, m_ref, oc_ref, ow_ref, *, n_slc, topk):
    c = pl.program_id(1)
    qw = Q_PER_KV * Q_TILE
    t0 = c * Q_TILE
    t_lane = t0 + (lax.broadcasted_iota(jnp.int32, (1, qw), 1) & (Q_TILE - 1))
    qt4 = qt_ref[...]
    qt = jnp.concatenate([qt4[g * HEAD_DIM:(g + 1) * HEAD_DIM] for g in range(Q_PER_KV)], axis=1)

    def lanes4(x):
        return jnp.concatenate([x] * Q_PER_KV, axis=1)

    blocks_per_tile = KV_TILE // SLC_LEN
    q_lane = (lax.broadcasted_iota(jnp.int32, (1, qw), 1) & (Q_TILE - 1)).astype(F32)
    tq = t0 + lax.broadcasted_iota(jnp.int32, (1, Q_TILE), 1)

    def front(nj):
        rows = CMP_PER_SLC * nj
        kc = jnp.concatenate([kc_ref[0, r * n_slc:r * n_slc + nj, :] for r in range(CMP_PER_SLC)], axis=0)
        s_c = _dot(kc, qt)
        row = lax.broadcasted_iota(jnp.int32, (rows, 1), 0)
        n_idx = CMP_PER_SLC * (row % nj) + row // nj
        first_q = (n_idx * CMP_STRIDE + (CMP_LEN - 1) - t0).astype(F32)
        s_c = jnp.where(first_q <= q_lane, s_c, NEG_INF)
        m_c = jnp.max(s_c, axis=0, keepdims=True)
        e_c = jnp.exp2(s_c - m_c)
        inv_c = jnp.where(m_c > 0.5 * NEG_INF, 1.0 / jnp.maximum(jnp.sum(e_c, axis=0, keepdims=True), 1e-30), 0.0)
        e_b = e_c.astype(BF16)
        if nj < n_slc:
            gap = jnp.zeros((n_slc - nj, qw), BF16)
            e_b = jnp.concatenate(
                [piece for r in range(CMP_PER_SLC) for piece in (e_b[r * nj:(r + 1) * nj], gap)], axis=0)
        oc_ref[...] = _dot(vct_ref[0], e_b) * inv_c

        p_c = e_c * inv_c
        psum = p_c[:, 0:Q_TILE]
        for g in range(1, Q_PER_KV):
            psum = psum + p_c[:, g * Q_TILE:(g + 1) * Q_TILE]
        j_idx = lax.broadcasted_iota(jnp.int32, (nj, 1), 0)
        straddle = psum[3 * nj:4 * nj]
        from_prev = jnp.where(j_idx == 0, 0.0, pltpu.roll(straddle, 1, axis=0))
        imp = (psum[0:nj] + psum[nj:2 * nj] + psum[2 * nj:3 * nj]
               + 0.5 * straddle + 0.5 * from_prev)

        span = WINDOW + Q_TILE
        w0 = pl.multiple_of(jnp.maximum(t0 - WINDOW, 0), Q_TILE)
        rhs_w = jnp.concatenate([qt, jnp.zeros((LANES - HEAD_DIM, qw), BF16)], axis=0)
        wb_off = pl.multiple_of(WINDOW - (t0 - w0), Q_TILE)
        s_w = _dot(kw_ref[0, pl.ds(w0, span), :], rhs_w) + lanes4(wb_ref[pl.ds(wb_off, span), :])
        e_w = jnp.exp2(s_w - jnp.max(s_w, axis=0, keepdims=True)).astype(BF16)
        acc_w = _dot(vwt_ref[0, :, pl.ds(w0, span)], e_w)
        ow_ref[...] = acc_w[:HEAD_DIM] * (1.0 / jnp.maximum(acc_w[HEAD_DIM:HEAD_DIM + 1], 1e-30))

        forced = (j_idx == 0) | (j_idx == tq // SLC_LEN)
        valid = (j_idx * SLC_LEN) <= tq
        score = jnp.where(valid, jnp.where(forced, FORCE_SCORE, imp), NEG_INF)
        j_f = j_idx.astype(F32)
        for _ in range(min(topk, nj)):
            best = jnp.max(score, axis=0, keepdims=True)
            first = jnp.min(jnp.where(score == best, j_f, float(nj)), axis=0, keepdims=True)
            score = jnp.where(j_f == first, -jnp.inf, score)
        tiles = nj // blocks_per_tile
        bias = jnp.where(valid & (score == -jnp.inf), 0.0, NEG_INF)
        bias_ref[0:tiles, 0:blocks_per_tile, :] = bias.reshape(tiles, blocks_per_tile, Q_TILE)
        if blocks_per_tile < BF16_ROWS:
            bias_ref[0:tiles, blocks_per_tile:, :] = jnp.zeros((tiles, BF16_ROWS - blocks_per_tile, Q_TILE), F32)

    j_vis = (t0 + Q_TILE) // SLC_LEN
    step = n_slc // FRONT_CLASSES
    for k in range(FRONT_CLASSES):
        pl.when((j_vis > k * step) & (j_vis <= (k + 1) * step))(functools.partial(front, (k + 1) * step))

    zeros_tail = jnp.zeros((LANES - HEAD_DIM - BF16_ROWS, qw), BF16)

    def scores(slot, j):
        k0 = pl.multiple_of(j * KV_TILE, KV_TILE)
        rhs = jnp.concatenate([qt, lanes4(bias_ref[j]).astype(BF16), zeros_tail], axis=0)
        s_ref[slot] = _dot(ks_ref[0, pl.ds(k0, KV_TILE), :], rhs)

    def accumulate(s, j):
        k0 = pl.multiple_of(j * KV_TILE, KV_TILE)
        m_old = m_ref[...]
        m_new = jnp.maximum(m_old, jnp.max(s, axis=0, keepdims=True))
        p = jnp.exp2(s - m_new).astype(BF16)
        acc_ref[...] = jnp.exp2(m_old - m_new) * acc_ref[...] + _dot(vst_ref[0, :, pl.ds(k0, KV_TILE)], p)
        m_ref[...] = m_new

    m_ref[...] = jnp.full((1, qw), NEG_INF, F32)
    acc_ref[...] = jnp.zeros((V_ROWS, qw), F32)
    n_full = t0 // KV_TILE
    scores(0, 0)

    def pair(j):
        scores(1, j + 1)
        accumulate(s_ref[0], j)
        scores(0, j + 2)
        accumulate(s_ref[1], j + 1)

    def two_pairs(i, _):
        pair(4 * i)
        pair(4 * i + 2)
        return 0

    n_pairs = n_full // 2
    lax.fori_loop(0, n_pairs // 2, two_pairs, 0)

    @pl.when(n_pairs % 2 == 1)
    def _():
        pair(2 * (n_pairs - 1))

    odd = n_full % 2

    @pl.when(odd == 1)
    def _():
        scores(1, n_full)
        accumulate(s_ref[0], n_full - 1)

    tok = n_full * KV_TILE + lax.broadcasted_iota(jnp.int32, (KV_TILE, 1), 0)
    accumulate(jnp.where(tok <= t_lane, s_ref[odd], NEG_INF), n_full)
    acc_s = acc_ref[...]
    o_s = acc_s[:HEAD_DIM] * (1.0 / jnp.maximum(acc_s[HEAD_DIM:HEAD_DIM + 1], 1e-30))

    gates = jax.nn.sigmoid(ng_ref[...])

    def gate(b):
        return jnp.concatenate([gates[b * Q_PER_KV + g:b * Q_PER_KV + g + 1] for g in range(Q_PER_KV)], axis=1)

    o = gate(0) * oc_ref[...] + gate(1) * o_s + gate(2) * ow_ref[...]
    pairs = []
    for g in range(0, Q_PER_KV, 2):
        two = jnp.concatenate([o[:, g * Q_TILE:(g + 1) * Q_TILE], o[:, (g + 1) * Q_TILE:(g + 2) * Q_TILE]], axis=0)
        pairs.append(two.T)
    out_ref[...] = jnp.concatenate(pairs, axis=1).astype(out_ref.dtype)


def _window_bias_table():
    r = jnp.arange(2 * WINDOW + Q_TILE, dtype=jnp.int32)[:, None]
    q = jnp.arange(Q_TILE, dtype=jnp.int32)[None, :]
    return jnp.where((r > q) & (r <= WINDOW + q), 0.0, NEG_INF).astype(F32)


def _attention(qt, kc, vct, ks, vst, kw, vwt, ngt, seq):
    wb = _window_bias_table()
    n_slc = seq // SLC_LEN
    nb = seq // Q_TILE
    qw = Q_PER_KV * Q_TILE
    n_rows = CMP_PER_SLC * n_slc
    topk = min(SLC_TOPK, n_slc)

    def per_head(block):
        return pl.BlockSpec(block, lambda k, c: (k, 0, 0), pipeline_mode=pl.Buffered(1))

    return pl.pallas_call(
        functools.partial(_attn_kernel, n_slc=n_slc, topk=topk),
        grid=(N_KV_HEADS, nb),
        in_specs=[
            pl.BlockSpec((Q_PER_KV * HEAD_DIM, Q_TILE), lambda k, c: (k, c)),
            per_head((1, n_rows, HEAD_DIM)),
            per_head((1, HEAD_DIM, n_rows)),
            per_head((1, seq, LANES)),
            per_head((1, V_ROWS, seq)),
            per_head((1, seq, LANES)),
            per_head((1, V_ROWS, seq)),
            pl.BlockSpec((GATE_ROWS, Q_TILE), lambda k, c: (k, c)),
            pl.BlockSpec(wb.shape, lambda k, c: (0, 0), pipeline_mode=pl.Buffered(1)),
        ],
        out_specs=pl.BlockSpec((Q_TILE, Q_PER_KV * HEAD_DIM), lambda k, c: (c, k)),
        out_shape=jax.ShapeDtypeStruct((seq, N_HEADS * HEAD_DIM), BF16),
        scratch_shapes=[
            pltpu.VMEM((n_slc * SLC_LEN // KV_TILE, BF16_ROWS, Q_TILE), F32),
            pltpu.VMEM((SCORE_SLOTS, KV_TILE, qw), F32),
            pltpu.VMEM((V_ROWS, qw), F32),
            pltpu.VMEM((1, qw), F32),
            pltpu.VMEM((HEAD_DIM, qw), F32),
            pltpu.VMEM((HEAD_DIM, qw), F32),
        ],
        compiler_params=pltpu.CompilerParams(
            dimension_semantics=("arbitrary", "arbitrary"), vmem_limit_bytes=VMEM_LIMIT_BYTES),
        name="nsa_attention",
    )(qt, kc, vct, ks, vst, kw, vwt, ngt, wb)


def _discretize_kernel(are_ref, aim_ref, ldt_ref, bre_ref, bim_ref, ar_ref, ai_ref, br_ref, bi_ref):
    a_re, a_im = are_ref[...], aim_ref[...]
    dt = jnp.exp(ldt_ref[...])
    mag = jnp.exp(a_re * dt)
    ang = a_im * dt
    ab_re, ab_im = mag * jnp.cos(ang), mag * jnp.sin(ang)
    n_re, n_im = ab_re - 1.0, ab_im
    den = a_re * a_re + a_im * a_im
    c_re = (n_re * a_re + n_im * a_im) / den
    c_im = (n_im * a_re - n_re * a_im) / den
    b_re, b_im = bre_ref[...], bim_ref[...]
    ar_ref[...] = ab_re
    ai_ref[...] = ab_im
    br_ref[...] = c_re * b_re - c_im * b_im
    bi_ref[...] = c_re * b_im + c_im * b_re


def _discretize(a_re, a_im, log_dt, b_re, b_im):
    g, p, h = b_re.shape
    col = jax.ShapeDtypeStruct((g, p, 1), F32)
    full = jax.ShapeDtypeStruct((g, p, h), F32)
    return pl.pallas_call(
        _discretize_kernel, out_shape=[col, col, full, full], name="ssm_discretize",
    )(a_re[..., None], a_im[..., None], log_dt[:, None, None], b_re, b_im)


def _cmul(ar, ai, br, bi):
    return ar * br - ai * bi, ar * bi + ai * br


def _s5_kernel(u_ref, bblk_ref, a_ref, cblk_ref, d_ref, wglu_ref, y_ref, xr_ref, xi_ref, st_ref, pw_ref):
    seg = SCAN_SEGMENTS
    steps = xr_ref.shape[0]
    n_state = xr_ref.shape[2]
    n_blk, blk_in, blk_state2 = bblk_ref.shape
    blk_state = blk_state2 // 2

    @pl.when(pl.program_id(0) == 0)
    def _():
        st_ref[...] = jnp.zeros_like(st_ref)
        wr, wi = a_ref[0:1, :], a_ref[1:2, :]
        for j in range(steps):
            pw_ref[j, 0] = jnp.broadcast_to(wr, (seg, n_state))
            pw_ref[j, 1] = jnp.broadcast_to(wi, (seg, n_state))
            wr, wi = _cmul(wr, wi, a_ref[0:1, :], a_ref[1:2, :])

    u = u_ref[...]
    ub = u.astype(BF16)
    for k in range(n_blk):
        bu = _dot(ub[:, k * blk_in:(k + 1) * blk_in], bblk_ref[k])
        xr_ref[:, :, k * blk_state:(k + 1) * blk_state] = bu[:, :blk_state].reshape(steps, seg, blk_state)
        xi_ref[:, :, k * blk_state:(k + 1) * blk_state] = bu[:, blk_state:].reshape(steps, seg, blk_state)

    lb = SCAN_LANE_BLOCK
    for b0 in range(0, n_state, lb):
        lanes = slice(b0, b0 + lb)
        a_re = a_ref[0:1, lanes]
        a_im = a_ref[1:2, lanes]
        ar8 = jnp.broadcast_to(a_re, (seg, lb))
        ai8 = jnp.broadcast_to(a_im, (seg, lb))

        def local(j, carry):
            xr, xi = carry
            pr, pi = _cmul(ar8, ai8, xr, xi)
            xr = pr + xr_ref[j, :, lanes]
            xi = pi + xi_ref[j, :, lanes]
            xr_ref[j, :, lanes] = xr
            xi_ref[j, :, lanes] = xi
            return xr, xi

        end_r, end_i = lax.fori_loop(0, steps, local, (jnp.zeros((seg, lb), F32), jnp.zeros((seg, lb), F32)))

        pr, pi = a_re, a_im
        for _ in range(steps.bit_length() - 1):
            pr, pi = _cmul(pr, pi, pr, pi)
        cr, ci = st_ref[0:1, lanes], st_ref[1:2, lanes]
        in_r, in_i = [], []
        for s in range(seg):
            in_r.append(cr)
            in_i.append(ci)
            qr, qi = _cmul(pr, pi, cr, ci)
            cr, ci = qr + end_r[s:s + 1], qi + end_i[s:s + 1]
        st_ref[0:1, lanes] = cr
        st_ref[1:2, lanes] = ci
        in_r = jnp.concatenate(in_r, axis=0)
        in_i = jnp.concatenate(in_i, axis=0)

        def fix(j, carry):
            fr, fi = _cmul(pw_ref[j, 0, :, lanes], pw_ref[j, 1, :, lanes], in_r, in_i)
            xr_ref[j, :, lanes] = xr_ref[j, :, lanes] + fr
            xi_ref[j, :, lanes] = xi_ref[j, :, lanes] + fi
            return carry

        lax.fori_loop(0, steps, fix, 0)

    rows = steps * seg
    y = d_ref[...] * u
    for k in range(n_blk):
        st = slice(k * blk_state, (k + 1) * blk_state)
        xs = jnp.concatenate([xr_ref[:, :, st].reshape(rows, blk_state),
                              xi_ref[:, :, st].reshape(rows, blk_state)], axis=1).astype(BF16)
        yk = _dot(xs, cblk_ref[k])
        if k == 0:
            ys = [yk]
        else:
            ys.append(yk)
    y = y + jnp.concatenate(ys, axis=1)
    y = jax.nn.gelu(y)
    y = y * jax.nn.sigmoid(_dot(y.astype(BF16), wglu_ref[...]))
    y_ref[...] = y.astype(y_ref.dtype)


def _s5(u_perm, bblk, a_lane, cblk, d_row, w_glu):
    seq, width = u_perm.shape
    tm = ROW_TILE
    steps = tm // SCAN_SEGMENTS
    n_state = a_lane.shape[1]
    const2 = lambda i: (0, 0)
    const3 = lambda i: (0, 0, 0)
    return pl.pallas_call(
        _s5_kernel,
        grid=(seq // tm,),
        in_specs=[
            pl.BlockSpec((tm, width), lambda i: (i, 0)),
            pl.BlockSpec(bblk.shape, const3),
            pl.BlockSpec(a_lane.shape, const2),
            pl.BlockSpec(cblk.shape, const3),
            pl.BlockSpec((1, width), const2),
            pl.BlockSpec((width, width), const2),
        ],
        out_specs=pl.BlockSpec((tm, width), lambda i: (i, 0)),
        out_shape=jax.ShapeDtypeStruct((seq, width), BF16),
        scratch_shapes=[
            pltpu.VMEM((steps, SCAN_SEGMENTS, n_state), F32),
            pltpu.VMEM((steps, SCAN_SEGMENTS, n_state), F32),
            pltpu.VMEM((2, n_state), F32),
            pltpu.VMEM((steps, 2, SCAN_SEGMENTS, n_state), F32),
        ],
        compiler_params=pltpu.CompilerParams(
            dimension_semantics=("arbitrary",), vmem_limit_bytes=VMEM_LIMIT_BYTES),
        name="s5_scan",
    )(u_perm, bblk, a_lane, cblk, d_row, w_glu)


def _ffn_kernel(x_ref, ya_ref, yb_ref, gmix_ref, wg_ref, wpa_ref, wpb_ref, wout_ref, gffn_ref,
                wup_ref, cw_ref, cb_ref, wdown_ref, gfin_ref, o_ref, tail_ref, *, d_ff, chunk):
    tm, d_model = x_ref.shape

    @pl.when(pl.program_id(0) == 0)
    def _():
        tail_ref[...] = jnp.zeros_like(tail_ref)

    x = x_ref[...]
    h = _rmsnorm(x, gmix_ref[...]).astype(BF16)
    gates = jax.nn.sigmoid(_dot(h, wg_ref[...]))
    merged = (gates[:, :d_model] * _dot(ya_ref[...], wpa_ref[...])
              + gates[:, d_model:] * _dot(yb_ref[...], wpb_ref[...]))
    x1 = x + _dot(merged.astype(BF16), wout_ref[...])
    h2 = _rmsnorm(x1, gffn_ref[...]).astype(BF16)

    def conv_cols(c0):
        cols = slice(c0, c0 + chunk)
        up = _dot(h2, wup_ref[:, cols])
        ext = jnp.concatenate([tail_ref[:, cols], up], axis=0)
        tail_ref[:, cols] = up[tm - SUBLANES:, :]
        prev1 = pltpu.roll(ext, 1, axis=0)[SUBLANES:]
        prev2 = pltpu.roll(ext, 2, axis=0)[SUBLANES:]
        return cb_ref[:, cols] + cw_ref[0:1, cols] * prev2 + cw_ref[1:2, cols] * prev1 + cw_ref[2:3, cols] * up

    acc = x1
    for k in range(d_ff // chunk):
        gate = conv_cols(k * chunk)
        val = conv_cols(d_ff + k * chunk)
        act = (jax.nn.silu(gate) * val).astype(BF16)
        acc = acc + _dot(act, wdown_ref[k * chunk:(k + 1) * chunk, :])
    o_ref[...] = _rmsnorm(acc, gfin_ref[...])


def _ffn(x2, ya, yb, g_mix, w_g, w_pa, w_pb, w_out, g_ffn, w_up, conv_w, conv_b, w_down, g_fin):
    seq, d_model = x2.shape
    d_ff = w_down.shape[0]
    tm = FFN_ROW_TILE
    chunk = FFN_COL_CHUNK
    assert d_ff % chunk == 0 and chunk % LANES == 0
    row = lambda i: (i, 0)

    def resident(arr):
        return pl.BlockSpec(arr.shape, lambda i: (0, 0), pipeline_mode=pl.Buffered(1))

    return pl.pallas_call(
        functools.partial(_ffn_kernel, d_ff=d_ff, chunk=chunk),
        grid=(seq // tm,),
        in_specs=[
            pl.BlockSpec((tm, d_model), row),
            pl.BlockSpec((tm, ya.shape[1]), row),
            pl.BlockSpec((tm, yb.shape[1]), row),
            resident(g_mix), resident(w_g), resident(w_pa), resident(w_pb), resident(w_out), resident(g_ffn),
            resident(w_up), resident(conv_w), resident(conv_b), resident(w_down), resident(g_fin),
        ],
        out_specs=pl.BlockSpec((tm, d_model), row),
        out_shape=jax.ShapeDtypeStruct((seq, d_model), F32),
        scratch_shapes=[pltpu.VMEM((SUBLANES, 2 * d_ff), F32)],
        compiler_params=pltpu.CompilerParams(
            dimension_semantics=("arbitrary",), vmem_limit_bytes=VMEM_LIMIT_BYTES),
        name="merge_ffn",
    )(x2, ya, yb, g_mix, w_g, w_pa, w_pb, w_out, g_ffn, w_up, conv_w, conv_b, w_down, g_fin)


def _block_diag(m, n_blk):
    g, r, c = m.shape
    per = g // n_blk
    eye = jnp.eye(per, dtype=m.dtype)
    m = m.reshape(n_blk, per, r, c)
    return jnp.einsum("bgrc,gk->bgrkc", m, eye).reshape(n_blk, per * r, per * c)


def kernel(x, positions, norm_mix, w_in, cmp_k_pe, cmp_k_w1, cmp_k_w2, cmp_v_pe, cmp_v_w1, cmp_v_w2, ssm_a_re, ssm_a_im, ssm_b_re, ssm_b_im, ssm_c_re, ssm_c_im, ssm_d, ssm_log_dt, ssm_w_glu, w_proj_a, w_proj_b, w_out, norm_ffn, w_up, conv_w, conv_b, w_down, norm_final):
    bsz, seq, d_model = x.shape
    assert bsz == 1 and norm_mix.shape[0] == 1
    assert seq % max(KV_TILE, ROW_TILE) == 0 and KV_TILE // SLC_LEN <= BF16_ROWS and seq >= WINDOW + Q_TILE
    attn_w = N_HEADS * HEAD_DIM
    kv_w = N_KV_HEADS * HEAD_DIM
    ssm_width = ssm_d.shape[1]
    x2 = x[0]
    pos = positions[0]

    sizes = (attn_w,) + (kv_w,) * 6 + (3 * N_HEADS, ssm_width, d_model, d_model)
    offs = [0]
    for s in sizes:
        offs.append(offs[-1] + s)
    w = w_in[0]
    col = lambda i: w[:, offs[i]:offs[i + 1]]
    w_q, w_kc, w_vc, w_ks, w_vs, w_kw, w_vw, w_ng, w_u, w_ga, w_gb = [col(i) for i in range(11)]
    w_ng = w_ng.reshape(d_model, N_KV_HEADS, Q_PER_KV, 3).transpose(0, 1, 3, 2).reshape(d_model, N_KV_HEADS, 3 * Q_PER_KV)
    w_ng = jnp.pad(w_ng, ((0, 0), (0, 0), (0, GATE_ROWS - 3 * Q_PER_KV))).reshape(d_model, N_KV_HEADS * GATE_ROWS)
    w_t = jnp.concatenate([w_q, w_ks, w_kw, w_vs, w_vw, w_ng], axis=1).T.astype(BF16)
    w_nat = jnp.concatenate([w_kc, w_vc, w_u], axis=1).astype(BF16)
    w_gate = jnp.concatenate([w_ga, w_gb], axis=1).astype(BF16)

    inv_freq = ROPE_THETA ** (-jnp.arange(0, ROT_DIM, 2, dtype=F32) / ROT_DIM)
    qt, ks, kw, vst, vwt, ngt, kvc, u = _inproj(x2, norm_mix, pos[None, :], inv_freq[:, None], w_nat, w_t)

    n_half = seq // CMP_STRIDE
    n_cmp = (seq - CMP_LEN) // CMP_STRIDE + 1
    h4 = kvc.reshape(seq, 2 * N_KV_HEADS, HEAD_DIM).transpose(1, 0, 2).reshape(
        2 * N_KV_HEADS, n_half, CMP_STRIDE * HEAD_DIM)
    halves = lambda pe: pe.reshape(2, CMP_STRIDE * HEAD_DIM)
    pe4 = jnp.stack([halves(cmp_k_pe[0])] * N_KV_HEADS + [halves(cmp_v_pe[0])] * N_KV_HEADS)
    w1 = jnp.stack([cmp_k_w1[0], cmp_v_w1[0]]).astype(BF16)
    w2 = jnp.pad(jnp.stack([cmp_k_w2[0], cmp_v_w2[0]]), ((0, 0), (0, 0), (0, LANES - HEAD_DIM))).astype(BF16)
    cmp_end = jnp.arange(n_cmp, dtype=jnp.int32) * CMP_STRIDE + CMP_LEN - 1
    cmp_pos = jnp.pad(pos[cmp_end], (0, n_half - n_cmp))
    pat = _rope_lane_patterns()
    pat2 = jnp.stack([pat, jnp.zeros_like(pat)])
    cmp = _compress(h4, pe4, w1, w2, cmp_pos[:, None], pat2)[:, :, :HEAD_DIM]
    n_slc = seq // SLC_LEN
    cmp = cmp.reshape(2 * N_KV_HEADS, n_slc, CMP_PER_SLC, HEAD_DIM).transpose(0, 2, 1, 3).reshape(
        2 * N_KV_HEADS, n_half, HEAD_DIM).astype(BF16)
    kc = cmp[:N_KV_HEADS]
    vct = cmp[N_KV_HEADS:].transpose(0, 2, 1)

    y_a = _attention(qt, kc, vct, ks, vst, kw, vwt, ngt, seq)

    ab_re, ab_im, bb_re, bb_im = _discretize(ssm_a_re[0], ssm_a_im[0], ssm_log_dt[0], ssm_b_re[0], ssm_b_im[0])
    n_blk = 2
    a_lane = jnp.stack([ab_re.reshape(-1), ab_im.reshape(-1)])
    to_in = lambda b: _block_diag(b.transpose(0, 2, 1), n_blk)
    bblk = jnp.concatenate([to_in(bb_re), to_in(bb_im)], axis=2).astype(BF16)
    to_out = lambda cm: _block_diag(cm.transpose(0, 2, 1), n_blk)
    cblk = jnp.concatenate([to_out(ssm_c_re[0]), -to_out(ssm_c_im[0])], axis=1).astype(BF16)
    steps = ROW_TILE // SCAN_SEGMENTS
    nt = seq // ROW_TILE
    u_perm = u.reshape(nt, SCAN_SEGMENTS, steps, ssm_width).transpose(0, 2, 1, 3).reshape(seq, ssm_width)
    y_b = _s5(u_perm, bblk, a_lane, cblk, ssm_d, ssm_w_glu[0].astype(BF16))
    y_b = y_b.reshape(nt, steps, SCAN_SEGMENTS, ssm_width).transpose(0, 2, 1, 3).reshape(seq, ssm_width)

    out = _ffn(x2, y_a, y_b, norm_mix, w_gate, w_proj_a[0].astype(BF16), w_proj_b[0].astype(BF16),
               w_out[0].astype(BF16), norm_ffn, w_up[0].astype(BF16), conv_w[0], conv_b, w_down[0].astype(BF16),
               norm_final[None, :])
    return out[None]
```

```python
import functools

import jax
import jax.numpy as jnp
from jax import lax
from jax.experimental import pallas as pl
from jax.experimental.pallas import tpu as pltpu

F32 = jnp.float32
BF16 = jnp.bfloat16

N_HEADS = 8
N_KV_HEADS = 2
HEAD_DIM = 64
Q_PER_KV = N_HEADS // N_KV_HEADS
ROT_DIM = HEAD_DIM // 4
ROPE_THETA = 500000.0
CMP_LEN = 32
CMP_STRIDE = 16
CMP_PER_SLC = 4
SLC_LEN = 64
SLC_TOPK = 16
WINDOW = 512
SSM_GROUP = 16
SSM_STATE = 64
CONV_WIDTH = 3
EPS = 1e-6
NEG_INF = -1e30
FORCE_SCORE = 1e4
LOG2E = 1.4426950408889634

LANES = 128
SUBLANES = 8
BF16_ROWS = 16
VMEM_LIMIT_BYTES = 56 * 1024 * 1024

ROW_TILE = 512
FFN_ROW_TILE = 256
FFN_COL_CHUNK = 1408
Q_TILE = 256
KV_TILE = 512
SCORE_SLOTS = 2
FRONT_CLASSES = 4
SCAN_SEGMENTS = SUBLANES
SCAN_LANE_BLOCK = 1024


def _rmsnorm(x, g):
    return x * lax.rsqrt(jnp.mean(x * x, axis=-1, keepdims=True) + EPS) * g


def _dot(a, b):
    return jnp.dot(a, b, preferred_element_type=F32)


def _dot_nt(a, b):
    return lax.dot_general(a, b, (((1,), (1,)), ((), ())), preferred_element_type=F32)


def _rope_lane_patterns():
    half = ROT_DIM // 2
    d = jnp.arange(LANES) % HEAD_DIM
    inv_freq = ROPE_THETA ** (-jnp.arange(0, ROT_DIM, 2, dtype=F32) / ROT_DIM)
    freq = jnp.where(d < ROT_DIM, inv_freq[d % half], 0.0)
    m_lo = jnp.where(d < half, -1.0, 0.0)
    m_hi = jnp.where((d >= half) & (d < ROT_DIM), 1.0, 0.0)
    return jnp.stack([freq, m_lo, m_hi]).astype(F32)


def _rope_tables(pos_col, pat):
    ang = pos_col.astype(F32) * pat[0:1]
    sin = jnp.sin(ang)
    return jnp.cos(ang), sin * pat[1:2], sin * pat[2:3]


def _rope(x, cos, sin_lo, sin_hi):
    half = ROT_DIM // 2
    return (x * cos + pltpu.roll(x, LANES - half, axis=1) * sin_lo
            + pltpu.roll(x, half, axis=1) * sin_hi)


KVC_COLS = 2 * N_KV_HEADS * HEAD_DIM
V_ROWS = HEAD_DIM + BF16_ROWS
GATE_ROWS = 16


def _rope_t(xh, cos, sin):
    half = ROT_DIM // 2
    x1, x2 = xh[0:half], xh[half:ROT_DIM]
    return jnp.concatenate([x1 * cos - x2 * sin, x2 * cos + x1 * sin, xh[ROT_DIM:]], axis=0)


def _inproj_kernel(x_ref, g_ref, pos_ref, invf_ref, wn_ref, wt_ref,
                   qt_ref, ks_ref, kw_ref, vst_ref, vwt_ref, ngt_ref, kvc_ref, u_ref):
    tm = x_ref.shape[0]
    h = _rmsnorm(x_ref[...], g_ref[...]).astype(BF16)
    pn = _dot(h, wn_ref[...])
    kvc_ref[...] = pn[:, :KVC_COLS]
    for k in range(u_ref.shape[0]):
        u_ref[k] = pn[:, KVC_COLS + k * LANES:KVC_COLS + (k + 1) * LANES]

    pt = _dot_nt(wt_ref[...], h)
    ang = invf_ref[...] * pos_ref[...].astype(F32)
    cos, sin = jnp.cos(ang), jnp.sin(ang)
    q_scale = (HEAD_DIM ** -0.5) * LOG2E
    for hd in range(N_HEADS):
        rows = slice(hd * HEAD_DIM, (hd + 1) * HEAD_DIM)
        qt_ref[rows, :] = (_rope_t(pt[rows], cos, sin) * q_scale).astype(BF16)

    kv_w = N_KV_HEADS * HEAD_DIM
    r_ks = N_HEADS * HEAD_DIM
    r_kw, r_vs, r_vw, r_ng = r_ks + kv_w, r_ks + 2 * kv_w, r_ks + 3 * kv_w, r_ks + 4 * kv_w
    key = pl.program_id(0) * tm + lax.broadcasted_iota(jnp.int32, (BF16_ROWS, tm), 1)
    blk = lax.broadcasted_iota(jnp.int32, (BF16_ROWS, tm), 0)
    onehot = jnp.where((key % KV_TILE) // SLC_LEN == blk, 1.0, 0.0)
    ones_rows = jnp.where(blk == 0, 1.0, 0.0)
    pad_k = jnp.zeros((LANES - HEAD_DIM - BF16_ROWS, tm), F32)
    pad_w = jnp.zeros((LANES - HEAD_DIM, tm), F32)
    for kh in range(N_KV_HEADS):
        off = kh * HEAD_DIM
        k_s = _rope_t(pt[r_ks + off:r_ks + off + HEAD_DIM], cos, sin)
        ks_ref[kh] = jnp.concatenate([k_s, onehot, pad_k], axis=0).T.astype(BF16)
        k_w = _rope_t(pt[r_kw + off:r_kw + off + HEAD_DIM], cos, sin)
        kw_ref[kh] = jnp.concatenate([k_w, pad_w], axis=0).T.astype(BF16)
        vst_ref[kh] = jnp.concatenate([pt[r_vs + off:r_vs + off + HEAD_DIM], ones_rows], axis=0).astype(BF16)
        vwt_ref[kh] = jnp.concatenate([pt[r_vw + off:r_vw + off + HEAD_DIM], ones_rows], axis=0).astype(BF16)
    ngt_ref[...] = pt[r_ng:r_ng + N_KV_HEADS * GATE_ROWS]


def _inproj(x2, g, pos_row, inv_freq, w_nat, w_t):
    seq, d_model = x2.shape
    tm = ROW_TILE
    n_nat = w_nat.shape[1]
    n_t = w_t.shape[0]
    attn_w = N_HEADS * HEAD_DIM
    row = lambda i: (i, 0)
    col = lambda i: (0, i)
    const = lambda i: (0, 0)
    head_rows = lambda i: (0, i, 0)
    head_cols = lambda i: (0, 0, i)
    return pl.pallas_call(
        _inproj_kernel,
        grid=(seq // tm,),
        in_specs=[
            pl.BlockSpec((tm, d_model), row),
            pl.BlockSpec((1, d_model), const),
            pl.BlockSpec((1, tm), col),
            pl.BlockSpec((ROT_DIM // 2, 1), const),
            pl.BlockSpec((d_model, n_nat), const),
            pl.BlockSpec((n_t, d_model), const),
        ],
        out_specs=[
            pl.BlockSpec((attn_w, tm), col),
            pl.BlockSpec((N_KV_HEADS, tm, LANES), head_rows),
            pl.BlockSpec((N_KV_HEADS, tm, LANES), head_rows),
            pl.BlockSpec((N_KV_HEADS, V_ROWS, tm), head_cols),
            pl.BlockSpec((N_KV_HEADS, V_ROWS, tm), head_cols),
            pl.BlockSpec((N_KV_HEADS * GATE_ROWS, tm), col),
            pl.BlockSpec((tm, KVC_COLS), row),
            pl.BlockSpec(((n_nat - KVC_COLS) // LANES, tm, LANES), head_rows),
        ],
        out_shape=[
            jax.ShapeDtypeStruct((attn_w, seq), BF16),
            jax.ShapeDtypeStruct((N_KV_HEADS, seq, LANES), BF16),
            jax.ShapeDtypeStruct((N_KV_HEADS, seq, LANES), BF16),
            jax.ShapeDtypeStruct((N_KV_HEADS, V_ROWS, seq), BF16),
            jax.ShapeDtypeStruct((N_KV_HEADS, V_ROWS, seq), BF16),
            jax.ShapeDtypeStruct((N_KV_HEADS * GATE_ROWS, seq), F32),
            jax.ShapeDtypeStruct((seq, KVC_COLS), F32),
            jax.ShapeDtypeStruct(((n_nat - KVC_COLS) // LANES, seq, LANES), F32),
        ],
        compiler_params=pltpu.CompilerParams(
            dimension_semantics=("arbitrary",), vmem_limit_bytes=VMEM_LIMIT_BYTES),
        name="inproj",
    )(x2, g, pos_row, inv_freq, w_nat, w_t)


def _compress_kernel(h_ref, pe_ref, w1_ref, w2_ref, pos_ref, pat_ref, out_ref):
    hh = h_ref[0]
    rows, half_w = hh.shape
    a = _dot((hh + pe_ref[0, 0:1]).astype(BF16), w1_ref[0, :half_w])
    b = _dot((hh + pe_ref[0, 1:2]).astype(BF16), w1_ref[0, half_w:])
    pre = a + pltpu.roll(b, rows - 1, axis=0)
    z = _dot(jax.nn.gelu(pre).astype(BF16), w2_ref[0])
    cos, sin_lo, sin_hi = _rope_tables(pos_ref[...], pat_ref[0])
    out_ref[0] = _rope(z, cos, sin_lo, sin_hi)


def _compress(h4, pe4, w1, w2, cmp_pos_col, pat2):
    n, rows, half_w = h4.shape
    hidden = w1.shape[2]
    per = n // 2
    return pl.pallas_call(
        _compress_kernel,
        grid=(n,),
        in_specs=[
            pl.BlockSpec((1, rows, half_w), lambda i: (i, 0, 0)),
            pl.BlockSpec((1, 2, half_w), lambda i: (i, 0, 0)),
            pl.BlockSpec((1, 2 * half_w, hidden), lambda i: (i // per, 0, 0)),
            pl.BlockSpec((1, hidden, LANES), lambda i: (i // per, 0, 0)),
            pl.BlockSpec((rows, 1), lambda i: (0, 0)),
            pl.BlockSpec((1, 3, LANES), lambda i: (i // per, 0, 0)),
        ],
        out_specs=pl.BlockSpec((1, rows, LANES), lambda i: (i, 0, 0)),
        out_shape=jax.ShapeDtypeStruct((n, rows, LANES), F32),
        compiler_params=pltpu.CompilerParams(
            dimension_semantics=("arbitrary",), vmem_limit_bytes=VMEM_LIMIT_BYTES),
        name="compress",
    )(h4, pe4, w1, w2, cmp_pos_col, pat2)


def _attn_kernel(qt_ref, kc_ref, vct_ref, ks_ref, vst_ref, kw_ref, vwt_ref, ng_ref, wb_ref,
                 out_ref, bias_ref, s_ref, acc_ref, m_ref, oc_ref, ow_ref, *, n_slc, topk):
    c = pl.program_id(1)
    qw = Q_PER_KV * Q_TILE
    t0 = c * Q_TILE
    t_lane = t0 + (lax.broadcasted_iota(jnp.int32, (1, qw), 1) & (Q_TILE - 1))
    qt4 = qt_ref[...]
    qt = jnp.concatenate([qt4[g * HEAD_DIM:(g + 1) * HEAD_DIM] for g in range(Q_PER_KV)], axis=1)

    def lanes4(x):
        return jnp.concatenate([x] * Q_PER_KV, axis=1)

    blocks_per_tile = KV_TILE // SLC_LEN
    q_lane = (lax.broadcasted_iota(jnp.int32, (1, qw), 1) & (Q_TILE - 1)).astype(F32)
    tq = t0 + lax.broadcasted_iota(jnp.int32, (1, Q_TILE), 1)

    def front(nj):
        rows = CMP_PER_SLC * nj
        kc = jnp.concatenate([kc_ref[0, r * n_slc:r * n_slc + nj, :] for r in range(CMP_PER_SLC)], axis=0)
        s_c = _dot(kc, qt)
        row = lax.broadcasted_iota(jnp.int32, (rows, 1), 0)
        n_idx = CMP_PER_SLC * (row % nj) + row // nj
        first_q = (n_idx * CMP_STRIDE + (CMP_LEN - 1) - t0).astype(F32)
        s_c = jnp.where(first_q <= q_lane, s_c, NEG_INF)
        m_c = jnp.max(s_c, axis=0, keepdims=True)
        e_c = jnp.exp2(s_c - m_c)
        inv_c = jnp.where(m_c > 0.5 * NEG_INF, 1.0 / jnp.maximum(jnp.sum(e_c, axis=0, keepdims=True), 1e-30), 0.0)
        e_b = e_c.astype(BF16)
        if nj < n_slc:
            gap = jnp.zeros((n_slc - nj, qw), BF16)
            e_b = jnp.concatenate(
                [piece for r in range(CMP_PER_SLC) for piece in (e_b[r * nj:(r + 1) * nj], gap)], axis=0)
        oc_ref[...] = _dot(vct_ref[0], e_b) * inv_c

        p_c = e_c * inv_c
        psum = p_c[:, 0:Q_TILE]
        for g in range(1, Q_PER_KV):
            psum = psum + p_c[:, g * Q_TILE:(g + 1) * Q_TILE]
        j_idx = lax.broadcasted_iota(jnp.int32, (nj, 1), 0)
        straddle = psum[3 * nj:4 * nj]
        from_prev = jnp.where(j_idx == 0, 0.0, pltpu.roll(straddle, 1, axis=0))
        imp = (psum[0:nj] + psum[nj:2 * nj] + psum[2 * nj:3 * nj]
               + 0.5 * straddle + 0.5 * from_prev)

        span = WINDOW + Q_TILE
        w0 = pl.multiple_of(jnp.maximum(t0 - WINDOW, 0), Q_TILE)
        rhs_w = jnp.concatenate([qt, jnp.zeros((LANES - HEAD_DIM, qw), BF16)], axis=0)
        wb_off = pl.multiple_of(WINDOW - (t0 - w0), Q_TILE)
        s_w = _dot(kw_ref[0, pl.ds(w0, span), :], rhs_w) + lanes4(wb_ref[pl.ds(wb_off, span), :])
        e_w = jnp.exp2(s_w - jnp.max(s_w, axis=0, keepdims=True)).astype(BF16)
        acc_w = _dot(vwt_ref[0, :, pl.ds(w0, span)], e_w)
        ow_ref[...] = acc_w[:HEAD_DIM] * (1.0 / jnp.maximum(acc_w[HEAD_DIM:HEAD_DIM + 1], 1e-30))

        forced = (j_idx == 0) | (j_idx == tq // SLC_LEN)
        valid = (j_idx * SLC_LEN) <= tq
        score = jnp.where(valid, jnp.where(forced, FORCE_SCORE, imp), NEG_INF)
        j_f = j_idx.astype(F32)
        for _ in range(min(topk, nj)):
            best = jnp.max(score, axis=0, keepdims=True)
            first = jnp.min(jnp.where(score == best, j_f, float(nj)), axis=0, keepdims=True)
            score = jnp.where(j_f == first, -jnp.inf, score)
        tiles = nj // blocks_per_tile
        bias = jnp.where(valid & (score == -jnp.inf), 0.0, NEG_INF)
        bias_ref[0:tiles, 0:blocks_per_tile, :] = bias.reshape(tiles, blocks_per_tile, Q_TILE)
        if blocks_per_tile < BF16_ROWS:
            bias_ref[0:tiles, blocks_per_tile:, :] = jnp.zeros((tiles, BF16_ROWS - blocks_per_tile, Q_TILE), F32)

    j_vis = (t0 + Q_TILE) // SLC_LEN
    step = n_slc // FRONT_CLASSES
    for k in range(FRONT_CLASSES):
        pl.when((j_vis > k * step) & (j_vis <= (k + 1) * step))(functools.partial(front, (k + 1) * step))

    zeros_tail = jnp.zeros((LANES - HEAD_DIM - BF16_ROWS, qw), BF16)

    def scores(slot, j):
        k0 = pl.multiple_of(j * KV_TILE, KV_TILE)
        rhs = jnp.concatenate([qt, lanes4(bias_ref[j]).astype(BF16), zeros_tail], axis=0)
        s_ref[slot] = _dot(ks_ref[0, pl.ds(k0, KV_TILE), :], rhs)

    def accumulate(s, j):
        k0 = pl.multiple_of(j * KV_TILE, KV_TILE)
        m_old = m_ref[...]
        m_new = jnp.maximum(m_old, jnp.max(s, axis=0, keepdims=True))
        p = jnp.exp2(s - m_new).astype(BF16)
        acc_ref[...] = jnp.exp2(m_old - m_new) * acc_ref[...] + _dot(vst_ref[0, :, pl.ds(k0, KV_TILE)], p)
        m_ref[...] = m_new

    m_ref[...] = jnp.full((1, qw), NEG_INF, F32)
    acc_ref[...] = jnp.zeros((V_ROWS, qw), F32)
    n_full = t0 // KV_TILE
    scores(0, 0)

    def pair(j):
        scores(1, j + 1)
        accumulate(s_ref[0], j)
        scores(0, j + 2)
        accumulate(s_ref[1], j + 1)

    def two_pairs(i, _):
        pair(4 * i)
        pair(4 * i + 2)
        return 0

    n_pairs = n_full // 2
    lax.fori_loop(0, n_pairs // 2, two_pairs, 0)

    @pl.when(n_pairs % 2 == 1)
    def _():
        pair(2 * (n_pairs - 1))

    odd = n_full % 2
    tok = n_full * KV_TILE + lax.broadcasted_iota(jnp.int32, (KV_TILE, 1), 0)

    def last_tile(slot):
        accumulate(jnp.where(tok <= t_lane, s_ref[slot], NEG_INF), n_full)

    @pl.when(odd == 1)
    def _():
        scores(1, n_full)
        accumulate(s_ref[0], n_full - 1)
        last_tile(1)

    @pl.when(odd == 0)
    def _():
        last_tile(0)

    acc_s = acc_ref[...]
    o_s = acc_s[:HEAD_DIM] * (1.0 / jnp.maximum(acc_s[HEAD_DIM:HEAD_DIM + 1], 1e-30))

    gates = jax.nn.sigmoid(ng_ref[...])

    def gate(b):
        return jnp.concatenate([gates[b * Q_PER_KV + g:b * Q_PER_KV + g + 1] for g in range(Q_PER_KV)], axis=1)

    o = gate(0) * oc_ref[...] + gate(1) * o_s + gate(2) * ow_ref[...]
    pairs = []
    for g in range(0, Q_PER_KV, 2):
        two = jnp.concatenate([o[:, g * Q_TILE:(g + 1) * Q_TILE], o[:, (g + 1) * Q_TILE:(g + 2) * Q_TILE]], axis=0)
        pairs.append(two.T)
    out_ref[...] = jnp.concatenate(pairs, axis=1).astype(out_ref.dtype)


def _window_bias_table():
    r = jnp.arange(2 * WINDOW + Q_TILE, dtype=jnp.int32)[:, None]
    q = jnp.arange(Q_TILE, dtype=jnp.int32)[None, :]
    return jnp.where((r > q) & (r <= WINDOW + q), 0.0, NEG_INF).astype(F32)


def _attention(qt, kc, vct, ks, vst, kw, vwt, ngt, seq):
    wb = _window_bias_table()
    n_slc = seq // SLC_LEN
    nb = seq // Q_TILE
    qw = Q_PER_KV * Q_TILE
    n_rows = CMP_PER_SLC * n_slc
    topk = min(SLC_TOPK, n_slc)

    def per_head(block):
        return pl.BlockSpec(block, lambda k, c: (k, 0, 0), pipeline_mode=pl.Buffered(1))

    return pl.pallas_call(
        functools.partial(_attn_kernel, n_slc=n_slc, topk=topk),
        grid=(N_KV_HEADS, nb),
        in_specs=[
            pl.BlockSpec((Q_PER_KV * HEAD_DIM, Q_TILE), lambda k, c: (k, c)),
            per_head((1, n_rows, HEAD_DIM)),
            per_head((1, HEAD_DIM, n_rows)),
            per_head((1, seq, LANES)),
            per_head((1, V_ROWS, seq)),
            per_head((1, seq, LANES)),
            per_head((1, V_ROWS, seq)),
            pl.BlockSpec((GATE_ROWS, Q_TILE), lambda k, c: (k, c)),
            pl.BlockSpec(wb.shape, lambda k, c: (0, 0), pipeline_mode=pl.Buffered(1)),
        ],
        out_specs=pl.BlockSpec((Q_TILE, Q_PER_KV * HEAD_DIM), lambda k, c: (c, k)),
        out_shape=jax.ShapeDtypeStruct((seq, N_HEADS * HEAD_DIM), BF16),
        scratch_shapes=[
            pltpu.VMEM((n_slc * SLC_LEN // KV_TILE, BF16_ROWS, Q_TILE), F32),
            pltpu.VMEM((SCORE_SLOTS, KV_TILE, qw), F32),
            pltpu.VMEM((V_ROWS, qw), F32),
            pltpu.VMEM((1, qw), F32),
            pltpu.VMEM((HEAD_DIM, qw), F32),
            pltpu.VMEM((HEAD_DIM, qw), F32),
        ],
        compiler_params=pltpu.CompilerParams(
            dimension_semantics=("arbitrary", "arbitrary"), vmem_limit_bytes=VMEM_LIMIT_BYTES),
        name="nsa_attention",
    )(qt, kc, vct, ks, vst, kw, vwt, ngt, wb)


def _discretize_kernel(are_ref, aim_ref, ldt_ref, bre_ref, bim_ref, ar_ref, ai_ref, br_ref, bi_ref):
    a_re, a_im = are_ref[...], aim_ref[...]
    dt = jnp.exp(ldt_ref[...])
    mag = jnp.exp(a_re * dt)
    ang = a_im * dt
    ab_re, ab_im = mag * jnp.cos(ang), mag * jnp.sin(ang)
    n_re, n_im = ab_re - 1.0, ab_im
    den = a_re * a_re + a_im * a_im
    c_re = (n_re * a_re + n_im * a_im) / den
    c_im = (n_im * a_re - n_re * a_im) / den
    b_re, b_im = bre_ref[...], bim_ref[...]
    ar_ref[...] = ab_re
    ai_ref[...] = ab_im
    br_ref[...] = c_re * b_re - c_im * b_im
    bi_ref[...] = c_re * b_im + c_im * b_re


def _discretize(a_re, a_im, log_dt, b_re, b_im):
    g, p, h = b_re.shape
    col = jax.ShapeDtypeStruct((g, p, 1), F32)
    full = jax.ShapeDtypeStruct((g, p, h), F32)
    return pl.pallas_call(
        _discretize_kernel, out_shape=[col, col, full, full], name="ssm_discretize",
    )(a_re[..., None], a_im[..., None], log_dt[:, None, None], b_re, b_im)


def _cmul(ar, ai, br, bi):
    return ar * br - ai * bi, ar * bi + ai * br


def _s5_kernel(u_ref, bblk_ref, a_ref, cblk_ref, d_ref, wglu_ref, y_ref, xr_ref, xi_ref, st_ref, pw_ref):
    seg = SCAN_SEGMENTS
    steps = xr_ref.shape[0]
    n_state = xr_ref.shape[2]
    n_blk, blk_in, blk_state2 = bblk_ref.shape
    blk_state = blk_state2 // 2

    @pl.when(pl.program_id(0) == 0)
    def _():
        st_ref[...] = jnp.zeros_like(st_ref)
        wr, wi = a_ref[0:1, :], a_ref[1:2, :]
        for j in range(steps):
            pw_ref[j, 0] = jnp.broadcast_to(wr, (seg, n_state))
            pw_ref[j, 1] = jnp.broadcast_to(wi, (seg, n_state))
            wr, wi = _cmul(wr, wi, a_ref[0:1, :], a_ref[1:2, :])

    u = jnp.concatenate(
        [jnp.concatenate([u_ref[k, pl.ds(j, seg, stride=steps), :] for j in range(steps)], axis=0)
         for k in range(u_ref.shape[0])], axis=1)
    ub = u.astype(BF16)
    for k in range(n_blk):
        bu = _dot(ub[:, k * blk_in:(k + 1) * blk_in], bblk_ref[k])
        xr_ref[:, :, k * blk_state:(k + 1) * blk_state] = bu[:, :blk_state].reshape(steps, seg, blk_state)
        xi_ref[:, :, k * blk_state:(k + 1) * blk_state] = bu[:, blk_state:].reshape(steps, seg, blk_state)

    lb = SCAN_LANE_BLOCK
    for b0 in range(0, n_state, lb):
        lanes = slice(b0, b0 + lb)
        a_re = a_ref[0:1, lanes]
        a_im = a_ref[1:2, lanes]
        ar8 = jnp.broadcast_to(a_re, (seg, lb))
        ai8 = jnp.broadcast_to(a_im, (seg, lb))

        def local(j, carry):
            xr, xi = carry
            pr, pi = _cmul(ar8, ai8, xr, xi)
            xr = pr + xr_ref[j, :, lanes]
            xi = pi + xi_ref[j, :, lanes]
            xr_ref[j, :, lanes] = xr
            xi_ref[j, :, lanes] = xi
            return xr, xi

        end_r, end_i = lax.fori_loop(0, steps, local, (jnp.zeros((seg, lb), F32), jnp.zeros((seg, lb), F32)))

        pr, pi = a_re, a_im
        for _ in range(steps.bit_length() - 1):
            pr, pi = _cmul(pr, pi, pr, pi)
        cr, ci = st_ref[0:1, lanes], st_ref[1:2, lanes]
        in_r, in_i = [], []
        for s in range(seg):
            in_r.append(cr)
            in_i.append(ci)
            qr, qi = _cmul(pr, pi, cr, ci)
            cr, ci = qr + end_r[s:s + 1], qi + end_i[s:s + 1]
        st_ref[0:1, lanes] = cr
        st_ref[1:2, lanes] = ci
        in_r = jnp.concatenate(in_r, axis=0)
        in_i = jnp.concatenate(in_i, axis=0)

        def fix(j, carry):
            fr, fi = _cmul(pw_ref[j, 0, :, lanes], pw_ref[j, 1, :, lanes], in_r, in_i)
            xr_ref[j, :, lanes] = xr_ref[j, :, lanes] + fr
            xi_ref[j, :, lanes] = xi_ref[j, :, lanes] + fi
            return carry

        lax.fori_loop(0, steps, fix, 0)

    rows = steps * seg
    y = d_ref[...] * u
    for k in range(n_blk):
        st = slice(k * blk_state, (k + 1) * blk_state)
        xs = jnp.concatenate([xr_ref[:, :, st].reshape(rows, blk_state),
                              xi_ref[:, :, st].reshape(rows, blk_state)], axis=1).astype(BF16)
        yk = _dot(xs, cblk_ref[k])
        if k == 0:
            ys = [yk]
        else:
            ys.append(yk)
    y = y + jnp.concatenate(ys, axis=1)
    y = jax.nn.gelu(y)
    y = y * jax.nn.sigmoid(_dot(y.astype(BF16), wglu_ref[...]))
    y_ref[...] = y.astype(y_ref.dtype)


def _s5(u_perm, bblk, a_lane, cblk, d_row, w_glu):
    n_slab, seq, _ = u_perm.shape
    width = n_slab * LANES
    tm = ROW_TILE
    steps = tm // SCAN_SEGMENTS
    n_state = a_lane.shape[1]
    const2 = lambda i: (0, 0)
    const3 = lambda i: (0, 0, 0)
    return pl.pallas_call(
        _s5_kernel,
        grid=(seq // tm,),
        in_specs=[
            pl.BlockSpec((n_slab, tm, LANES), lambda i: (0, i, 0)),
            pl.BlockSpec(bblk.shape, const3),
            pl.BlockSpec(a_lane.shape, const2),
            pl.BlockSpec(cblk.shape, const3),
            pl.BlockSpec((1, width), const2),
            pl.BlockSpec((width, width), const2),
        ],
        out_specs=pl.BlockSpec((tm, width), lambda i: (i, 0)),
        out_shape=jax.ShapeDtypeStruct((seq, width), BF16),
        scratch_shapes=[
            pltpu.VMEM((steps, SCAN_SEGMENTS, n_state), F32),
            pltpu.VMEM((steps, SCAN_SEGMENTS, n_state), F32),
            pltpu.VMEM((2, n_state), F32),
            pltpu.VMEM((steps, 2, SCAN_SEGMENTS, n_state), F32),
        ],
        compiler_params=pltpu.CompilerParams(
            dimension_semantics=("arbitrary",), vmem_limit_bytes=VMEM_LIMIT_BYTES),
        name="s5_scan",
    )(u_perm, bblk, a_lane, cblk, d_row, w_glu)


def _ffn_kernel(x_ref, ya_ref, yb_ref, gmix_ref, wg_ref, wpa_ref, wpb_ref, wout_ref, gffn_ref,
                wup_ref, cw_ref, cb_ref, wdown_ref, gfin_ref, o_ref, tail_ref, *, d_ff, chunk):
    tm, d_model = x_ref.shape

    @pl.when(pl.program_id(0) == 0)
    def _():
        tail_ref[...] = jnp.zeros_like(tail_ref)

    x = x_ref[...]
    h = _rmsnorm(x, gmix_ref[...]).astype(BF16)
    gates = jax.nn.sigmoid(_dot(h, wg_ref[...]))
    merged = (gates[:, :d_model] * _dot(ya_ref[...], wpa_ref[...])
              + gates[:, d_model:] * _dot(yb_ref[...], wpb_ref[...]))
    x1 = x + _dot(merged.astype(BF16), wout_ref[...])
    h2 = _rmsnorm(x1, gffn_ref[...]).astype(BF16)

    def conv_cols(c0):
        cols = slice(c0, c0 + chunk)
        up = _dot(h2, wup_ref[:, cols])
        ext = jnp.concatenate([tail_ref[:, cols], up], axis=0)
        tail_ref[:, cols] = up[tm - SUBLANES:, :]
        prev1 = pltpu.roll(ext, 1, axis=0)[SUBLANES:]
        prev2 = pltpu.roll(ext, 2, axis=0)[SUBLANES:]
        return cb_ref[:, cols] + cw_ref[0:1, cols] * prev2 + cw_ref[1:2, cols] * prev1 + cw_ref[2:3, cols] * up

    acc = x1
    for k in range(d_ff // chunk):
        gate = conv_cols(k * chunk)
        val = conv_cols(d_ff + k * chunk)
        act = (jax.nn.silu(gate) * val).astype(BF16)
        acc = acc + _dot(act, wdown_ref[k * chunk:(k + 1) * chunk, :])
    o_ref[...] = _rmsnorm(acc, gfin_ref[...])


def _ffn(x2, ya, yb, g_mix, w_g, w_pa, w_pb, w_out, g_ffn, w_up, conv_w, conv_b, w_down, g_fin):
    seq, d_model = x2.shape
    d_ff = w_down.shape[0]
    tm = FFN_ROW_TILE
    chunk = FFN_COL_CHUNK
    assert d_ff % chunk == 0 and chunk % LANES == 0
    row = lambda i: (i, 0)

    def resident(arr):
        return pl.BlockSpec(arr.shape, lambda i: (0, 0), pipeline_mode=pl.Buffered(1))

    return pl.pallas_call(
        functools.partial(_ffn_kernel, d_ff=d_ff, chunk=chunk),
        grid=(seq // tm,),
        in_specs=[
            pl.BlockSpec((tm, d_model), row),
            pl.BlockSpec((tm, ya.shape[1]), row),
            pl.BlockSpec((tm, yb.shape[1]), row),
            resident(g_mix), resident(w_g), resident(w_pa), resident(w_pb), resident(w_out), resident(g_ffn),
            resident(w_up), resident(conv_w), resident(conv_b), resident(w_down), resident(g_fin),
        ],
        out_specs=pl.BlockSpec((tm, d_model), row),
        out_shape=jax.ShapeDtypeStruct((seq, d_model), F32),
        scratch_shapes=[pltpu.VMEM((SUBLANES, 2 * d_ff), F32)],
        compiler_params=pltpu.CompilerParams(
            dimension_semantics=("arbitrary",), vmem_limit_bytes=VMEM_LIMIT_BYTES),
        name="merge_ffn",
    )(x2, ya, yb, g_mix, w_g, w_pa, w_pb, w_out, g_ffn, w_up, conv_w, conv_b, w_down, g_fin)


def _block_diag(m, n_blk):
    g, r, c = m.shape
    per = g // n_blk
    eye = jnp.eye(per, dtype=m.dtype)
    m = m.reshape(n_blk, per, r, c)
    return jnp.einsum("bgrc,gk->bgrkc", m, eye).reshape(n_blk, per * r, per * c)


def kernel(x, positions, norm_mix, w_in, cmp_k_pe, cmp_k_w1, cmp_k_w2, cmp_v_pe, cmp_v_w1, cmp_v_w2, ssm_a_re, ssm_a_im, ssm_b_re, ssm_b_im, ssm_c_re, ssm_c_im, ssm_d, ssm_log_dt, ssm_w_glu, w_proj_a, w_proj_b, w_out, norm_ffn, w_up, conv_w, conv_b, w_down, norm_final):
    bsz, seq, d_model = x.shape
    assert bsz == 1 and norm_mix.shape[0] == 1
    assert seq % max(KV_TILE, ROW_TILE) == 0 and KV_TILE // SLC_LEN <= BF16_ROWS and seq >= WINDOW + Q_TILE
    attn_w = N_HEADS * HEAD_DIM
    kv_w = N_KV_HEADS * HEAD_DIM
    ssm_width = ssm_d.shape[1]
    x2 = x[0]
    pos = positions[0]

    sizes = (attn_w,) + (kv_w,) * 6 + (3 * N_HEADS, ssm_width, d_model, d_model)
    offs = [0]
    for s in sizes:
        offs.append(offs[-1] + s)
    w = w_in[0]
    col = lambda i: w[:, offs[i]:offs[i + 1]]
    w_q, w_kc, w_vc, w_ks, w_vs, w_kw, w_vw, w_ng, w_u, w_ga, w_gb = [col(i) for i in range(11)]
    w_ng = w_ng.reshape(d_model, N_KV_HEADS, Q_PER_KV, 3).transpose(0, 1, 3, 2).reshape(d_model, N_KV_HEADS, 3 * Q_PER_KV)
    w_ng = jnp.pad(w_ng, ((0, 0), (0, 0), (0, GATE_ROWS - 3 * Q_PER_KV))).reshape(d_model, N_KV_HEADS * GATE_ROWS)
    w_t = jnp.concatenate([w_q, w_ks, w_kw, w_vs, w_vw, w_ng], axis=1).T.astype(BF16)
    w_nat = jnp.concatenate([w_kc, w_vc, w_u], axis=1).astype(BF16)
    w_gate = jnp.concatenate([w_ga, w_gb], axis=1).astype(BF16)

    inv_freq = ROPE_THETA ** (-jnp.arange(0, ROT_DIM, 2, dtype=F32) / ROT_DIM)
    qt, ks, kw, vst, vwt, ngt, kvc, u = _inproj(x2, norm_mix, pos[None, :], inv_freq[:, None], w_nat, w_t)

    n_half = seq // CMP_STRIDE
    n_cmp = (seq - CMP_LEN) // CMP_STRIDE + 1
    h4 = kvc.reshape(seq, 2 * N_KV_HEADS, HEAD_DIM).transpose(1, 0, 2).reshape(
        2 * N_KV_HEADS, n_half, CMP_STRIDE * HEAD_DIM)
    halves = lambda pe: pe.reshape(2, CMP_STRIDE * HEAD_DIM)
    pe4 = jnp.stack([halves(cmp_k_pe[0])] * N_KV_HEADS + [halves(cmp_v_pe[0])] * N_KV_HEADS)
    w1 = jnp.stack([cmp_k_w1[0], cmp_v_w1[0]]).astype(BF16)
    w2 = jnp.pad(jnp.stack([cmp_k_w2[0], cmp_v_w2[0]]), ((0, 0), (0, 0), (0, LANES - HEAD_DIM))).astype(BF16)
    cmp_end = jnp.arange(n_cmp, dtype=jnp.int32) * CMP_STRIDE + CMP_LEN - 1
    cmp_pos = jnp.pad(pos[cmp_end], (0, n_half - n_cmp))
    pat = _rope_lane_patterns()
    pat2 = jnp.stack([pat, jnp.zeros_like(pat)])
    cmp = _compress(h4, pe4, w1, w2, cmp_pos[:, None], pat2)[:, :, :HEAD_DIM]
    n_slc = seq // SLC_LEN
    cmp = cmp.reshape(2 * N_KV_HEADS, n_slc, CMP_PER_SLC, HEAD_DIM).transpose(0, 2, 1, 3).reshape(
        2 * N_KV_HEADS, n_half, HEAD_DIM).astype(BF16)
    kc = cmp[:N_KV_HEADS]
    vct = cmp[N_KV_HEADS:].transpose(0, 2, 1)

    y_a = _attention(qt, kc, vct, ks, vst, kw, vwt, ngt, seq)

    ab_re, ab_im, bb_re, bb_im = _discretize(ssm_a_re[0], ssm_a_im[0], ssm_log_dt[0], ssm_b_re[0], ssm_b_im[0])
    n_blk = 2
    a_lane = jnp.stack([ab_re.reshape(-1), ab_im.reshape(-1)])
    to_in = lambda b: _block_diag(b.transpose(0, 2, 1), n_blk)
    bblk = jnp.concatenate([to_in(bb_re), to_in(bb_im)], axis=2).astype(BF16)
    to_out = lambda cm: _block_diag(cm.transpose(0, 2, 1), n_blk)
    cblk = jnp.concatenate([to_out(ssm_c_re[0]), -to_out(ssm_c_im[0])], axis=1).astype(BF16)
    steps = ROW_TILE // SCAN_SEGMENTS
    nt = seq // ROW_TILE
    y_b = _s5(u, bblk, a_lane, cblk, ssm_d, ssm_w_glu[0].astype(BF16))
    y_b = y_b.reshape(nt, steps, SCAN_SEGMENTS, ssm_width).transpose(0, 2, 1, 3).reshape(seq, ssm_width)

    out = _ffn(x2, y_a, y_b, norm_mix, w_gate, w_proj_a[0].astype(BF16), w_proj_b[0].astype(BF16),
               w_out[0].astype(BF16), norm_ffn, w_up[0].astype(BF16), conv_w[0], conv_b, w_down[0].astype(BF16),
               norm_final[None, :])
    return out[None]
```

```python
import functools

import jax
import jax.numpy as jnp
from jax import lax
from jax.experimental import pallas as pl
from jax.experimental.pallas import tpu as pltpu

F32 = jnp.float32
BF16 = jnp.bfloat16

N_HEADS = 8
N_KV_HEADS = 2
HEAD_DIM = 64
Q_PER_KV = N_HEADS // N_KV_HEADS
ROT_DIM = HEAD_DIM // 4
ROPE_THETA = 500000.0
CMP_LEN = 32
CMP_STRIDE = 16
CMP_PER_SLC = 4
SLC_LEN = 64
SLC_TOPK = 16
WINDOW = 512
SSM_GROUP = 16
SSM_STATE = 64
CONV_WIDTH = 3
EPS = 1e-6
NEG_INF = -1e30
FORCE_SCORE = 1e4
LOG2E = 1.4426950408889634

LANES = 128
SUBLANES = 8
BF16_ROWS = 16
VMEM_BYTES_V7X = 64 * 1024 * 1024
VMEM_LIMIT_BYTES = VMEM_BYTES_V7X * 7 // 8

ROW_TILE = 512
FFN_ROW_TILE = 256
FFN_COL_CHUNK = 1408
Q_TILE = 256
KV_TILE = 512
SCORE_SLOTS = 2
FRONT_CLASSES = 8
SCAN_SEGMENTS = SUBLANES
SCAN_LANE_BLOCK = 1024


def _rmsnorm(x, g):
    return x * lax.rsqrt(jnp.mean(x * x, axis=-1, keepdims=True) + EPS) * g


def _dot(a, b):
    return jnp.dot(a, b, preferred_element_type=F32)


def _dot_nt(a, b):
    return lax.dot_general(a, b, (((1,), (1,)), ((), ())), preferred_element_type=F32)


def _rope_lane_patterns():
    half = ROT_DIM // 2
    d = jnp.arange(LANES) % HEAD_DIM
    inv_freq = ROPE_THETA ** (-jnp.arange(0, ROT_DIM, 2, dtype=F32) / ROT_DIM)
    freq = jnp.where(d < ROT_DIM, inv_freq[d % half], 0.0)
    m_lo = jnp.where(d < half, -1.0, 0.0)
    m_hi = jnp.where((d >= half) & (d < ROT_DIM), 1.0, 0.0)
    return jnp.stack([freq, m_lo, m_hi]).astype(F32)


def _rope_tables(pos_col, pat):
    ang = pos_col.astype(F32) * pat[0:1]
    sin = jnp.sin(ang)
    return jnp.cos(ang), sin * pat[1:2], sin * pat[2:3]


def _rope(x, cos, sin_lo, sin_hi):
    half = ROT_DIM // 2
    return (x * cos + pltpu.roll(x, LANES - half, axis=1) * sin_lo
            + pltpu.roll(x, half, axis=1) * sin_hi)


KVC_COLS = 2 * N_KV_HEADS * HEAD_DIM
V_ROWS = HEAD_DIM + BF16_ROWS
GATE_ROWS = 16


def _rope_t(xh, cos, sin):
    half = ROT_DIM // 2
    x1, x2 = xh[0:half], xh[half:ROT_DIM]
    return jnp.concatenate([x1 * cos - x2 * sin, x2 * cos + x1 * sin, xh[ROT_DIM:]], axis=0)


def _inproj_kernel(x_ref, g_ref, pos_ref, invf_ref, wn_ref, wt_ref,
                   qt_ref, ks_ref, kw_ref, vst_ref, vwt_ref, ngt_ref, kvc_ref, u_ref):
    tm = x_ref.shape[0]
    h = _rmsnorm(x_ref[...], g_ref[...]).astype(BF16)
    pn = _dot(h, wn_ref[...])
    kvc_ref[...] = pn[:, :KVC_COLS]
    for k in range(u_ref.shape[0]):
        u_ref[k] = pn[:, KVC_COLS + k * LANES:KVC_COLS + (k + 1) * LANES]

    pt = _dot_nt(wt_ref[...], h)
    ang = invf_ref[...] * pos_ref[...].astype(F32)
    cos, sin = jnp.cos(ang), jnp.sin(ang)
    q_scale = (HEAD_DIM ** -0.5) * LOG2E
    for hd in range(N_HEADS):
        rows = slice(hd * HEAD_DIM, (hd + 1) * HEAD_DIM)
        qt_ref[rows, :] = (_rope_t(pt[rows], cos, sin) * q_scale).astype(BF16)

    kv_w = N_KV_HEADS * HEAD_DIM
    r_ks = N_HEADS * HEAD_DIM
    r_kw, r_vs, r_vw, r_ng = r_ks + kv_w, r_ks + 2 * kv_w, r_ks + 3 * kv_w, r_ks + 4 * kv_w
    key = pl.program_id(0) * tm + lax.broadcasted_iota(jnp.int32, (BF16_ROWS, tm), 1)
    blk = lax.broadcasted_iota(jnp.int32, (BF16_ROWS, tm), 0)
    onehot = jnp.where((key % KV_TILE) // SLC_LEN == blk, 1.0, 0.0)
    ones_rows = jnp.where(blk == 0, 1.0, 0.0)
    pad_k = jnp.zeros((LANES - HEAD_DIM - BF16_ROWS, tm), F32)
    pad_w = jnp.zeros((LANES - HEAD_DIM, tm), F32)
    for kh in range(N_KV_HEADS):
        off = kh * HEAD_DIM
        k_s = _rope_t(pt[r_ks + off:r_ks + off + HEAD_DIM], cos, sin)
        ks_ref[kh] = jnp.concatenate([k_s, onehot, pad_k], axis=0).T.astype(BF16)
        k_w = _rope_t(pt[r_kw + off:r_kw + off + HEAD_DIM], cos, sin)
        kw_ref[kh] = jnp.concatenate([k_w, pad_w], axis=0).T.astype(BF16)
        vst_ref[kh] = jnp.concatenate([pt[r_vs + off:r_vs + off + HEAD_DIM], ones_rows], axis=0).astype(BF16)
        vwt_ref[kh] = jnp.concatenate([pt[r_vw + off:r_vw + off + HEAD_DIM], ones_rows], axis=0).astype(BF16)
    ngt_ref[...] = pt[r_ng:r_ng + N_KV_HEADS * GATE_ROWS]


def _inproj(x2, g, pos_row, inv_freq, w_nat, w_t):
    seq, d_model = x2.shape
    tm = ROW_TILE
    n_nat = w_nat.shape[1]
    n_t = w_t.shape[0]
    attn_w = N_HEADS * HEAD_DIM
    row = lambda i: (i, 0)
    col = lambda i: (0, i)
    const = lambda i: (0, 0)
    head_rows = lambda i: (0, i, 0)
    head_cols = lambda i: (0, 0, i)
    return pl.pallas_call(
        _inproj_kernel,
        grid=(seq // tm,),
        in_specs=[
            pl.BlockSpec((tm, d_model), row),
            pl.BlockSpec((1, d_model), const),
            pl.BlockSpec((1, tm), col),
            pl.BlockSpec((ROT_DIM // 2, 1), const),
            pl.BlockSpec((d_model, n_nat), const),
            pl.BlockSpec((n_t, d_model), const),
        ],
        out_specs=[
            pl.BlockSpec((attn_w, tm), col),
            pl.BlockSpec((N_KV_HEADS, tm, LANES), head_rows),
            pl.BlockSpec((N_KV_HEADS, tm, LANES), head_rows),
            pl.BlockSpec((N_KV_HEADS, V_ROWS, tm), head_cols),
            pl.BlockSpec((N_KV_HEADS, V_ROWS, tm), head_cols),
            pl.BlockSpec((N_KV_HEADS * GATE_ROWS, tm), col),
            pl.BlockSpec((tm, KVC_COLS), row),
            pl.BlockSpec(((n_nat - KVC_COLS) // LANES, tm, LANES), head_rows),
        ],
        out_shape=[
            jax.ShapeDtypeStruct((attn_w, seq), BF16),
            jax.ShapeDtypeStruct((N_KV_HEADS, seq, LANES), BF16),
            jax.ShapeDtypeStruct((N_KV_HEADS, seq, LANES), BF16),
            jax.ShapeDtypeStruct((N_KV_HEADS, V_ROWS, seq), BF16),
            jax.ShapeDtypeStruct((N_KV_HEADS, V_ROWS, seq), BF16),
            jax.ShapeDtypeStruct((N_KV_HEADS * GATE_ROWS, seq), F32),
            jax.ShapeDtypeStruct((seq, KVC_COLS), F32),
            jax.ShapeDtypeStruct(((n_nat - KVC_COLS) // LANES, seq, LANES), F32),
        ],
        compiler_params=pltpu.CompilerParams(
            dimension_semantics=("arbitrary",), vmem_limit_bytes=VMEM_LIMIT_BYTES),
        name="inproj",
    )(x2, g, pos_row, inv_freq, w_nat, w_t)


def _compress_kernel(h_ref, pe_ref, w1_ref, w2_ref, pos_ref, pat_ref, out_ref):
    hh = h_ref[0]
    rows, half_w = hh.shape
    a = _dot((hh + pe_ref[0, 0:1]).astype(BF16), w1_ref[0, :half_w])
    b = _dot((hh + pe_ref[0, 1:2]).astype(BF16), w1_ref[0, half_w:])
    pre = a + pltpu.roll(b, rows - 1, axis=0)
    z = _dot(jax.nn.gelu(pre).astype(BF16), w2_ref[0])
    cos, sin_lo, sin_hi = _rope_tables(pos_ref[...], pat_ref[0])
    out_ref[0] = _rope(z, cos, sin_lo, sin_hi)


def _compress(h4, pe4, w1, w2, cmp_pos_col, pat2):
    n, rows, half_w = h4.shape
    hidden = w1.shape[2]
    per = n // 2
    return pl.pallas_call(
        _compress_kernel,
        grid=(n,),
        in_specs=[
            pl.BlockSpec((1, rows, half_w), lambda i: (i, 0, 0)),
            pl.BlockSpec((1, 2, half_w), lambda i: (i, 0, 0)),
            pl.BlockSpec((1, 2 * half_w, hidden), lambda i: (i // per, 0, 0)),
            pl.BlockSpec((1, hidden, LANES), lambda i: (i // per, 0, 0)),
            pl.BlockSpec((rows, 1), lambda i: (0, 0)),
            pl.BlockSpec((1, 3, LANES), lambda i: (i // per, 0, 0)),
        ],
        out_specs=pl.BlockSpec((1, rows, LANES), lambda i: (i, 0, 0)),
        out_shape=jax.ShapeDtypeStruct((n, rows, LANES), F32),
        compiler_params=pltpu.CompilerParams(
            dimension_semantics=("arbitrary",), vmem_limit_bytes=VMEM_LIMIT_BYTES),
        name="compress",
    )(h4, pe4, w1, w2, cmp_pos_col, pat2)


def _attn_kernel(qt_ref, kc_ref, vct_ref, ks_ref, vst_ref, kw_ref, vwt_ref, ng_ref, wb_ref,
                 out_ref, bias_ref, s_ref, acc_ref, m_ref, oc_ref, ow_ref, *, n_slc, topk):
    c = pl.program_id(1)
    qw = Q_PER_KV * Q_TILE
    t0 = c * Q_TILE
    t_lane = t0 + (lax.broadcasted_iota(jnp.int32, (1, qw), 1) & (Q_TILE - 1))
    qt4 = qt_ref[...]
    qt = jnp.concatenate([qt4[g * HEAD_DIM:(g + 1) * HEAD_DIM] for g in range(Q_PER_KV)], axis=1)

    def lanes4(x):
        return jnp.concatenate([x] * Q_PER_KV, axis=1)

    blocks_per_tile = KV_TILE // SLC_LEN
    q_lane = (lax.broadcasted_iota(jnp.int32, (1, qw), 1) & (Q_TILE - 1)).astype(F32)
    tq = t0 + lax.broadcasted_iota(jnp.int32, (1, Q_TILE), 1)

    def front(nj):
        rows = CMP_PER_SLC * nj
        kc = jnp.concatenate([kc_ref[0, r * n_slc:r * n_slc + nj, :] for r in range(CMP_PER_SLC)], axis=0)
        s_c = _dot(kc, qt)
        row = lax.broadcasted_iota(jnp.int32, (rows, 1), 0)
        n_idx = CMP_PER_SLC * (row % nj) + row // nj
        first_q = (n_idx * CMP_STRIDE + (CMP_LEN - 1) - t0).astype(F32)
        s_c = jnp.where(first_q <= q_lane, s_c, NEG_INF)
        m_c = jnp.max(s_c, axis=0, keepdims=True)
        e_c = jnp.exp2(s_c - m_c)
        inv_c = jnp.where(m_c > 0.5 * NEG_INF, 1.0 / jnp.maximum(jnp.sum(e_c, axis=0, keepdims=True), 1e-30), 0.0)
        e_b = e_c.astype(BF16)
        if nj < n_slc:
            gap = jnp.zeros((n_slc - nj, qw), BF16)
            e_b = jnp.concatenate(
                [piece for r in range(CMP_PER_SLC) for piece in (e_b[r * nj:(r + 1) * nj], gap)], axis=0)
        oc_ref[...] = _dot(vct_ref[0], e_b) * inv_c

        p_c = e_c * inv_c
        psum = p_c[:, 0:Q_TILE]
        for g in range(1, Q_PER_KV):
            psum = psum + p_c[:, g * Q_TILE:(g + 1) * Q_TILE]
        j_idx = lax.broadcasted_iota(jnp.int32, (nj, 1), 0)
        straddle = psum[3 * nj:4 * nj]
        from_prev = jnp.where(j_idx == 0, 0.0, pltpu.roll(straddle, 1, axis=0))
        imp = (psum[0:nj] + psum[nj:2 * nj] + psum[2 * nj:3 * nj]
               + 0.5 * straddle + 0.5 * from_prev)

        span = WINDOW + Q_TILE
        w0 = pl.multiple_of(jnp.maximum(t0 - WINDOW, 0), Q_TILE)
        rhs_w = jnp.concatenate([qt, jnp.zeros((LANES - HEAD_DIM, qw), BF16)], axis=0)
        wb_off = pl.multiple_of(WINDOW - (t0 - w0), Q_TILE)
        s_w = _dot(kw_ref[0, pl.ds(w0, span), :], rhs_w) + lanes4(wb_ref[pl.ds(wb_off, span), :])
        e_w = jnp.exp2(s_w - jnp.max(s_w, axis=0, keepdims=True)).astype(BF16)
        acc_w = _dot(vwt_ref[0, :, pl.ds(w0, span)], e_w)
        ow_ref[...] = acc_w[:HEAD_DIM] * (1.0 / jnp.maximum(acc_w[HEAD_DIM:HEAD_DIM + 1], 1e-30))

        forced = (j_idx == 0) | (j_idx == tq // SLC_LEN)
        valid = (j_idx * SLC_LEN) <= tq
        score = jnp.where(valid, jnp.where(forced, FORCE_SCORE, imp), NEG_INF)
        j_f = j_idx.astype(F32)
        for _ in range(min(topk, nj)):
            best = jnp.max(score, axis=0, keepdims=True)
            first = jnp.min(jnp.where(score == best, j_f, float(nj)), axis=0, keepdims=True)
            score = jnp.where(j_f == first, -jnp.inf, score)
        tiles = nj // blocks_per_tile
        bias = jnp.where(valid & (score == -jnp.inf), 0.0, NEG_INF)
        bias_ref[0:tiles, 0:blocks_per_tile, :] = bias.reshape(tiles, blocks_per_tile, Q_TILE)
        if blocks_per_tile < BF16_ROWS:
            bias_ref[0:tiles, blocks_per_tile:, :] = jnp.zeros((tiles, BF16_ROWS - blocks_per_tile, Q_TILE), F32)

    j_vis = (t0 + Q_TILE) // SLC_LEN
    step = n_slc // FRONT_CLASSES
    for k in range(FRONT_CLASSES):
        pl.when((j_vis > k * step) & (j_vis <= (k + 1) * step))(functools.partial(front, (k + 1) * step))

    zeros_tail = jnp.zeros((LANES - HEAD_DIM - BF16_ROWS, qw), BF16)

    def scores(slot, j):
        k0 = pl.multiple_of(j * KV_TILE, KV_TILE)
        rhs = jnp.concatenate([qt, lanes4(bias_ref[j]).astype(BF16), zeros_tail], axis=0)
        s_ref[slot] = _dot(ks_ref[0, pl.ds(k0, KV_TILE), :], rhs)

    def accumulate(s, j):
        k0 = pl.multiple_of(j * KV_TILE, KV_TILE)
        m_old = m_ref[...]
        m_new = jnp.maximum(m_old, jnp.max(s, axis=0, keepdims=True))
        p = jnp.exp2(s - m_new).astype(BF16)
        acc_ref[...] = jnp.exp2(m_old - m_new) * acc_ref[...] + _dot(vst_ref[0, :, pl.ds(k0, KV_TILE)], p)
        m_ref[...] = m_new

    m_ref[...] = jnp.full((1, qw), NEG_INF, F32)
    acc_ref[...] = jnp.zeros((V_ROWS, qw), F32)
    n_full = t0 // KV_TILE
    scores(0, 0)

    def pair(j):
        scores(1, j + 1)
        accumulate(s_ref[0], j)
        scores(0, j + 2)
        accumulate(s_ref[1], j + 1)

    def two_pairs(i, _):
        pair(4 * i)
        pair(4 * i + 2)
        return 0

    n_pairs = n_full // 2
    lax.fori_loop(0, n_pairs // 2, two_pairs, 0)

    @pl.when(n_pairs % 2 == 1)
    def _():
        pair(2 * (n_pairs - 1))

    odd = n_full % 2
    tok = n_full * KV_TILE + lax.broadcasted_iota(jnp.int32, (KV_TILE, 1), 0)

    def last_tile(slot):
        accumulate(jnp.where(tok <= t_lane, s_ref[slot], NEG_INF), n_full)

    @pl.when(odd == 1)
    def _():
        scores(1, n_full)
        accumulate(s_ref[0], n_full - 1)
        last_tile(1)

    @pl.when(odd == 0)
    def _():
        last_tile(0)

    acc_s = acc_ref[...]
    o_s = acc_s[:HEAD_DIM] * (1.0 / jnp.maximum(acc_s[HEAD_DIM:HEAD_DIM + 1], 1e-30))

    gates = jax.nn.sigmoid(ng_ref[...])

    def gate(b):
        return jnp.concatenate([gates[b * Q_PER_KV + g:b * Q_PER_KV + g + 1] for g in range(Q_PER_KV)], axis=1)

    o = gate(0) * oc_ref[...] + gate(1) * o_s + gate(2) * ow_ref[...]
    pairs = []
    for g in range(0, Q_PER_KV, 2):
        two = jnp.concatenate([o[:, g * Q_TILE:(g + 1) * Q_TILE], o[:, (g + 1) * Q_TILE:(g + 2) * Q_TILE]], axis=0)
        pairs.append(two.T)
    out_ref[...] = jnp.concatenate(pairs, axis=1).astype(out_ref.dtype)


def _window_bias_table():
    r = jnp.arange(2 * WINDOW + Q_TILE, dtype=jnp.int32)[:, None]
    q = jnp.arange(Q_TILE, dtype=jnp.int32)[None, :]
    return jnp.where((r > q) & (r <= WINDOW + q), 0.0, NEG_INF).astype(F32)


def _attention(qt, kc, vct, ks, vst, kw, vwt, ngt, seq):
    wb = _window_bias_table()
    n_slc = seq // SLC_LEN
    nb = seq // Q_TILE
    qw = Q_PER_KV * Q_TILE
    n_rows = CMP_PER_SLC * n_slc
    topk = min(SLC_TOPK, n_slc)
    assert (n_slc // FRONT_CLASSES) % (KV_TILE // SLC_LEN) == 0

    def per_head(block):
        return pl.BlockSpec(block, lambda k, c: (k, 0, 0), pipeline_mode=pl.Buffered(1))

    return pl.pallas_call(
        functools.partial(_attn_kernel, n_slc=n_slc, topk=topk),
        grid=(N_KV_HEADS, nb),
        in_specs=[
            pl.BlockSpec((Q_PER_KV * HEAD_DIM, Q_TILE), lambda k, c: (k, c)),
            per_head((1, n_rows, HEAD_DIM)),
            per_head((1, HEAD_DIM, n_rows)),
            per_head((1, seq, LANES)),
            per_head((1, V_ROWS, seq)),
            per_head((1, seq, LANES)),
            per_head((1, V_ROWS, seq)),
            pl.BlockSpec((GATE_ROWS, Q_TILE), lambda k, c: (k, c)),
            pl.BlockSpec(wb.shape, lambda k, c: (0, 0), pipeline_mode=pl.Buffered(1)),
        ],
        out_specs=pl.BlockSpec((Q_TILE, Q_PER_KV * HEAD_DIM), lambda k, c: (c, k)),
        out_shape=jax.ShapeDtypeStruct((seq, N_HEADS * HEAD_DIM), BF16),
        scratch_shapes=[
            pltpu.VMEM((n_slc * SLC_LEN // KV_TILE, BF16_ROWS, Q_TILE), F32),
            pltpu.VMEM((SCORE_SLOTS, KV_TILE, qw), F32),
            pltpu.VMEM((V_ROWS, qw), F32),
            pltpu.VMEM((1, qw), F32),
            pltpu.VMEM((HEAD_DIM, qw), F32),
            pltpu.VMEM((HEAD_DIM, qw), F32),
        ],
        compiler_params=pltpu.CompilerParams(
            dimension_semantics=("arbitrary", "arbitrary"), vmem_limit_bytes=VMEM_LIMIT_BYTES),
        name="nsa_attention",
    )(qt, kc, vct, ks, vst, kw, vwt, ngt, wb)


def _discretize_kernel(are_ref, aim_ref, ldt_ref, bre_ref, bim_ref, ar_ref, ai_ref, br_ref, bi_ref):
    a_re, a_im = are_ref[...], aim_ref[...]
    dt = jnp.exp(ldt_ref[...])
    mag = jnp.exp(a_re * dt)
    ang = a_im * dt
    ab_re, ab_im = mag * jnp.cos(ang), mag * jnp.sin(ang)
    n_re, n_im = ab_re - 1.0, ab_im
    den = a_re * a_re + a_im * a_im
    c_re = (n_re * a_re + n_im * a_im) / den
    c_im = (n_im * a_re - n_re * a_im) / den
    b_re, b_im = bre_ref[...], bim_ref[...]
    ar_ref[...] = ab_re
    ai_ref[...] = ab_im
    br_ref[...] = c_re * b_re - c_im * b_im
    bi_ref[...] = c_re * b_im + c_im * b_re


def _discretize(a_re, a_im, log_dt, b_re, b_im):
    g, p, h = b_re.shape
    col = jax.ShapeDtypeStruct((g, p, 1), F32)
    full = jax.ShapeDtypeStruct((g, p, h), F32)
    return pl.pallas_call(
        _discretize_kernel, out_shape=[col, col, full, full], name="ssm_discretize",
    )(a_re[..., None], a_im[..., None], log_dt[:, None, None], b_re, b_im)


def _cmul(ar, ai, br, bi):
    return ar * br - ai * bi, ar * bi + ai * br


def _s5_kernel(u_ref, bblk_ref, a_ref, cblk_ref, d_ref, wglu_ref, y_ref, xr_ref, xi_ref, st_ref, pw_ref):
    seg = SCAN_SEGMENTS
    steps = xr_ref.shape[0]
    n_state = xr_ref.shape[2]
    n_blk, blk_in, blk_state2 = bblk_ref.shape
    blk_state = blk_state2 // 2

    @pl.when(pl.program_id(0) == 0)
    def _():
        st_ref[...] = jnp.zeros_like(st_ref)
        wr, wi = a_ref[0:1, :], a_ref[1:2, :]
        for j in range(steps):
            pw_ref[j, 0] = jnp.broadcast_to(wr, (seg, n_state))
            pw_ref[j, 1] = jnp.broadcast_to(wi, (seg, n_state))
            wr, wi = _cmul(wr, wi, a_ref[0:1, :], a_ref[1:2, :])

    u = jnp.concatenate(
        [jnp.concatenate([u_ref[k, pl.ds(j, seg, stride=steps), :] for j in range(steps)], axis=0)
         for k in range(u_ref.shape[0])], axis=1)
    ub = u.astype(BF16)
    for k in range(n_blk):
        bu = _dot(ub[:, k * blk_in:(k + 1) * blk_in], bblk_ref[k])
        xr_ref[:, :, k * blk_state:(k + 1) * blk_state] = bu[:, :blk_state].reshape(steps, seg, blk_state)
        xi_ref[:, :, k * blk_state:(k + 1) * blk_state] = bu[:, blk_state:].reshape(steps, seg, blk_state)

    lb = SCAN_LANE_BLOCK
    for b0 in range(0, n_state, lb):
        lanes = slice(b0, b0 + lb)
        a_re = a_ref[0:1, lanes]
        a_im = a_ref[1:2, lanes]
        ar8 = jnp.broadcast_to(a_re, (seg, lb))
        ai8 = jnp.broadcast_to(a_im, (seg, lb))

        def local(j, carry):
            xr, xi = carry
            pr, pi = _cmul(ar8, ai8, xr, xi)
            xr = pr + xr_ref[j, :, lanes]
            xi = pi + xi_ref[j, :, lanes]
            xr_ref[j, :, lanes] = xr
            xi_ref[j, :, lanes] = xi
            return xr, xi

        end_r, end_i = lax.fori_loop(0, steps, local, (jnp.zeros((seg, lb), F32), jnp.zeros((seg, lb), F32)))

        pr, pi = a_re, a_im
        for _ in range(steps.bit_length() - 1):
            pr, pi = _cmul(pr, pi, pr, pi)
        cr, ci = st_ref[0:1, lanes], st_ref[1:2, lanes]
        in_r, in_i = [], []
        for s in range(seg):
            in_r.append(cr)
            in_i.append(ci)
            qr, qi = _cmul(pr, pi, cr, ci)
            cr, ci = qr + end_r[s:s + 1], qi + end_i[s:s + 1]
        st_ref[0:1, lanes] = cr
        st_ref[1:2, lanes] = ci
        in_r = jnp.concatenate(in_r, axis=0)
        in_i = jnp.concatenate(in_i, axis=0)

        def fix(j, carry):
            fr, fi = _cmul(pw_ref[j, 0, :, lanes], pw_ref[j, 1, :, lanes], in_r, in_i)
            xr_ref[j, :, lanes] = xr_ref[j, :, lanes] + fr
            xi_ref[j, :, lanes] = xi_ref[j, :, lanes] + fi
            return carry

        lax.fori_loop(0, steps, fix, 0)

    rows = steps * seg
    y = d_ref[...] * u
    for k in range(n_blk):
        st = slice(k * blk_state, (k + 1) * blk_state)
        xs = jnp.concatenate([xr_ref[:, :, st].reshape(rows, blk_state),
                              xi_ref[:, :, st].reshape(rows, blk_state)], axis=1).astype(BF16)
        yk = _dot(xs, cblk_ref[k])
        if k == 0:
            ys = [yk]
        else:
            ys.append(yk)
    y = y + jnp.concatenate(ys, axis=1)
    y = jax.nn.gelu(y)
    y = y * jax.nn.sigmoid(_dot(y.astype(BF16), wglu_ref[...]))
    y_ref[...] = y.astype(y_ref.dtype)


def _s5(u_slabs, bblk, a_lane, cblk, d_row, w_glu):
    n_slab, seq, _ = u_slabs.shape
    width = n_slab * LANES
    tm = ROW_TILE
    steps = tm // SCAN_SEGMENTS
    n_state = a_lane.shape[1]
    const2 = lambda i: (0, 0)
    const3 = lambda i: (0, 0, 0)
    return pl.pallas_call(
        _s5_kernel,
        grid=(seq // tm,),
        in_specs=[
            pl.BlockSpec((n_slab, tm, LANES), lambda i: (0, i, 0)),
            pl.BlockSpec(bblk.shape, const3),
            pl.BlockSpec(a_lane.shape, const2),
            pl.BlockSpec(cblk.shape, const3),
            pl.BlockSpec((1, width), const2),
            pl.BlockSpec((width, width), const2),
        ],
        out_specs=pl.BlockSpec((tm, width), lambda i: (i, 0)),
        out_shape=jax.ShapeDtypeStruct((seq, width), BF16),
        scratch_shapes=[
            pltpu.VMEM((steps, SCAN_SEGMENTS, n_state), F32),
            pltpu.VMEM((steps, SCAN_SEGMENTS, n_state), F32),
            pltpu.VMEM((2, n_state), F32),
            pltpu.VMEM((steps, 2, SCAN_SEGMENTS, n_state), F32),
        ],
        compiler_params=pltpu.CompilerParams(
            dimension_semantics=("arbitrary",), vmem_limit_bytes=VMEM_LIMIT_BYTES),
        name="s5_scan",
    )(u_slabs, bblk, a_lane, cblk, d_row, w_glu)


def _ffn_kernel(x_ref, ya_ref, yb_ref, gmix_ref, wg_ref, wpa_ref, wpb_ref, wout_ref, gffn_ref,
                wup_ref, cw_ref, cb_ref, wdown_ref, gfin_ref, o_ref, tail_ref, *, d_ff, chunk):
    tm, d_model = x_ref.shape

    @pl.when(pl.program_id(0) == 0)
    def _():
        tail_ref[...] = jnp.zeros_like(tail_ref)

    x = x_ref[...]
    h = _rmsnorm(x, gmix_ref[...]).astype(BF16)
    gates = jax.nn.sigmoid(_dot(h, wg_ref[...]))
    merged = (gates[:, :d_model] * _dot(ya_ref[...], wpa_ref[...])
              + gates[:, d_model:] * _dot(yb_ref[...], wpb_ref[...]))
    x1 = x + _dot(merged.astype(BF16), wout_ref[...])
    h2 = _rmsnorm(x1, gffn_ref[...]).astype(BF16)

    def conv_cols(c0):
        cols = slice(c0, c0 + chunk)
        up = _dot(h2, wup_ref[:, cols])
        ext = jnp.concatenate([tail_ref[:, cols], up], axis=0)
        tail_ref[:, cols] = up[tm - SUBLANES:, :]
        prev1 = pltpu.roll(ext, 1, axis=0)[SUBLANES:]
        prev2 = pltpu.roll(ext, 2, axis=0)[SUBLANES:]
        return cb_ref[:, cols] + cw_ref[0:1, cols] * prev2 + cw_ref[1:2, cols] * prev1 + cw_ref[2:3, cols] * up

    acc = x1
    for k in range(d_ff // chunk):
        gate = conv_cols(k * chunk)
        val = conv_cols(d_ff + k * chunk)
        act = (jax.nn.silu(gate) * val).astype(BF16)
        acc = acc + _dot(act, wdown_ref[k * chunk:(k + 1) * chunk, :])
    o_ref[...] = _rmsnorm(acc, gfin_ref[...])


def _ffn(x2, ya, yb, g_mix, w_g, w_pa, w_pb, w_out, g_ffn, w_up, conv_w, conv_b, w_down, g_fin):
    seq, d_model = x2.shape
    d_ff = w_down.shape[0]
    tm = FFN_ROW_TILE
    chunk = FFN_COL_CHUNK
    assert d_ff % chunk == 0 and chunk % LANES == 0
    row = lambda i: (i, 0)

    def resident(arr):
        return pl.BlockSpec(arr.shape, lambda i: (0, 0), pipeline_mode=pl.Buffered(1))

    return pl.pallas_call(
        functools.partial(_ffn_kernel, d_ff=d_ff, chunk=chunk),
        grid=(seq // tm,),
        in_specs=[
            pl.BlockSpec((tm, d_model), row),
            pl.BlockSpec((tm, ya.shape[1]), row),
            pl.BlockSpec((tm, yb.shape[1]), row),
            resident(g_mix), resident(w_g), resident(w_pa), resident(w_pb), resident(w_out), resident(g_ffn),
            resident(w_up), resident(conv_w), resident(conv_b), resident(w_down), resident(g_fin),
        ],
        out_specs=pl.BlockSpec((tm, d_model), row),
        out_shape=jax.ShapeDtypeStruct((seq, d_model), F32),
        scratch_shapes=[pltpu.VMEM((SUBLANES, 2 * d_ff), F32)],
        compiler_params=pltpu.CompilerParams(
            dimension_semantics=("arbitrary",), vmem_limit_bytes=VMEM_LIMIT_BYTES),
        name="merge_ffn",
    )(x2, ya, yb, g_mix, w_g, w_pa, w_pb, w_out, g_ffn, w_up, conv_w, conv_b, w_down, g_fin)


def _block_diag(m, n_blk):
    g, r, c = m.shape
    per = g // n_blk
    eye = jnp.eye(per, dtype=m.dtype)
    m = m.reshape(n_blk, per, r, c)
    return jnp.einsum("bgrc,gk->bgrkc", m, eye).reshape(n_blk, per * r, per * c)


def kernel(x, positions, norm_mix, w_in, cmp_k_pe, cmp_k_w1, cmp_k_w2, cmp_v_pe, cmp_v_w1, cmp_v_w2, ssm_a_re, ssm_a_im, ssm_b_re, ssm_b_im, ssm_c_re, ssm_c_im, ssm_d, ssm_log_dt, ssm_w_glu, w_proj_a, w_proj_b, w_out, norm_ffn, w_up, conv_w, conv_b, w_down, norm_final):
    bsz, seq, d_model = x.shape
    assert bsz == 1 and norm_mix.shape[0] == 1
    assert seq % max(KV_TILE, ROW_TILE) == 0 and KV_TILE // SLC_LEN <= BF16_ROWS and seq >= WINDOW + Q_TILE
    attn_w = N_HEADS * HEAD_DIM
    kv_w = N_KV_HEADS * HEAD_DIM
    ssm_width = ssm_d.shape[1]
    x2 = x[0]
    pos = positions[0]

    sizes = (attn_w,) + (kv_w,) * 6 + (3 * N_HEADS, ssm_width, d_model, d_model)
    offs = [0]
    for s in sizes:
        offs.append(offs[-1] + s)
    w = w_in[0]
    col = lambda i: w[:, offs[i]:offs[i + 1]]
    w_q, w_kc, w_vc, w_ks, w_vs, w_kw, w_vw, w_ng, w_u, w_ga, w_gb = [col(i) for i in range(11)]
    w_ng = w_ng.reshape(d_model, N_KV_HEADS, Q_PER_KV, 3).transpose(0, 1, 3, 2).reshape(d_model, N_KV_HEADS, 3 * Q_PER_KV)
    w_ng = jnp.pad(w_ng, ((0, 0), (0, 0), (0, GATE_ROWS - 3 * Q_PER_KV))).reshape(d_model, N_KV_HEADS * GATE_ROWS)
    w_t = jnp.concatenate([w_q, w_ks, w_kw, w_vs, w_vw, w_ng], axis=1).T.astype(BF16)
    w_nat = jnp.concatenate([w_kc, w_vc, w_u], axis=1).astype(BF16)
    w_gate = jnp.concatenate([w_ga, w_gb], axis=1).astype(BF16)

    inv_freq = ROPE_THETA ** (-jnp.arange(0, ROT_DIM, 2, dtype=F32) / ROT_DIM)
    qt, ks, kw, vst, vwt, ngt, kvc, u = _inproj(x2, norm_mix, pos[None, :], inv_freq[:, None], w_nat, w_t)

    n_half = seq // CMP_STRIDE
    n_cmp = (seq - CMP_LEN) // CMP_STRIDE + 1
    h4 = kvc.reshape(seq, 2 * N_KV_HEADS, HEAD_DIM).transpose(1, 0, 2).reshape(
        2 * N_KV_HEADS, n_half, CMP_STRIDE * HEAD_DIM)
    halves = lambda pe: pe.reshape(2, CMP_STRIDE * HEAD_DIM)
    pe4 = jnp.stack([halves(cmp_k_pe[0])] * N_KV_HEADS + [halves(cmp_v_pe[0])] * N_KV_HEADS)
    w1 = jnp.stack([cmp_k_w1[0], cmp_v_w1[0]]).astype(BF16)
    w2 = jnp.pad(jnp.stack([cmp_k_w2[0], cmp_v_w2[0]]), ((0, 0), (0, 0), (0, LANES - HEAD_DIM))).astype(BF16)
    cmp_end = jnp.arange(n_cmp, dtype=jnp.int32) * CMP_STRIDE + CMP_LEN - 1
    cmp_pos = jnp.pad(pos[cmp_end], (0, n_half - n_cmp))
    pat = _rope_lane_patterns()
    pat2 = jnp.stack([pat, jnp.zeros_like(pat)])
    cmp = _compress(h4, pe4, w1, w2, cmp_pos[:, None], pat2)[:, :, :HEAD_DIM]
    n_slc = seq // SLC_LEN
    cmp = cmp.reshape(2 * N_KV_HEADS, n_slc, CMP_PER_SLC, HEAD_DIM).transpose(0, 2, 1, 3).reshape(
        2 * N_KV_HEADS, n_half, HEAD_DIM).astype(BF16)
    kc = cmp[:N_KV_HEADS]
    vct = cmp[N_KV_HEADS:].transpose(0, 2, 1)

    y_a = _attention(qt, kc, vct, ks, vst, kw, vwt, ngt, seq)

    ab_re, ab_im, bb_re, bb_im = _discretize(ssm_a_re[0], ssm_a_im[0], ssm_log_dt[0], ssm_b_re[0], ssm_b_im[0])
    n_blk = 2
    a_lane = jnp.stack([ab_re.reshape(-1), ab_im.reshape(-1)])
    to_in = lambda b: _block_diag(b.transpose(0, 2, 1), n_blk)
    bblk = jnp.concatenate([to_in(bb_re), to_in(bb_im)], axis=2).astype(BF16)
    to_out = lambda cm: _block_diag(cm.transpose(0, 2, 1), n_blk)
    cblk = jnp.concatenate([to_out(ssm_c_re[0]), -to_out(ssm_c_im[0])], axis=1).astype(BF16)
    steps = ROW_TILE // SCAN_SEGMENTS
    nt = seq // ROW_TILE
    y_b = _s5(u, bblk, a_lane, cblk, ssm_d, ssm_w_glu[0].astype(BF16))
    y_b = y_b.reshape(nt, steps, SCAN_SEGMENTS, ssm_width).transpose(0, 2, 1, 3).reshape(seq, ssm_width)

    out = _ffn(x2, y_a, y_b, norm_mix, w_gate, w_proj_a[0].astype(BF16), w_proj_b[0].astype(BF16),
               w_out[0].astype(BF16), norm_ffn, w_up[0].astype(BF16), conv_w[0], conv_b, w_down[0].astype(BF16),
               norm_final[None, :])
    return out[None]
```

```python
import functools

import jax
import jax.numpy as jnp
from jax import lax
from jax.experimental import pallas as pl
from jax.experimental.pallas import tpu as pltpu

F32 = jnp.float32
BF16 = jnp.bfloat16

N_HEADS = 8
N_KV_HEADS = 2
HEAD_DIM = 64
Q_PER_KV = N_HEADS // N_KV_HEADS
ROT_DIM = HEAD_DIM // 4
ROPE_THETA = 500000.0
CMP_LEN = 32
CMP_STRIDE = 16
CMP_PER_SLC = 4
SLC_LEN = 64
SLC_TOPK = 16
WINDOW = 512
SSM_GROUP = 16
SSM_STATE = 64
CONV_WIDTH = 3
EPS = 1e-6
NEG_INF = -1e30
FORCE_SCORE = 1e4
LOG2E = 1.4426950408889634

LANES = 128
SUBLANES = 8
BF16_ROWS = 16
VMEM_BYTES_V7X = 64 * 1024 * 1024
VMEM_LIMIT_BYTES = VMEM_BYTES_V7X * 7 // 8

ROW_TILE = 512
FFN_ROW_TILE = 256
FFN_COL_CHUNK = 1408
Q_TILE = 256
KV_TILE = 512
SCORE_SLOTS = 2
FRONT_CLASSES = 16
SCAN_SEGMENTS = SUBLANES
SCAN_LANE_BLOCK = 1024


def _rmsnorm(x, g):
    return x * lax.rsqrt(jnp.mean(x * x, axis=-1, keepdims=True) + EPS) * g


def _dot(a, b):
    return jnp.dot(a, b, preferred_element_type=F32)


def _dot_nt(a, b):
    return lax.dot_general(a, b, (((1,), (1,)), ((), ())), preferred_element_type=F32)


def _rope_lane_patterns():
    half = ROT_DIM // 2
    d = jnp.arange(LANES) % HEAD_DIM
    inv_freq = ROPE_THETA ** (-jnp.arange(0, ROT_DIM, 2, dtype=F32) / ROT_DIM)
    freq = jnp.where(d < ROT_DIM, inv_freq[d % half], 0.0)
    m_lo = jnp.where(d < half, -1.0, 0.0)
    m_hi = jnp.where((d >= half) & (d < ROT_DIM), 1.0, 0.0)
    return jnp.stack([freq, m_lo, m_hi]).astype(F32)


def _rope_tables(pos_col, pat):
    ang = pos_col.astype(F32) * pat[0:1]
    sin = jnp.sin(ang)
    return jnp.cos(ang), sin * pat[1:2], sin * pat[2:3]


def _rope(x, cos, sin_lo, sin_hi):
    half = ROT_DIM // 2
    return (x * cos + pltpu.roll(x, LANES - half, axis=1) * sin_lo
            + pltpu.roll(x, half, axis=1) * sin_hi)


KVC_COLS = 2 * N_KV_HEADS * HEAD_DIM
V_ROWS = HEAD_DIM + BF16_ROWS
GATE_ROWS = 16


def _rope_t(xh, cos, sin):
    half = ROT_DIM // 2
    x1, x2 = xh[0:half], xh[half:ROT_DIM]
    return jnp.concatenate([x1 * cos - x2 * sin, x2 * cos + x1 * sin, xh[ROT_DIM:]], axis=0)


def _inproj_kernel(x_ref, g_ref, pos_ref, invf_ref, wn_ref, wt_ref,
                   qt_ref, ks_ref, kw_ref, vst_ref, vwt_ref, ngt_ref, kvc_ref, u_ref):
    tm = x_ref.shape[0]
    h = _rmsnorm(x_ref[...], g_ref[...]).astype(BF16)
    pn = _dot(h, wn_ref[...])
    kvc_ref[...] = pn[:, :KVC_COLS]
    for k in range(u_ref.shape[0]):
        u_ref[k] = pn[:, KVC_COLS + k * LANES:KVC_COLS + (k + 1) * LANES]

    pt = _dot_nt(wt_ref[...], h)
    ang = invf_ref[...] * pos_ref[...].astype(F32)
    cos, sin = jnp.cos(ang), jnp.sin(ang)
    q_scale = (HEAD_DIM ** -0.5) * LOG2E
    for hd in range(N_HEADS):
        rows = slice(hd * HEAD_DIM, (hd + 1) * HEAD_DIM)
        qt_ref[rows, :] = (_rope_t(pt[rows], cos, sin) * q_scale).astype(BF16)

    kv_w = N_KV_HEADS * HEAD_DIM
    r_ks = N_HEADS * HEAD_DIM
    r_kw, r_vs, r_vw, r_ng = r_ks + kv_w, r_ks + 2 * kv_w, r_ks + 3 * kv_w, r_ks + 4 * kv_w
    key = pl.program_id(0) * tm + lax.broadcasted_iota(jnp.int32, (BF16_ROWS, tm), 1)
    blk = lax.broadcasted_iota(jnp.int32, (BF16_ROWS, tm), 0)
    onehot = jnp.where((key % KV_TILE) // SLC_LEN == blk, 1.0, 0.0)
    ones_rows = jnp.where(blk == 0, 1.0, 0.0)
    pad_k = jnp.zeros((LANES - HEAD_DIM - BF16_ROWS, tm), F32)
    pad_w = jnp.zeros((LANES - HEAD_DIM, tm), F32)
    for kh in range(N_KV_HEADS):
        off = kh * HEAD_DIM
        k_s = _rope_t(pt[r_ks + off:r_ks + off + HEAD_DIM], cos, sin)
        ks_ref[kh] = jnp.concatenate([k_s, onehot, pad_k], axis=0).T.astype(BF16)
        k_w = _rope_t(pt[r_kw + off:r_kw + off + HEAD_DIM], cos, sin)
        kw_ref[kh] = jnp.concatenate([k_w, pad_w], axis=0).T.astype(BF16)
        vst_ref[kh] = jnp.concatenate([pt[r_vs + off:r_vs + off + HEAD_DIM], ones_rows], axis=0).astype(BF16)
        vwt_ref[kh] = jnp.concatenate([pt[r_vw + off:r_vw + off + HEAD_DIM], ones_rows], axis=0).astype(BF16)
    ngt_ref[...] = pt[r_ng:r_ng + N_KV_HEADS * GATE_ROWS]


def _inproj(x2, g, pos_row, inv_freq, w_nat, w_t):
    seq, d_model = x2.shape
    tm = ROW_TILE
    n_nat = w_nat.shape[1]
    n_t = w_t.shape[0]
    attn_w = N_HEADS * HEAD_DIM
    row = lambda i: (i, 0)
    col = lambda i: (0, i)
    const = lambda i: (0, 0)
    head_rows = lambda i: (0, i, 0)
    head_cols = lambda i: (0, 0, i)
    return pl.pallas_call(
        _inproj_kernel,
        grid=(seq // tm,),
        in_specs=[
            pl.BlockSpec((tm, d_model), row),
            pl.BlockSpec((1, d_model), const),
            pl.BlockSpec((1, tm), col),
            pl.BlockSpec((ROT_DIM // 2, 1), const),
            pl.BlockSpec((d_model, n_nat), const),
            pl.BlockSpec((n_t, d_model), const),
        ],
        out_specs=[
            pl.BlockSpec((attn_w, tm), col),
            pl.BlockSpec((N_KV_HEADS, tm, LANES), head_rows),
            pl.BlockSpec((N_KV_HEADS, tm, LANES), head_rows),
            pl.BlockSpec((N_KV_HEADS, V_ROWS, tm), head_cols),
            pl.BlockSpec((N_KV_HEADS, V_ROWS, tm), head_cols),
            pl.BlockSpec((N_KV_HEADS * GATE_ROWS, tm), col),
            pl.BlockSpec((tm, KVC_COLS), row),
            pl.BlockSpec(((n_nat - KVC_COLS) // LANES, tm, LANES), head_rows),
        ],
        out_shape=[
            jax.ShapeDtypeStruct((attn_w, seq), BF16),
            jax.ShapeDtypeStruct((N_KV_HEADS, seq, LANES), BF16),
            jax.ShapeDtypeStruct((N_KV_HEADS, seq, LANES), BF16),
            jax.ShapeDtypeStruct((N_KV_HEADS, V_ROWS, seq), BF16),
            jax.ShapeDtypeStruct((N_KV_HEADS, V_ROWS, seq), BF16),
            jax.ShapeDtypeStruct((N_KV_HEADS * GATE_ROWS, seq), F32),
            jax.ShapeDtypeStruct((seq, KVC_COLS), F32),
            jax.ShapeDtypeStruct(((n_nat - KVC_COLS) // LANES, seq, LANES), F32),
        ],
        compiler_params=pltpu.CompilerParams(
            dimension_semantics=("arbitrary",), vmem_limit_bytes=VMEM_LIMIT_BYTES),
        name="inproj",
    )(x2, g, pos_row, inv_freq, w_nat, w_t)


def _compress_kernel(h_ref, pe_ref, w1_ref, w2_ref, pos_ref, pat_ref, out_ref):
    hh = h_ref[0]
    rows, half_w = hh.shape
    a = _dot((hh + pe_ref[0, 0:1]).astype(BF16), w1_ref[0, :half_w])
    b = _dot((hh + pe_ref[0, 1:2]).astype(BF16), w1_ref[0, half_w:])
    pre = a + pltpu.roll(b, rows - 1, axis=0)
    z = _dot(jax.nn.gelu(pre).astype(BF16), w2_ref[0])
    cos, sin_lo, sin_hi = _rope_tables(pos_ref[...], pat_ref[0])
    out_ref[0] = _rope(z, cos, sin_lo, sin_hi)


def _compress(h4, pe4, w1, w2, cmp_pos_col, pat2):
    n, rows, half_w = h4.shape
    hidden = w1.shape[2]
    per = n // 2
    return pl.pallas_call(
        _compress_kernel,
        grid=(n,),
        in_specs=[
            pl.BlockSpec((1, rows, half_w), lambda i: (i, 0, 0)),
            pl.BlockSpec((1, 2, half_w), lambda i: (i, 0, 0)),
            pl.BlockSpec((1, 2 * half_w, hidden), lambda i: (i // per, 0, 0)),
            pl.BlockSpec((1, hidden, LANES), lambda i: (i // per, 0, 0)),
            pl.BlockSpec((rows, 1), lambda i: (0, 0)),
            pl.BlockSpec((1, 3, LANES), lambda i: (i // per, 0, 0)),
        ],
        out_specs=pl.BlockSpec((1, rows, LANES), lambda i: (i, 0, 0)),
        out_shape=jax.ShapeDtypeStruct((n, rows, LANES), F32),
        compiler_params=pltpu.CompilerParams(
            dimension_semantics=("arbitrary",), vmem_limit_bytes=VMEM_LIMIT_BYTES),
        name="compress",
    )(h4, pe4, w1, w2, cmp_pos_col, pat2)


def _attn_kernel(qt_ref, kc_ref, vct_ref, ks_ref, vst_ref, kw_ref, vwt_ref, ng_ref, wb_ref,
                 out_ref, bias_ref, s_ref, acc_ref, m_ref, oc_ref, ow_ref, *, n_slc, topk):
    c = pl.program_id(1)
    qw = Q_PER_KV * Q_TILE
    t0 = c * Q_TILE
    t_lane = t0 + (lax.broadcasted_iota(jnp.int32, (1, qw), 1) & (Q_TILE - 1))
    qt4 = qt_ref[...]
    qt = jnp.concatenate([qt4[g * HEAD_DIM:(g + 1) * HEAD_DIM] for g in range(Q_PER_KV)], axis=1)

    def lanes4(x):
        return jnp.concatenate([x] * Q_PER_KV, axis=1)

    blocks_per_tile = KV_TILE // SLC_LEN
    q_lane = (lax.broadcasted_iota(jnp.int32, (1, qw), 1) & (Q_TILE - 1)).astype(F32)
    tq = t0 + lax.broadcasted_iota(jnp.int32, (1, Q_TILE), 1)

    def front(nj):
        rows = CMP_PER_SLC * nj
        kc = jnp.concatenate([kc_ref[0, r * n_slc:r * n_slc + nj, :] for r in range(CMP_PER_SLC)], axis=0)
        s_c = _dot(kc, qt)
        row = lax.broadcasted_iota(jnp.int32, (rows, 1), 0)
        n_idx = CMP_PER_SLC * (row % nj) + row // nj
        first_q = (n_idx * CMP_STRIDE + (CMP_LEN - 1) - t0).astype(F32)
        s_c = jnp.where(first_q <= q_lane, s_c, NEG_INF)
        m_c = jnp.max(s_c, axis=0, keepdims=True)
        e_c = jnp.exp2(s_c - m_c)
        inv_c = jnp.where(m_c > 0.5 * NEG_INF, 1.0 / jnp.maximum(jnp.sum(e_c, axis=0, keepdims=True), 1e-30), 0.0)
        e_b = e_c.astype(BF16)
        if nj < n_slc:
            gap = jnp.zeros((n_slc - nj, qw), BF16)
            e_b = jnp.concatenate(
                [piece for r in range(CMP_PER_SLC) for piece in (e_b[r * nj:(r + 1) * nj], gap)], axis=0)
        oc_ref[...] = _dot(vct_ref[0], e_b) * inv_c

        p_c = e_c * inv_c
        psum = p_c[:, 0:Q_TILE]
        for g in range(1, Q_PER_KV):
            psum = psum + p_c[:, g * Q_TILE:(g + 1) * Q_TILE]
        j_idx = lax.broadcasted_iota(jnp.int32, (nj, 1), 0)
        straddle = psum[3 * nj:4 * nj]
        from_prev = jnp.where(j_idx == 0, 0.0, pltpu.roll(straddle, 1, axis=0))
        imp = (psum[0:nj] + psum[nj:2 * nj] + psum[2 * nj:3 * nj]
               + 0.5 * straddle + 0.5 * from_prev)

        span = WINDOW + Q_TILE
        w0 = pl.multiple_of(jnp.maximum(t0 - WINDOW, 0), Q_TILE)
        rhs_w = jnp.concatenate([qt, jnp.zeros((LANES - HEAD_DIM, qw), BF16)], axis=0)
        wb_off = pl.multiple_of(WINDOW - (t0 - w0), Q_TILE)
        s_w = _dot(kw_ref[0, pl.ds(w0, span), :], rhs_w) + lanes4(wb_ref[pl.ds(wb_off, span), :])
        e_w = jnp.exp2(s_w - jnp.max(s_w, axis=0, keepdims=True)).astype(BF16)
        acc_w = _dot(vwt_ref[0, :, pl.ds(w0, span)], e_w)
        ow_ref[...] = acc_w[:HEAD_DIM] * (1.0 / jnp.maximum(acc_w[HEAD_DIM:HEAD_DIM + 1], 1e-30))

        forced = (j_idx == 0) | (j_idx == tq // SLC_LEN)
        valid = (j_idx * SLC_LEN) <= tq
        score = jnp.where(valid, jnp.where(forced, FORCE_SCORE, imp), NEG_INF)
        j_f = j_idx.astype(F32)
        for _ in range(min(topk, nj)):
            best = jnp.max(score, axis=0, keepdims=True)
            first = jnp.min(jnp.where(score == best, j_f, float(nj)), axis=0, keepdims=True)
            score = jnp.where(j_f == first, -jnp.inf, score)
        tiles = nj // blocks_per_tile
        bias = jnp.where(valid & (score == -jnp.inf), 0.0, NEG_INF)
        bias_ref[0:tiles, 0:blocks_per_tile, :] = bias.reshape(tiles, blocks_per_tile, Q_TILE)
        if blocks_per_tile < BF16_ROWS:
            bias_ref[0:tiles, blocks_per_tile:, :] = jnp.zeros((tiles, BF16_ROWS - blocks_per_tile, Q_TILE), F32)

    j_vis = (t0 + Q_TILE) // SLC_LEN
    step = n_slc // FRONT_CLASSES
    for k in range(FRONT_CLASSES):
        pl.when((j_vis > k * step) & (j_vis <= (k + 1) * step))(functools.partial(front, (k + 1) * step))

    zeros_tail = jnp.zeros((LANES - HEAD_DIM - BF16_ROWS, qw), BF16)

    def scores(slot, j):
        k0 = pl.multiple_of(j * KV_TILE, KV_TILE)
        rhs = jnp.concatenate([qt, lanes4(bias_ref[j]).astype(BF16), zeros_tail], axis=0)
        s_ref[slot] = _dot(ks_ref[0, pl.ds(k0, KV_TILE), :], rhs)

    def accumulate(s, j):
        k0 = pl.multiple_of(j * KV_TILE, KV_TILE)
        m_old = m_ref[...]
        m_new = jnp.maximum(m_old, jnp.max(s, axis=0, keepdims=True))
        p = jnp.exp2(s - m_new).astype(BF16)
        acc_ref[...] = jnp.exp2(m_old - m_new) * acc_ref[...] + _dot(vst_ref[0, :, pl.ds(k0, KV_TILE)], p)
        m_ref[...] = m_new

    m_ref[...] = jnp.full((1, qw), NEG_INF, F32)
    acc_ref[...] = jnp.zeros((V_ROWS, qw), F32)
    n_full = t0 // KV_TILE
    scores(0, 0)

    def pair(j):
        scores(1, j + 1)
        accumulate(s_ref[0], j)
        scores(0, j + 2)
        accumulate(s_ref[1], j + 1)

    def two_pairs(i, _):
        pair(4 * i)
        pair(4 * i + 2)
        return 0

    n_pairs = n_full // 2
    lax.fori_loop(0, n_pairs // 2, two_pairs, 0)

    @pl.when(n_pairs % 2 == 1)
    def _():
        pair(2 * (n_pairs - 1))

    odd = n_full % 2
    tok = n_full * KV_TILE + lax.broadcasted_iota(jnp.int32, (KV_TILE, 1), 0)

    def last_tile(slot):
        accumulate(jnp.where(tok <= t_lane, s_ref[slot], NEG_INF), n_full)

    @pl.when(odd == 1)
    def _():
        scores(1, n_full)
        accumulate(s_ref[0], n_full - 1)
        last_tile(1)

    @pl.when(odd == 0)
    def _():
        last_tile(0)

    acc_s = acc_ref[...]
    o_s = acc_s[:HEAD_DIM] * (1.0 / jnp.maximum(acc_s[HEAD_DIM:HEAD_DIM + 1], 1e-30))

    gates = jax.nn.sigmoid(ng_ref[...])

    def gate(b):
        return jnp.concatenate([gates[b * Q_PER_KV + g:b * Q_PER_KV + g + 1] for g in range(Q_PER_KV)], axis=1)

    o = gate(0) * oc_ref[...] + gate(1) * o_s + gate(2) * ow_ref[...]
    pairs = []
    for g in range(0, Q_PER_KV, 2):
        two = jnp.concatenate([o[:, g * Q_TILE:(g + 1) * Q_TILE], o[:, (g + 1) * Q_TILE:(g + 2) * Q_TILE]], axis=0)
        pairs.append(two.T)
    out_ref[...] = jnp.concatenate(pairs, axis=1).astype(out_ref.dtype)


def _window_bias_table():
    r = jnp.arange(2 * WINDOW + Q_TILE, dtype=jnp.int32)[:, None]
    q = jnp.arange(Q_TILE, dtype=jnp.int32)[None, :]
    return jnp.where((r > q) & (r <= WINDOW + q), 0.0, NEG_INF).astype(F32)


def _attention(qt, kc, vct, ks, vst, kw, vwt, ngt, seq):
    wb = _window_bias_table()
    n_slc = seq // SLC_LEN
    nb = seq // Q_TILE
    qw = Q_PER_KV * Q_TILE
    n_rows = CMP_PER_SLC * n_slc
    topk = min(SLC_TOPK, n_slc)
    assert (n_slc // FRONT_CLASSES) % (KV_TILE // SLC_LEN) == 0

    def per_head(block):
        return pl.BlockSpec(block, lambda k, c: (k, 0, 0), pipeline_mode=pl.Buffered(1))

    return pl.pallas_call(
        functools.partial(_attn_kernel, n_slc=n_slc, topk=topk),
        grid=(N_KV_HEADS, nb),
        in_specs=[
            pl.BlockSpec((Q_PER_KV * HEAD_DIM, Q_TILE), lambda k, c: (k, c)),
            per_head((1, n_rows, HEAD_DIM)),
            per_head((1, HEAD_DIM, n_rows)),
            per_head((1, seq, LANES)),
            per_head((1, V_ROWS, seq)),
            per_head((1, seq, LANES)),
            per_head((1, V_ROWS, seq)),
            pl.BlockSpec((GATE_ROWS, Q_TILE), lambda k, c: (k, c)),
            pl.BlockSpec(wb.shape, lambda k, c: (0, 0), pipeline_mode=pl.Buffered(1)),
        ],
        out_specs=pl.BlockSpec((Q_TILE, Q_PER_KV * HEAD_DIM), lambda k, c: (c, k)),
        out_shape=jax.ShapeDtypeStruct((seq, N_HEADS * HEAD_DIM), BF16),
        scratch_shapes=[
            pltpu.VMEM((n_slc * SLC_LEN // KV_TILE, BF16_ROWS, Q_TILE), F32),
            pltpu.VMEM((SCORE_SLOTS, KV_TILE, qw), F32),
            pltpu.VMEM((V_ROWS, qw), F32),
            pltpu.VMEM((1, qw), F32),
            pltpu.VMEM((HEAD_DIM, qw), F32),
            pltpu.VMEM((HEAD_DIM, qw), F32),
        ],
        compiler_params=pltpu.CompilerParams(
            dimension_semantics=("arbitrary", "arbitrary"), vmem_limit_bytes=VMEM_LIMIT_BYTES),
        name="nsa_attention",
    )(qt, kc, vct, ks, vst, kw, vwt, ngt, wb)


def _discretize_kernel(are_ref, aim_ref, ldt_ref, bre_ref, bim_ref, ar_ref, ai_ref, br_ref, bi_ref):
    a_re, a_im = are_ref[...], aim_ref[...]
    dt = jnp.exp(ldt_ref[...])
    mag = jnp.exp(a_re * dt)
    ang = a_im * dt
    ab_re, ab_im = mag * jnp.cos(ang), mag * jnp.sin(ang)
    n_re, n_im = ab_re - 1.0, ab_im
    den = a_re * a_re + a_im * a_im
    c_re = (n_re * a_re + n_im * a_im) / den
    c_im = (n_im * a_re - n_re * a_im) / den
    b_re, b_im = bre_ref[...], bim_ref[...]
    ar_ref[...] = ab_re
    ai_ref[...] = ab_im
    br_ref[...] = c_re * b_re - c_im * b_im
    bi_ref[...] = c_re * b_im + c_im * b_re


def _discretize(a_re, a_im, log_dt, b_re, b_im):
    g, p, h = b_re.shape
    col = jax.ShapeDtypeStruct((g, p, 1), F32)
    full = jax.ShapeDtypeStruct((g, p, h), F32)
    return pl.pallas_call(
        _discretize_kernel, out_shape=[col, col, full, full], name="ssm_discretize",
    )(a_re[..., None], a_im[..., None], log_dt[:, None, None], b_re, b_im)


def _cmul(ar, ai, br, bi):
    return ar * br - ai * bi, ar * bi + ai * br


def _s5_kernel(u_ref, bblk_ref, a_ref, cblk_ref, d_ref, wglu_ref, y_ref, xr_ref, xi_ref, st_ref, pw_ref):
    seg = SCAN_SEGMENTS
    steps = xr_ref.shape[0]
    n_state = xr_ref.shape[2]
    n_blk, blk_in, blk_state2 = bblk_ref.shape
    blk_state = blk_state2 // 2

    @pl.when(pl.program_id(0) == 0)
    def _():
        st_ref[...] = jnp.zeros_like(st_ref)
        wr, wi = a_ref[0:1, :], a_ref[1:2, :]
        for j in range(steps):
            pw_ref[j, 0] = jnp.broadcast_to(wr, (seg, n_state))
            pw_ref[j, 1] = jnp.broadcast_to(wi, (seg, n_state))
            wr, wi = _cmul(wr, wi, a_ref[0:1, :], a_ref[1:2, :])

    u = jnp.concatenate(
        [jnp.concatenate([u_ref[k, pl.ds(j, seg, stride=steps), :] for j in range(steps)], axis=0)
         for k in range(u_ref.shape[0])], axis=1)
    ub = u.astype(BF16)
    for k in range(n_blk):
        bu = _dot(ub[:, k * blk_in:(k + 1) * blk_in], bblk_ref[k])
        xr_ref[:, :, k * blk_state:(k + 1) * blk_state] = bu[:, :blk_state].reshape(steps, seg, blk_state)
        xi_ref[:, :, k * blk_state:(k + 1) * blk_state] = bu[:, blk_state:].reshape(steps, seg, blk_state)

    lb = SCAN_LANE_BLOCK
    for b0 in range(0, n_state, lb):
        lanes = slice(b0, b0 + lb)
        a_re = a_ref[0:1, lanes]
        a_im = a_ref[1:2, lanes]
        ar8 = jnp.broadcast_to(a_re, (seg, lb))
        ai8 = jnp.broadcast_to(a_im, (seg, lb))

        def local(j, carry):
            xr, xi = carry
            pr, pi = _cmul(ar8, ai8, xr, xi)
            xr = pr + xr_ref[j, :, lanes]
            xi = pi + xi_ref[j, :, lanes]
            xr_ref[j, :, lanes] = xr
            xi_ref[j, :, lanes] = xi
            return xr, xi

        end_r, end_i = lax.fori_loop(0, steps, local, (jnp.zeros((seg, lb), F32), jnp.zeros((seg, lb), F32)))

        pr, pi = a_re, a_im
        for _ in range(steps.bit_length() - 1):
            pr, pi = _cmul(pr, pi, pr, pi)
        cr, ci = st_ref[0:1, lanes], st_ref[1:2, lanes]
        in_r, in_i = [], []
        for s in range(seg):
            in_r.append(cr)
            in_i.append(ci)
            qr, qi = _cmul(pr, pi, cr, ci)
            cr, ci = qr + end_r[s:s + 1], qi + end_i[s:s + 1]
        st_ref[0:1, lanes] = cr
        st_ref[1:2, lanes] = ci
        in_r = jnp.concatenate(in_r, axis=0)
        in_i = jnp.concatenate(in_i, axis=0)

        def fix(j, carry):
            fr, fi = _cmul(pw_ref[j, 0, :, lanes], pw_ref[j, 1, :, lanes], in_r, in_i)
            xr_ref[j, :, lanes] = xr_ref[j, :, lanes] + fr
            xi_ref[j, :, lanes] = xi_ref[j, :, lanes] + fi
            return carry

        lax.fori_loop(0, steps, fix, 0)

    rows = steps * seg
    y = d_ref[...] * u
    for k in range(n_blk):
        st = slice(k * blk_state, (k + 1) * blk_state)
        xs = jnp.concatenate([xr_ref[:, :, st].reshape(rows, blk_state),
                              xi_ref[:, :, st].reshape(rows, blk_state)], axis=1).astype(BF16)
        yk = _dot(xs, cblk_ref[k])
        if k == 0:
            ys = [yk]
        else:
            ys.append(yk)
    y = y + jnp.concatenate(ys, axis=1)
    y = jax.nn.gelu(y)
    y = y * jax.nn.sigmoid(_dot(y.astype(BF16), wglu_ref[...]))
    y_ref[...] = y.astype(y_ref.dtype)


def _s5(u_slabs, bblk, a_lane, cblk, d_row, w_glu):
    n_slab, seq, _ = u_slabs.shape
    width = n_slab * LANES
    tm = ROW_TILE
    steps = tm // SCAN_SEGMENTS
    n_state = a_lane.shape[1]
    const2 = lambda i: (0, 0)
    const3 = lambda i: (0, 0, 0)
    return pl.pallas_call(
        _s5_kernel,
        grid=(seq // tm,),
        in_specs=[
            pl.BlockSpec((n_slab, tm, LANES), lambda i: (0, i, 0)),
            pl.BlockSpec(bblk.shape, const3),
            pl.BlockSpec(a_lane.shape, const2),
            pl.BlockSpec(cblk.shape, const3),
            pl.BlockSpec((1, width), const2),
            pl.BlockSpec((width, width), const2),
        ],
        out_specs=pl.BlockSpec((tm, width), lambda i: (i, 0)),
        out_shape=jax.ShapeDtypeStruct((seq, width), BF16),
        scratch_shapes=[
            pltpu.VMEM((steps, SCAN_SEGMENTS, n_state), F32),
            pltpu.VMEM((steps, SCAN_SEGMENTS, n_state), F32),
            pltpu.VMEM((2, n_state), F32),
            pltpu.VMEM((steps, 2, SCAN_SEGMENTS, n_state), F32),
        ],
        compiler_params=pltpu.CompilerParams(
            dimension_semantics=("arbitrary",), vmem_limit_bytes=VMEM_LIMIT_BYTES),
        name="s5_scan",
    )(u_slabs, bblk, a_lane, cblk, d_row, w_glu)


def _ffn_kernel(x_ref, ya_ref, yb_ref, gmix_ref, wg_ref, wpa_ref, wpb_ref, wout_ref, gffn_ref,
                wup_ref, cw_ref, cb_ref, wdown_ref, gfin_ref, o_ref, tail_ref, *, d_ff, chunk):
    tm, d_model = x_ref.shape

    @pl.when(pl.program_id(0) == 0)
    def _():
        tail_ref[...] = jnp.zeros_like(tail_ref)

    x = x_ref[...]
    h = _rmsnorm(x, gmix_ref[...]).astype(BF16)
    gates = jax.nn.sigmoid(_dot(h, wg_ref[...]))
    merged = (gates[:, :d_model] * _dot(ya_ref[...], wpa_ref[...])
              + gates[:, d_model:] * _dot(yb_ref[...], wpb_ref[...]))
    x1 = x + _dot(merged.astype(BF16), wout_ref[...])
    h2 = _rmsnorm(x1, gffn_ref[...]).astype(BF16)

    def conv_cols(c0):
        cols = slice(c0, c0 + chunk)
        up = _dot(h2, wup_ref[:, cols])
        ext = jnp.concatenate([tail_ref[:, cols], up], axis=0)
        tail_ref[:, cols] = up[tm - SUBLANES:, :]
        prev1 = pltpu.roll(ext, 1, axis=0)[SUBLANES:]
        prev2 = pltpu.roll(ext, 2, axis=0)[SUBLANES:]
        return cb_ref[:, cols] + cw_ref[0:1, cols] * prev2 + cw_ref[1:2, cols] * prev1 + cw_ref[2:3, cols] * up

    acc = x1
    for k in range(d_ff // chunk):
        gate = conv_cols(k * chunk)
        val = conv_cols(d_ff + k * chunk)
        act = (jax.nn.silu(gate) * val).astype(BF16)
        acc = acc + _dot(act, wdown_ref[k * chunk:(k + 1) * chunk, :])
    o_ref[...] = _rmsnorm(acc, gfin_ref[...])


def _ffn(x2, ya, yb, g_mix, w_g, w_pa, w_pb, w_out, g_ffn, w_up, conv_w, conv_b, w_down, g_fin):
    seq, d_model = x2.shape
    d_ff = w_down.shape[0]
    tm = FFN_ROW_TILE
    chunk = FFN_COL_CHUNK
    assert d_ff % chunk == 0 and chunk % LANES == 0
    row = lambda i: (i, 0)

    def resident(arr):
        return pl.BlockSpec(arr.shape, lambda i: (0, 0), pipeline_mode=pl.Buffered(1))

    return pl.pallas_call(
        functools.partial(_ffn_kernel, d_ff=d_ff, chunk=chunk),
        grid=(seq // tm,),
        in_specs=[
            pl.BlockSpec((tm, d_model), row),
            pl.BlockSpec((tm, ya.shape[1]), row),
            pl.BlockSpec((tm, yb.shape[1]), row),
            resident(g_mix), resident(w_g), resident(w_pa), resident(w_pb), resident(w_out), resident(g_ffn),
            resident(w_up), resident(conv_w), resident(conv_b), resident(w_down), resident(g_fin),
        ],
        out_specs=pl.BlockSpec((tm, d_model), row),
        out_shape=jax.ShapeDtypeStruct((seq, d_model), F32),
        scratch_shapes=[pltpu.VMEM((SUBLANES, 2 * d_ff), F32)],
        compiler_params=pltpu.CompilerParams(
            dimension_semantics=("arbitrary",), vmem_limit_bytes=VMEM_LIMIT_BYTES),
        name="merge_ffn",
    )(x2, ya, yb, g_mix, w_g, w_pa, w_pb, w_out, g_ffn, w_up, conv_w, conv_b, w_down, g_fin)


def _block_diag(m, n_blk):
    g, r, c = m.shape
    per = g // n_blk
    eye = jnp.eye(per, dtype=m.dtype)
    m = m.reshape(n_blk, per, r, c)
    return jnp.einsum("bgrc,gk->bgrkc", m, eye).reshape(n_blk, per * r, per * c)


def kernel(x, positions, norm_mix, w_in, cmp_k_pe, cmp_k_w1, cmp_k_w2, cmp_v_pe, cmp_v_w1, cmp_v_w2, ssm_a_re, ssm_a_im, ssm_b_re, ssm_b_im, ssm_c_re, ssm_c_im, ssm_d, ssm_log_dt, ssm_w_glu, w_proj_a, w_proj_b, w_out, norm_ffn, w_up, conv_w, conv_b, w_down, norm_final):
    bsz, seq, d_model = x.shape
    assert bsz == 1 and norm_mix.shape[0] == 1
    assert seq % max(KV_TILE, ROW_TILE) == 0 and KV_TILE // SLC_LEN <= BF16_ROWS and seq >= WINDOW + Q_TILE
    attn_w = N_HEADS * HEAD_DIM
    kv_w = N_KV_HEADS * HEAD_DIM
    ssm_width = ssm_d.shape[1]
    x2 = x[0]
    pos = positions[0]

    sizes = (attn_w,) + (kv_w,) * 6 + (3 * N_HEADS, ssm_width, d_model, d_model)
    offs = [0]
    for s in sizes:
        offs.append(offs[-1] + s)
    w = w_in[0]
    col = lambda i: w[:, offs[i]:offs[i + 1]]
    w_q, w_kc, w_vc, w_ks, w_vs, w_kw, w_vw, w_ng, w_u, w_ga, w_gb = [col(i) for i in range(11)]
    w_ng = w_ng.reshape(d_model, N_KV_HEADS, Q_PER_KV, 3).transpose(0, 1, 3, 2).reshape(d_model, N_KV_HEADS, 3 * Q_PER_KV)
    w_ng = jnp.pad(w_ng, ((0, 0), (0, 0), (0, GATE_ROWS - 3 * Q_PER_KV))).reshape(d_model, N_KV_HEADS * GATE_ROWS)
    w_t = jnp.concatenate([w_q, w_ks, w_kw, w_vs, w_vw, w_ng], axis=1).T.astype(BF16)
    w_nat = jnp.concatenate([w_kc, w_vc, w_u], axis=1).astype(BF16)
    w_gate = jnp.concatenate([w_ga, w_gb], axis=1).astype(BF16)

    inv_freq = ROPE_THETA ** (-jnp.arange(0, ROT_DIM, 2, dtype=F32) / ROT_DIM)
    qt, ks, kw, vst, vwt, ngt, kvc, u = _inproj(x2, norm_mix, pos[None, :], inv_freq[:, None], w_nat, w_t)

    n_half = seq // CMP_STRIDE
    n_cmp = (seq - CMP_LEN) // CMP_STRIDE + 1
    h4 = kvc.reshape(seq, 2 * N_KV_HEADS, HEAD_DIM).transpose(1, 0, 2).reshape(
        2 * N_KV_HEADS, n_half, CMP_STRIDE * HEAD_DIM)
    halves = lambda pe: pe.reshape(2, CMP_STRIDE * HEAD_DIM)
    pe4 = jnp.stack([halves(cmp_k_pe[0])] * N_KV_HEADS + [halves(cmp_v_pe[0])] * N_KV_HEADS)
    w1 = jnp.stack([cmp_k_w1[0], cmp_v_w1[0]]).astype(BF16)
    w2 = jnp.pad(jnp.stack([cmp_k_w2[0], cmp_v_w2[0]]), ((0, 0), (0, 0), (0, LANES - HEAD_DIM))).astype(BF16)
    cmp_end = jnp.arange(n_cmp, dtype=jnp.int32) * CMP_STRIDE + CMP_LEN - 1
    cmp_pos = jnp.pad(pos[cmp_end], (0, n_half - n_cmp))
    pat = _rope_lane_patterns()
    pat2 = jnp.stack([pat, jnp.zeros_like(pat)])
    cmp = _compress(h4, pe4, w1, w2, cmp_pos[:, None], pat2)[:, :, :HEAD_DIM]
    n_slc = seq // SLC_LEN
    cmp = cmp.reshape(2 * N_KV_HEADS, n_slc, CMP_PER_SLC, HEAD_DIM).transpose(0, 2, 1, 3).reshape(
        2 * N_KV_HEADS, n_half, HEAD_DIM).astype(BF16)
    kc = cmp[:N_KV_HEADS]
    vct = cmp[N_KV_HEADS:].transpose(0, 2, 1)

    y_a = _attention(qt, kc, vct, ks, vst, kw, vwt, ngt, seq)

    ab_re, ab_im, bb_re, bb_im = _discretize(ssm_a_re[0], ssm_a_im[0], ssm_log_dt[0], ssm_b_re[0], ssm_b_im[0])
    n_blk = 2
    a_lane = jnp.stack([ab_re.reshape(-1), ab_im.reshape(-1)])
    to_in = lambda b: _block_diag(b.transpose(0, 2, 1), n_blk)
    bblk = jnp.concatenate([to_in(bb_re), to_in(bb_im)], axis=2).astype(BF16)
    to_out = lambda cm: _block_diag(cm.transpose(0, 2, 1), n_blk)
    cblk = jnp.concatenate([to_out(ssm_c_re[0]), -to_out(ssm_c_im[0])], axis=1).astype(BF16)
    steps = ROW_TILE // SCAN_SEGMENTS
    nt = seq // ROW_TILE
    y_b = _s5(u, bblk, a_lane, cblk, ssm_d, ssm_w_glu[0].astype(BF16))
    y_b = y_b.reshape(nt, steps, SCAN_SEGMENTS, ssm_width).transpose(0, 2, 1, 3).reshape(seq, ssm_width)

    out = _ffn(x2, y_a, y_b, norm_mix, w_gate, w_proj_a[0].astype(BF16), w_proj_b[0].astype(BF16),
               w_out[0].astype(BF16), norm_ffn, w_up[0].astype(BF16), conv_w[0], conv_b, w_down[0].astype(BF16),
               norm_final[None, :])
    return out[None]
```

```python
import functools

import jax
import jax.numpy as jnp
from jax import lax
from jax.experimental import pallas as pl
from jax.experimental.pallas import tpu as pltpu

F32 = jnp.float32
BF16 = jnp.bfloat16

N_HEADS = 8
N_KV_HEADS = 2
HEAD_DIM = 64
Q_PER_KV = N_HEADS // N_KV_HEADS
ROT_DIM = HEAD_DIM // 4
ROPE_THETA = 500000.0
CMP_LEN = 32
CMP_STRIDE = 16
CMP_PER_SLC = 4
SLC_LEN = 64
SLC_TOPK = 16
WINDOW = 512
SSM_GROUP = 16
SSM_STATE = 64
CONV_WIDTH = 3
EPS = 1e-6
NEG_INF = -1e30
FORCE_SCORE = 1e4
LOG2E = 1.4426950408889634

LANES = 128
SUBLANES = 8
BF16_ROWS = 16
VMEM_BYTES_V7X = 64 * 1024 * 1024
VMEM_LIMIT_BYTES = VMEM_BYTES_V7X * 7 // 8

ROW_TILE = 512
FFN_ROW_TILE = 256
FFN_COL_CHUNK = 2816
Q_TILE = 256
KV_TILE = 512
SCORE_SLOTS = 2
FRONT_CLASSES = 8
SCAN_SEGMENTS = SUBLANES
SCAN_LANE_BLOCK = 1024


def _rmsnorm(x, g):
    return x * lax.rsqrt(jnp.mean(x * x, axis=-1, keepdims=True) + EPS) * g


def _dot(a, b):
    return jnp.dot(a, b, preferred_element_type=F32)


def _dot_nt(a, b):
    return lax.dot_general(a, b, (((1,), (1,)), ((), ())), preferred_element_type=F32)


def _rope_lane_patterns():
    half = ROT_DIM // 2
    d = jnp.arange(LANES) % HEAD_DIM
    inv_freq = ROPE_THETA ** (-jnp.arange(0, ROT_DIM, 2, dtype=F32) / ROT_DIM)
    freq = jnp.where(d < ROT_DIM, inv_freq[d % half], 0.0)
    m_lo = jnp.where(d < half, -1.0, 0.0)
    m_hi = jnp.where((d >= half) & (d < ROT_DIM), 1.0, 0.0)
    return jnp.stack([freq, m_lo, m_hi]).astype(F32)


def _rope_tables(pos_col, pat):
    ang = pos_col.astype(F32) * pat[0:1]
    sin = jnp.sin(ang)
    return jnp.cos(ang), sin * pat[1:2], sin * pat[2:3]


def _rope(x, cos, sin_lo, sin_hi):
    half = ROT_DIM // 2
    return (x * cos + pltpu.roll(x, LANES - half, axis=1) * sin_lo
            + pltpu.roll(x, half, axis=1) * sin_hi)


KVC_COLS = 2 * N_KV_HEADS * HEAD_DIM
V_ROWS = HEAD_DIM + BF16_ROWS
GATE_ROWS = 16


def _rope_t(xh, cos, sin):
    half = ROT_DIM // 2
    x1, x2 = xh[0:half], xh[half:ROT_DIM]
    return jnp.concatenate([x1 * cos - x2 * sin, x2 * cos + x1 * sin, xh[ROT_DIM:]], axis=0)


def _inproj_kernel(x_ref, g_ref, pos_ref, invf_ref, wn_ref, wt_ref,
                   qt_ref, ks_ref, kw_ref, vst_ref, vwt_ref, ngt_ref, kvc_ref, u_ref):
    tm = x_ref.shape[0]
    h = _rmsnorm(x_ref[...], g_ref[...]).astype(BF16)
    pn = _dot(h, wn_ref[...])
    kvc_ref[...] = pn[:, :KVC_COLS]
    for k in range(u_ref.shape[0]):
        u_ref[k] = pn[:, KVC_COLS + k * LANES:KVC_COLS + (k + 1) * LANES]

    pt = _dot_nt(wt_ref[...], h)
    ang = invf_ref[...] * pos_ref[...].astype(F32)
    cos, sin = jnp.cos(ang), jnp.sin(ang)
    q_scale = (HEAD_DIM ** -0.5) * LOG2E
    for hd in range(N_HEADS):
        rows = slice(hd * HEAD_DIM, (hd + 1) * HEAD_DIM)
        qt_ref[rows, :] = (_rope_t(pt[rows], cos, sin) * q_scale).astype(BF16)

    kv_w = N_KV_HEADS * HEAD_DIM
    r_ks = N_HEADS * HEAD_DIM
    r_kw, r_vs, r_vw, r_ng = r_ks + kv_w, r_ks + 2 * kv_w, r_ks + 3 * kv_w, r_ks + 4 * kv_w
    key = pl.program_id(0) * tm + lax.broadcasted_iota(jnp.int32, (BF16_ROWS, tm), 1)
    blk = lax.broadcasted_iota(jnp.int32, (BF16_ROWS, tm), 0)
    onehot = jnp.where((key % KV_TILE) // SLC_LEN == blk, 1.0, 0.0)
    ones_rows = jnp.where(blk == 0, 1.0, 0.0)
    pad_k = jnp.zeros((LANES - HEAD_DIM - BF16_ROWS, tm), F32)
    pad_w = jnp.zeros((LANES - HEAD_DIM, tm), F32)
    for kh in range(N_KV_HEADS):
        off = kh * HEAD_DIM
        k_s = _rope_t(pt[r_ks + off:r_ks + off + HEAD_DIM], cos, sin)
        ks_ref[kh] = jnp.concatenate([k_s, onehot, pad_k], axis=0).T.astype(BF16)
        k_w = _rope_t(pt[r_kw + off:r_kw + off + HEAD_DIM], cos, sin)
        kw_ref[kh] = jnp.concatenate([k_w, pad_w], axis=0).T.astype(BF16)
        vst_ref[kh] = jnp.concatenate([pt[r_vs + off:r_vs + off + HEAD_DIM], ones_rows], axis=0).astype(BF16)
        vwt_ref[kh] = jnp.concatenate([pt[r_vw + off:r_vw + off + HEAD_DIM], ones_rows], axis=0).astype(BF16)
    ngt_ref[...] = pt[r_ng:r_ng + N_KV_HEADS * GATE_ROWS]


def _inproj(x2, g, pos_row, inv_freq, w_nat, w_t):
    seq, d_model = x2.shape
    tm = ROW_TILE
    n_nat = w_nat.shape[1]
    n_t = w_t.shape[0]
    attn_w = N_HEADS * HEAD_DIM
    row = lambda i: (i, 0)
    col = lambda i: (0, i)
    const = lambda i: (0, 0)
    head_rows = lambda i: (0, i, 0)
    head_cols = lambda i: (0, 0, i)
    return pl.pallas_call(
        _inproj_kernel,
        grid=(seq // tm,),
        in_specs=[
            pl.BlockSpec((tm, d_model), row),
            pl.BlockSpec((1, d_model), const),
            pl.BlockSpec((1, tm), col),
            pl.BlockSpec((ROT_DIM // 2, 1), const),
            pl.BlockSpec((d_model, n_nat), const),
            pl.BlockSpec((n_t, d_model), const),
        ],
        out_specs=[
            pl.BlockSpec((attn_w, tm), col),
            pl.BlockSpec((N_KV_HEADS, tm, LANES), head_rows),
            pl.BlockSpec((N_KV_HEADS, tm, LANES), head_rows),
            pl.BlockSpec((N_KV_HEADS, V_ROWS, tm), head_cols),
            pl.BlockSpec((N_KV_HEADS, V_ROWS, tm), head_cols),
            pl.BlockSpec((N_KV_HEADS * GATE_ROWS, tm), col),
            pl.BlockSpec((tm, KVC_COLS), row),
            pl.BlockSpec(((n_nat - KVC_COLS) // LANES, tm, LANES), head_rows),
        ],
        out_shape=[
            jax.ShapeDtypeStruct((attn_w, seq), BF16),
            jax.ShapeDtypeStruct((N_KV_HEADS, seq, LANES), BF16),
            jax.ShapeDtypeStruct((N_KV_HEADS, seq, LANES), BF16),
            jax.ShapeDtypeStruct((N_KV_HEADS, V_ROWS, seq), BF16),
            jax.ShapeDtypeStruct((N_KV_HEADS, V_ROWS, seq), BF16),
            jax.ShapeDtypeStruct((N_KV_HEADS * GATE_ROWS, seq), F32),
            jax.ShapeDtypeStruct((seq, KVC_COLS), F32),
            jax.ShapeDtypeStruct(((n_nat - KVC_COLS) // LANES, seq, LANES), F32),
        ],
        compiler_params=pltpu.CompilerParams(
            dimension_semantics=("arbitrary",), vmem_limit_bytes=VMEM_LIMIT_BYTES),
        name="inproj",
    )(x2, g, pos_row, inv_freq, w_nat, w_t)


def _compress_kernel(h_ref, pe_ref, w1_ref, w2_ref, pos_ref, pat_ref, out_ref):
    hh = h_ref[0]
    rows, half_w = hh.shape
    a = _dot((hh + pe_ref[0, 0:1]).astype(BF16), w1_ref[0, :half_w])
    b = _dot((hh + pe_ref[0, 1:2]).astype(BF16), w1_ref[0, half_w:])
    pre = a + pltpu.roll(b, rows - 1, axis=0)
    z = _dot(jax.nn.gelu(pre).astype(BF16), w2_ref[0])
    cos, sin_lo, sin_hi = _rope_tables(pos_ref[...], pat_ref[0])
    out_ref[0] = _rope(z, cos, sin_lo, sin_hi)


def _compress(h4, pe4, w1, w2, cmp_pos_col, pat2):
    n, rows, half_w = h4.shape
    hidden = w1.shape[2]
    per = n // 2
    return pl.pallas_call(
        _compress_kernel,
        grid=(n,),
        in_specs=[
            pl.BlockSpec((1, rows, half_w), lambda i: (i, 0, 0)),
            pl.BlockSpec((1, 2, half_w), lambda i: (i, 0, 0)),
            pl.BlockSpec((1, 2 * half_w, hidden), lambda i: (i // per, 0, 0)),
            pl.BlockSpec((1, hidden, LANES), lambda i: (i // per, 0, 0)),
            pl.BlockSpec((rows, 1), lambda i: (0, 0)),
            pl.BlockSpec((1, 3, LANES), lambda i: (i // per, 0, 0)),
        ],
        out_specs=pl.BlockSpec((1, rows, LANES), lambda i: (i, 0, 0)),
        out_shape=jax.ShapeDtypeStruct((n, rows, LANES), F32),
        compiler_params=pltpu.CompilerParams(
            dimension_semantics=("arbitrary",), vmem_limit_bytes=VMEM_LIMIT_BYTES),
        name="compress",
    )(h4, pe4, w1, w2, cmp_pos_col, pat2)


def _attn_kernel(qt_ref, kc_ref, vct_ref, ks_ref, vst_ref, kw_ref, vwt_ref, ng_ref, wb_ref,
                 out_ref, bias_ref, s_ref, acc_ref, m_ref, oc_ref, ow_ref, *, n_slc, topk):
    c = pl.program_id(1)
    qw = Q_PER_KV * Q_TILE
    t0 = c * Q_TILE
    t_lane = t0 + (lax.broadcasted_iota(jnp.int32, (1, qw), 1) & (Q_TILE - 1))
    qt4 = qt_ref[...]
    qt = jnp.concatenate([qt4[g * HEAD_DIM:(g + 1) * HEAD_DIM] for g in range(Q_PER_KV)], axis=1)

    def lanes4(x):
        return jnp.concatenate([x] * Q_PER_KV, axis=1)

    blocks_per_tile = KV_TILE // SLC_LEN
    q_lane = (lax.broadcasted_iota(jnp.int32, (1, qw), 1) & (Q_TILE - 1)).astype(F32)
    tq = t0 + lax.broadcasted_iota(jnp.int32, (1, Q_TILE), 1)

    def front(nj):
        rows = CMP_PER_SLC * nj
        kc = jnp.concatenate([kc_ref[0, r * n_slc:r * n_slc + nj, :] for r in range(CMP_PER_SLC)], axis=0)
        s_c = _dot(kc, qt)
        row = lax.broadcasted_iota(jnp.int32, (rows, 1), 0)
        n_idx = CMP_PER_SLC * (row % nj) + row // nj
        first_q = (n_idx * CMP_STRIDE + (CMP_LEN - 1) - t0).astype(F32)
        s_c = jnp.where(first_q <= q_lane, s_c, NEG_INF)
        m_c = jnp.max(s_c, axis=0, keepdims=True)
        e_c = jnp.exp2(s_c - m_c)
        inv_c = jnp.where(m_c > 0.5 * NEG_INF, 1.0 / jnp.maximum(jnp.sum(e_c, axis=0, keepdims=True), 1e-30), 0.0)
        e_b = e_c.astype(BF16)
        if nj < n_slc:
            gap = jnp.zeros((n_slc - nj, qw), BF16)
            e_b = jnp.concatenate(
                [piece for r in range(CMP_PER_SLC) for piece in (e_b[r * nj:(r + 1) * nj], gap)], axis=0)
        oc_ref[...] = _dot(vct_ref[0], e_b) * inv_c

        p_c = e_c * inv_c
        psum = p_c[:, 0:Q_TILE]
        for g in range(1, Q_PER_KV):
            psum = psum + p_c[:, g * Q_TILE:(g + 1) * Q_TILE]
        j_idx = lax.broadcasted_iota(jnp.int32, (nj, 1), 0)
        straddle = psum[3 * nj:4 * nj]
        from_prev = jnp.where(j_idx == 0, 0.0, pltpu.roll(straddle, 1, axis=0))
        imp = (psum[0:nj] + psum[nj:2 * nj] + psum[2 * nj:3 * nj]
               + 0.5 * straddle + 0.5 * from_prev)

        span = WINDOW + Q_TILE
        w0 = pl.multiple_of(jnp.maximum(t0 - WINDOW, 0), Q_TILE)
        rhs_w = jnp.concatenate([qt, jnp.zeros((LANES - HEAD_DIM, qw), BF16)], axis=0)
        wb_off = pl.multiple_of(WINDOW - (t0 - w0), Q_TILE)
        s_w = _dot(kw_ref[0, pl.ds(w0, span), :], rhs_w) + lanes4(wb_ref[pl.ds(wb_off, span), :])
        e_w = jnp.exp2(s_w - jnp.max(s_w, axis=0, keepdims=True)).astype(BF16)
        acc_w = _dot(vwt_ref[0, :, pl.ds(w0, span)], e_w)
        ow_ref[...] = acc_w[:HEAD_DIM] * (1.0 / jnp.maximum(acc_w[HEAD_DIM:HEAD_DIM + 1], 1e-30))

        forced = (j_idx == 0) | (j_idx == tq // SLC_LEN)
        valid = (j_idx * SLC_LEN) <= tq
        score = jnp.where(valid, jnp.where(forced, FORCE_SCORE, imp), NEG_INF)
        j_f = j_idx.astype(F32)
        for _ in range(min(topk, nj)):
            best = jnp.max(score, axis=0, keepdims=True)
            first = jnp.min(jnp.where(score == best, j_f, float(nj)), axis=0, keepdims=True)
            score = jnp.where(j_f == first, -jnp.inf, score)
        tiles = nj // blocks_per_tile
        bias = jnp.where(valid & (score == -jnp.inf), 0.0, NEG_INF)
        bias_ref[0:tiles, 0:blocks_per_tile, :] = bias.reshape(tiles, blocks_per_tile, Q_TILE)
        if blocks_per_tile < BF16_ROWS:
            bias_ref[0:tiles, blocks_per_tile:, :] = jnp.zeros((tiles, BF16_ROWS - blocks_per_tile, Q_TILE), F32)

    j_vis = (t0 + Q_TILE) // SLC_LEN
    step = n_slc // FRONT_CLASSES
    for k in range(FRONT_CLASSES):
        pl.when((j_vis > k * step) & (j_vis <= (k + 1) * step))(functools.partial(front, (k + 1) * step))

    zeros_tail = jnp.zeros((LANES - HEAD_DIM - BF16_ROWS, qw), BF16)

    def scores(slot, j):
        k0 = pl.multiple_of(j * KV_TILE, KV_TILE)
        rhs = jnp.concatenate([qt, lanes4(bias_ref[j]).astype(BF16), zeros_tail], axis=0)
        s_ref[slot] = _dot(ks_ref[0, pl.ds(k0, KV_TILE), :], rhs)

    def accumulate(s, j):
        k0 = pl.multiple_of(j * KV_TILE, KV_TILE)
        m_old = m_ref[...]
        m_new = jnp.maximum(m_old, jnp.max(s, axis=0, keepdims=True))
        p = jnp.exp2(s - m_new).astype(BF16)
        acc_ref[...] = jnp.exp2(m_old - m_new) * acc_ref[...] + _dot(vst_ref[0, :, pl.ds(k0, KV_TILE)], p)
        m_ref[...] = m_new

    m_ref[...] = jnp.full((1, qw), NEG_INF, F32)
    acc_ref[...] = jnp.zeros((V_ROWS, qw), F32)
    n_full = t0 // KV_TILE
    scores(0, 0)

    def pair(j):
        scores(1, j + 1)
        accumulate(s_ref[0], j)
        scores(0, j + 2)
        accumulate(s_ref[1], j + 1)

    def two_pairs(i, _):
        pair(4 * i)
        pair(4 * i + 2)
        return 0

    n_pairs = n_full // 2
    lax.fori_loop(0, n_pairs // 2, two_pairs, 0)

    @pl.when(n_pairs % 2 == 1)
    def _():
        pair(2 * (n_pairs - 1))

    odd = n_full % 2
    tok = n_full * KV_TILE + lax.broadcasted_iota(jnp.int32, (KV_TILE, 1), 0)

    def last_tile(slot):
        accumulate(jnp.where(tok <= t_lane, s_ref[slot], NEG_INF), n_full)

    @pl.when(odd == 1)
    def _():
        scores(1, n_full)
        accumulate(s_ref[0], n_full - 1)
        last_tile(1)

    @pl.when(odd == 0)
    def _():
        last_tile(0)

    acc_s = acc_ref[...]
    o_s = acc_s[:HEAD_DIM] * (1.0 / jnp.maximum(acc_s[HEAD_DIM:HEAD_DIM + 1], 1e-30))

    gates = jax.nn.sigmoid(ng_ref[...])

    def gate(b):
        return jnp.concatenate([gates[b * Q_PER_KV + g:b * Q_PER_KV + g + 1] for g in range(Q_PER_KV)], axis=1)

    o = gate(0) * oc_ref[...] + gate(1) * o_s + gate(2) * ow_ref[...]
    pairs = []
    for g in range(0, Q_PER_KV, 2):
        two = jnp.concatenate([o[:, g * Q_TILE:(g + 1) * Q_TILE], o[:, (g + 1) * Q_TILE:(g + 2) * Q_TILE]], axis=0)
        pairs.append(two.T)
    out_ref[...] = jnp.concatenate(pairs, axis=1).astype(out_ref.dtype)


def _window_bias_table():
    r = jnp.arange(2 * WINDOW + Q_TILE, dtype=jnp.int32)[:, None]
    q = jnp.arange(Q_TILE, dtype=jnp.int32)[None, :]
    return jnp.where((r > q) & (r <= WINDOW + q), 0.0, NEG_INF).astype(F32)


def _attention(qt, kc, vct, ks, vst, kw, vwt, ngt, seq):
    wb = _window_bias_table()
    n_slc = seq // SLC_LEN
    nb = seq // Q_TILE
    qw = Q_PER_KV * Q_TILE
    n_rows = CMP_PER_SLC * n_slc
    topk = min(SLC_TOPK, n_slc)
    assert (n_slc // FRONT_CLASSES) % (KV_TILE // SLC_LEN) == 0

    def per_head(block):
        return pl.BlockSpec(block, lambda k, c: (k, 0, 0), pipeline_mode=pl.Buffered(1))

    return pl.pallas_call(
        functools.partial(_attn_kernel, n_slc=n_slc, topk=topk),
        grid=(N_KV_HEADS, nb),
        in_specs=[
            pl.BlockSpec((Q_PER_KV * HEAD_DIM, Q_TILE), lambda k, c: (k, c)),
            per_head((1, n_rows, HEAD_DIM)),
            per_head((1, HEAD_DIM, n_rows)),
            per_head((1, seq, LANES)),
            per_head((1, V_ROWS, seq)),
            per_head((1, seq, LANES)),
            per_head((1, V_ROWS, seq)),
            pl.BlockSpec((GATE_ROWS, Q_TILE), lambda k, c: (k, c)),
            pl.BlockSpec(wb.shape, lambda k, c: (0, 0), pipeline_mode=pl.Buffered(1)),
        ],
        out_specs=pl.BlockSpec((Q_TILE, Q_PER_KV * HEAD_DIM), lambda k, c: (c, k)),
        out_shape=jax.ShapeDtypeStruct((seq, N_HEADS * HEAD_DIM), BF16),
        scratch_shapes=[
            pltpu.VMEM((n_slc * SLC_LEN // KV_TILE, BF16_ROWS, Q_TILE), F32),
            pltpu.VMEM((SCORE_SLOTS, KV_TILE, qw), F32),
            pltpu.VMEM((V_ROWS, qw), F32),
            pltpu.VMEM((1, qw), F32),
            pltpu.VMEM((HEAD_DIM, qw), F32),
            pltpu.VMEM((HEAD_DIM, qw), F32),
        ],
        compiler_params=pltpu.CompilerParams(
            dimension_semantics=("arbitrary", "arbitrary"), vmem_limit_bytes=VMEM_LIMIT_BYTES),
        name="nsa_attention",
    )(qt, kc, vct, ks, vst, kw, vwt, ngt, wb)


def _discretize_kernel(are_ref, aim_ref, ldt_ref, bre_ref, bim_ref, ar_ref, ai_ref, br_ref, bi_ref):
    a_re, a_im = are_ref[...], aim_ref[...]
    dt = jnp.exp(ldt_ref[...])
    mag = jnp.exp(a_re * dt)
    ang = a_im * dt
    ab_re, ab_im = mag * jnp.cos(ang), mag * jnp.sin(ang)
    n_re, n_im = ab_re - 1.0, ab_im
    den = a_re * a_re + a_im * a_im
    c_re = (n_re * a_re + n_im * a_im) / den
    c_im = (n_im * a_re - n_re * a_im) / den
    b_re, b_im = bre_ref[...], bim_ref[...]
    ar_ref[...] = ab_re
    ai_ref[...] = ab_im
    br_ref[...] = c_re * b_re - c_im * b_im
    bi_ref[...] = c_re * b_im + c_im * b_re


def _discretize(a_re, a_im, log_dt, b_re, b_im):
    g, p, h = b_re.shape
    col = jax.ShapeDtypeStruct((g, p, 1), F32)
    full = jax.ShapeDtypeStruct((g, p, h), F32)
    return pl.pallas_call(
        _discretize_kernel, out_shape=[col, col, full, full], name="ssm_discretize",
    )(a_re[..., None], a_im[..., None], log_dt[:, None, None], b_re, b_im)


def _cmul(ar, ai, br, bi):
    return ar * br - ai * bi, ar * bi + ai * br


def _s5_kernel(u_ref, bblk_ref, a_ref, cblk_ref, d_ref, wglu_ref, y_ref, xr_ref, xi_ref, st_ref, pw_ref):
    seg = SCAN_SEGMENTS
    steps = xr_ref.shape[0]
    n_state = xr_ref.shape[2]
    n_blk, blk_in, blk_state2 = bblk_ref.shape
    blk_state = blk_state2 // 2

    @pl.when(pl.program_id(0) == 0)
    def _():
        st_ref[...] = jnp.zeros_like(st_ref)
        wr, wi = a_ref[0:1, :], a_ref[1:2, :]
        for j in range(steps):
            pw_ref[j, 0] = jnp.broadcast_to(wr, (seg, n_state))
            pw_ref[j, 1] = jnp.broadcast_to(wi, (seg, n_state))
            wr, wi = _cmul(wr, wi, a_ref[0:1, :], a_ref[1:2, :])

    u = jnp.concatenate(
        [jnp.concatenate([u_ref[k, pl.ds(j, seg, stride=steps), :] for j in range(steps)], axis=0)
         for k in range(u_ref.shape[0])], axis=1)
    ub = u.astype(BF16)
    for k in range(n_blk):
        bu = _dot(ub[:, k * blk_in:(k + 1) * blk_in], bblk_ref[k])
        xr_ref[:, :, k * blk_state:(k + 1) * blk_state] = bu[:, :blk_state].reshape(steps, seg, blk_state)
        xi_ref[:, :, k * blk_state:(k + 1) * blk_state] = bu[:, blk_state:].reshape(steps, seg, blk_state)

    lb = SCAN_LANE_BLOCK
    for b0 in range(0, n_state, lb):
        lanes = slice(b0, b0 + lb)
        a_re = a_ref[0:1, lanes]
        a_im = a_ref[1:2, lanes]
        ar8 = jnp.broadcast_to(a_re, (seg, lb))
        ai8 = jnp.broadcast_to(a_im, (seg, lb))

        def local(j, carry):
            xr, xi = carry
            pr, pi = _cmul(ar8, ai8, xr, xi)
            xr = pr + xr_ref[j, :, lanes]
            xi = pi + xi_ref[j, :, lanes]
            xr_ref[j, :, lanes] = xr
            xi_ref[j, :, lanes] = xi
            return xr, xi

        end_r, end_i = lax.fori_loop(0, steps, local, (jnp.zeros((seg, lb), F32), jnp.zeros((seg, lb), F32)))

        pr, pi = a_re, a_im
        for _ in range(steps.bit_length() - 1):
            pr, pi = _cmul(pr, pi, pr, pi)
        cr, ci = st_ref[0:1, lanes], st_ref[1:2, lanes]
        in_r, in_i = [], []
        for s in range(seg):
            in_r.append(cr)
            in_i.append(ci)
            qr, qi = _cmul(pr, pi, cr, ci)
            cr, ci = qr + end_r[s:s + 1], qi + end_i[s:s + 1]
        st_ref[0:1, lanes] = cr
        st_ref[1:2, lanes] = ci
        in_r = jnp.concatenate(in_r, axis=0)
        in_i = jnp.concatenate(in_i, axis=0)

        def fix(j, carry):
            fr, fi = _cmul(pw_ref[j, 0, :, lanes], pw_ref[j, 1, :, lanes], in_r, in_i)
            xr_ref[j, :, lanes] = xr_ref[j, :, lanes] + fr
            xi_ref[j, :, lanes] = xi_ref[j, :, lanes] + fi
            return carry

        lax.fori_loop(0, steps, fix, 0)

    rows = steps * seg
    y = d_ref[...] * u
    for k in range(n_blk):
        st = slice(k * blk_state, (k + 1) * blk_state)
        xs = jnp.concatenate([xr_ref[:, :, st].reshape(rows, blk_state),
                              xi_ref[:, :, st].reshape(rows, blk_state)], axis=1).astype(BF16)
        yk = _dot(xs, cblk_ref[k])
        if k == 0:
            ys = [yk]
        else:
            ys.append(yk)
    y = y + jnp.concatenate(ys, axis=1)
    y = jax.nn.gelu(y)
    y = y * jax.nn.sigmoid(_dot(y.astype(BF16), wglu_ref[...]))
    y_ref[...] = y.astype(y_ref.dtype)


def _s5(u_slabs, bblk, a_lane, cblk, d_row, w_glu):
    n_slab, seq, _ = u_slabs.shape
    width = n_slab * LANES
    tm = ROW_TILE
    steps = tm // SCAN_SEGMENTS
    n_state = a_lane.shape[1]
    const2 = lambda i: (0, 0)
    const3 = lambda i: (0, 0, 0)
    return pl.pallas_call(
        _s5_kernel,
        grid=(seq // tm,),
        in_specs=[
            pl.BlockSpec((n_slab, tm, LANES), lambda i: (0, i, 0)),
            pl.BlockSpec(bblk.shape, const3),
            pl.BlockSpec(a_lane.shape, const2),
            pl.BlockSpec(cblk.shape, const3),
            pl.BlockSpec((1, width), const2),
            pl.BlockSpec((width, width), const2),
        ],
        out_specs=pl.BlockSpec((tm, width), lambda i: (i, 0)),
        out_shape=jax.ShapeDtypeStruct((seq, width), BF16),
        scratch_shapes=[
            pltpu.VMEM((steps, SCAN_SEGMENTS, n_state), F32),
            pltpu.VMEM((steps, SCAN_SEGMENTS, n_state), F32),
            pltpu.VMEM((2, n_state), F32),
            pltpu.VMEM((steps, 2, SCAN_SEGMENTS, n_state), F32),
        ],
        compiler_params=pltpu.CompilerParams(
            dimension_semantics=("arbitrary",), vmem_limit_bytes=VMEM_LIMIT_BYTES),
        name="s5_scan",
    )(u_slabs, bblk, a_lane, cblk, d_row, w_glu)


def _ffn_kernel(x_ref, ya_ref, yb_ref, gmix_ref, wg_ref, wpa_ref, wpb_ref, wout_ref, gffn_ref,
                wup_ref, cw_ref, cb_ref, wdown_ref, gfin_ref, o_ref, tail_ref, *, d_ff, chunk):
    tm, d_model = x_ref.shape

    @pl.when(pl.program_id(0) == 0)
    def _():
        tail_ref[...] = jnp.zeros_like(tail_ref)

    x = x_ref[...]
    h = _rmsnorm(x, gmix_ref[...]).astype(BF16)
    gates = jax.nn.sigmoid(_dot(h, wg_ref[...]))
    merged = (gates[:, :d_model] * _dot(ya_ref[...], wpa_ref[...])
              + gates[:, d_model:] * _dot(yb_ref[...], wpb_ref[...]))
    x1 = x + _dot(merged.astype(BF16), wout_ref[...])
    h2 = _rmsnorm(x1, gffn_ref[...]).astype(BF16)

    def conv_cols(c0):
        cols = slice(c0, c0 + chunk)
        up = _dot(h2, wup_ref[:, cols])
        ext = jnp.concatenate([tail_ref[:, cols], up], axis=0)
        tail_ref[:, cols] = up[tm - SUBLANES:, :]
        prev1 = pltpu.roll(ext, 1, axis=0)[SUBLANES:]
        prev2 = pltpu.roll(ext, 2, axis=0)[SUBLANES:]
        return cb_ref[:, cols] + cw_ref[0:1, cols] * prev2 + cw_ref[1:2, cols] * prev1 + cw_ref[2:3, cols] * up

    acc = x1
    for k in range(d_ff // chunk):
        gate = conv_cols(k * chunk)
        val = conv_cols(d_ff + k * chunk)
        act = (jax.nn.silu(gate) * val).astype(BF16)
        acc = acc + _dot(act, wdown_ref[k * chunk:(k + 1) * chunk, :])
    o_ref[...] = _rmsnorm(acc, gfin_ref[...])


def _ffn(x2, ya, yb, g_mix, w_g, w_pa, w_pb, w_out, g_ffn, w_up, conv_w, conv_b, w_down, g_fin):
    seq, d_model = x2.shape
    d_ff = w_down.shape[0]
    tm = FFN_ROW_TILE
    chunk = FFN_COL_CHUNK
    assert d_ff % chunk == 0 and chunk % LANES == 0
    row = lambda i: (i, 0)

    def resident(arr):
        return pl.BlockSpec(arr.shape, lambda i: (0, 0), pipeline_mode=pl.Buffered(1))

    return pl.pallas_call(
        functools.partial(_ffn_kernel, d_ff=d_ff, chunk=chunk),
        grid=(seq // tm,),
        in_specs=[
            pl.BlockSpec((tm, d_model), row),
            pl.BlockSpec((tm, ya.shape[1]), row),
            pl.BlockSpec((tm, yb.shape[1]), row),
            resident(g_mix), resident(w_g), resident(w_pa), resident(w_pb), resident(w_out), resident(g_ffn),
            resident(w_up), resident(conv_w), resident(conv_b), resident(w_down), resident(g_fin),
        ],
        out_specs=pl.BlockSpec((tm, d_model), row),
        out_shape=jax.ShapeDtypeStruct((seq, d_model), F32),
        scratch_shapes=[pltpu.VMEM((SUBLANES, 2 * d_ff), F32)],
        compiler_params=pltpu.CompilerParams(
            dimension_semantics=("arbitrary",), vmem_limit_bytes=VMEM_LIMIT_BYTES),
        name="merge_ffn",
    )(x2, ya, yb, g_mix, w_g, w_pa, w_pb, w_out, g_ffn, w_up, conv_w, conv_b, w_down, g_fin)


def _block_diag(m, n_blk):
    g, r, c = m.shape
    per = g // n_blk
    eye = jnp.eye(per, dtype=m.dtype)
    m = m.reshape(n_blk, per, r, c)
    return jnp.einsum("bgrc,gk->bgrkc", m, eye).reshape(n_blk, per * r, per * c)


def kernel(x, positions, norm_mix, w_in, cmp_k_pe, cmp_k_w1, cmp_k_w2, cmp_v_pe, cmp_v_w1, cmp_v_w2, ssm_a_re, ssm_a_im, ssm_b_re, ssm_b_im, ssm_c_re, ssm_c_im, ssm_d, ssm_log_dt, ssm_w_glu, w_proj_a, w_proj_b, w_out, norm_ffn, w_up, conv_w, conv_b, w_down, norm_final):
    bsz, seq, d_model = x.shape
    assert bsz == 1 and norm_mix.shape[0] == 1
    assert seq % max(KV_TILE, ROW_TILE) == 0 and KV_TILE // SLC_LEN <= BF16_ROWS and seq >= WINDOW + Q_TILE
    attn_w = N_HEADS * HEAD_DIM
    kv_w = N_KV_HEADS * HEAD_DIM
    ssm_width = ssm_d.shape[1]
    x2 = x[0]
    pos = positions[0]

    sizes = (attn_w,) + (kv_w,) * 6 + (3 * N_HEADS, ssm_width, d_model, d_model)
    offs = [0]
    for s in sizes:
        offs.append(offs[-1] + s)
    w = w_in[0]
    col = lambda i: w[:, offs[i]:offs[i + 1]]
    w_q, w_kc, w_vc, w_ks, w_vs, w_kw, w_vw, w_ng, w_u, w_ga, w_gb = [col(i) for i in range(11)]
    w_ng = w_ng.reshape(d_model, N_KV_HEADS, Q_PER_KV, 3).transpose(0, 1, 3, 2).reshape(d_model, N_KV_HEADS, 3 * Q_PER_KV)
    w_ng = jnp.pad(w_ng, ((0, 0), (0, 0), (0, GATE_ROWS - 3 * Q_PER_KV))).reshape(d_model, N_KV_HEADS * GATE_ROWS)
    w_t = jnp.concatenate([w_q, w_ks, w_kw, w_vs, w_vw, w_ng], axis=1).T.astype(BF16)
    w_nat = jnp.concatenate([w_kc, w_vc, w_u], axis=1).astype(BF16)
    w_gate = jnp.concatenate([w_ga, w_gb], axis=1).astype(BF16)

    inv_freq = ROPE_THETA ** (-jnp.arange(0, ROT_DIM, 2, dtype=F32) / ROT_DIM)
    qt, ks, kw, vst, vwt, ngt, kvc, u = _inproj(x2, norm_mix, pos[None, :], inv_freq[:, None], w_nat, w_t)

    n_half = seq // CMP_STRIDE
    n_cmp = (seq - CMP_LEN) // CMP_STRIDE + 1
    h4 = kvc.reshape(seq, 2 * N_KV_HEADS, HEAD_DIM).transpose(1, 0, 2).reshape(
        2 * N_KV_HEADS, n_half, CMP_STRIDE * HEAD_DIM)
    halves = lambda pe: pe.reshape(2, CMP_STRIDE * HEAD_DIM)
    pe4 = jnp.stack([halves(cmp_k_pe[0])] * N_KV_HEADS + [halves(cmp_v_pe[0])] * N_KV_HEADS)
    w1 = jnp.stack([cmp_k_w1[0], cmp_v_w1[0]]).astype(BF16)
    w2 = jnp.pad(jnp.stack([cmp_k_w2[0], cmp_v_w2[0]]), ((0, 0), (0, 0), (0, LANES - HEAD_DIM))).astype(BF16)
    cmp_end = jnp.arange(n_cmp, dtype=jnp.int32) * CMP_STRIDE + CMP_LEN - 1
    cmp_pos = jnp.pad(pos[cmp_end], (0, n_half - n_cmp))
    pat = _rope_lane_patterns()
    pat2 = jnp.stack([pat, jnp.zeros_like(pat)])
    cmp = _compress(h4, pe4, w1, w2, cmp_pos[:, None], pat2)[:, :, :HEAD_DIM]
    n_slc = seq // SLC_LEN
    cmp = cmp.reshape(2 * N_KV_HEADS, n_slc, CMP_PER_SLC, HEAD_DIM).transpose(0, 2, 1, 3).reshape(
        2 * N_KV_HEADS, n_half, HEAD_DIM).astype(BF16)
    kc = cmp[:N_KV_HEADS]
    vct = cmp[N_KV_HEADS:].transpose(0, 2, 1)

    y_a = _attention(qt, kc, vct, ks, vst, kw, vwt, ngt, seq)

    ab_re, ab_im, bb_re, bb_im = _discretize(ssm_a_re[0], ssm_a_im[0], ssm_log_dt[0], ssm_b_re[0], ssm_b_im[0])
    n_blk = 2
    a_lane = jnp.stack([ab_re.reshape(-1), ab_im.reshape(-1)])
    to_in = lambda b: _block_diag(b.transpose(0, 2, 1), n_blk)
    bblk = jnp.concatenate([to_in(bb_re), to_in(bb_im)], axis=2).astype(BF16)
    to_out = lambda cm: _block_diag(cm.transpose(0, 2, 1), n_blk)
    cblk = jnp.concatenate([to_out(ssm_c_re[0]), -to_out(ssm_c_im[0])], axis=1).astype(BF16)
    steps = ROW_TILE // SCAN_SEGMENTS
    nt = seq // ROW_TILE
    y_b = _s5(u, bblk, a_lane, cblk, ssm_d, ssm_w_glu[0].astype(BF16))
    y_b = y_b.reshape(nt, steps, SCAN_SEGMENTS, ssm_width).transpose(0, 2, 1, 3).reshape(seq, ssm_width)

    out = _ffn(x2, y_a, y_b, norm_mix, w_gate, w_proj_a[0].astype(BF16), w_proj_b[0].astype(BF16),
               w_out[0].astype(BF16), norm_ffn, w_up[0].astype(BF16), conv_w[0], conv_b, w_down[0].astype(BF16),
               norm_final[None, :])
    return out[None]
```

```python
import functools

import jax
import jax.numpy as jnp
from jax import lax
from jax.experimental import pallas as pl
from jax.experimental.pallas import tpu as pltpu

F32 = jnp.float32
BF16 = jnp.bfloat16

N_HEADS = 8
N_KV_HEADS = 2
HEAD_DIM = 64
Q_PER_KV = N_HEADS // N_KV_HEADS
ROT_DIM = HEAD_DIM // 4
ROPE_THETA = 500000.0
CMP_LEN = 32
CMP_STRIDE = 16
CMP_PER_SLC = 4
SLC_LEN = 64
SLC_TOPK = 16
WINDOW = 512
SSM_GROUP = 16
SSM_STATE = 64
CONV_WIDTH = 3
EPS = 1e-6
NEG_INF = -1e30
FORCE_SCORE = 1e4
LOG2E = 1.4426950408889634

LANES = 128
SUBLANES = 8
BF16_ROWS = 16
VMEM_BYTES_V7X = 64 * 1024 * 1024
VMEM_LIMIT_BYTES = VMEM_BYTES_V7X * 7 // 8

ROW_TILE = 512
FFN_ROW_TILE = 256
FFN_COL_CHUNK = 2816
Q_TILE = 256
KV_TILE = 512
LANE_HALVES = 2
SCORE_SLOTS = 2
FRONT_CLASSES = 8
SCAN_SEGMENTS = SUBLANES
SCAN_LANE_BLOCK = 1024


def _rmsnorm(x, g):
    return x * lax.rsqrt(jnp.mean(x * x, axis=-1, keepdims=True) + EPS) * g


def _dot(a, b):
    return jnp.dot(a, b, preferred_element_type=F32)


def _dot_nt(a, b):
    return lax.dot_general(a, b, (((1,), (1,)), ((), ())), preferred_element_type=F32)


def _rope_lane_patterns():
    half = ROT_DIM // 2
    d = jnp.arange(LANES) % HEAD_DIM
    inv_freq = ROPE_THETA ** (-jnp.arange(0, ROT_DIM, 2, dtype=F32) / ROT_DIM)
    freq = jnp.where(d < ROT_DIM, inv_freq[d % half], 0.0)
    m_lo = jnp.where(d < half, -1.0, 0.0)
    m_hi = jnp.where((d >= half) & (d < ROT_DIM), 1.0, 0.0)
    return jnp.stack([freq, m_lo, m_hi]).astype(F32)


def _rope_tables(pos_col, pat):
    ang = pos_col.astype(F32) * pat[0:1]
    sin = jnp.sin(ang)
    return jnp.cos(ang), sin * pat[1:2], sin * pat[2:3]


def _rope(x, cos, sin_lo, sin_hi):
    half = ROT_DIM // 2
    return (x * cos + pltpu.roll(x, LANES - half, axis=1) * sin_lo
            + pltpu.roll(x, half, axis=1) * sin_hi)


KVC_COLS = 2 * N_KV_HEADS * HEAD_DIM
V_ROWS = HEAD_DIM + BF16_ROWS
GATE_ROWS = 16


def _rope_t(xh, cos, sin):
    half = ROT_DIM // 2
    x1, x2 = xh[0:half], xh[half:ROT_DIM]
    return jnp.concatenate([x1 * cos - x2 * sin, x2 * cos + x1 * sin, xh[ROT_DIM:]], axis=0)


def _inproj_kernel(x_ref, g_ref, pos_ref, invf_ref, wn_ref, wt_ref,
                   qt_ref, ks_ref, kw_ref, vst_ref, vwt_ref, ngt_ref, kvc_ref, u_ref):
    tm = x_ref.shape[0]
    h = _rmsnorm(x_ref[...], g_ref[...]).astype(BF16)
    pn = _dot(h, wn_ref[...])
    kvc_ref[...] = pn[:, :KVC_COLS]
    for k in range(u_ref.shape[0]):
        u_ref[k] = pn[:, KVC_COLS + k * LANES:KVC_COLS + (k + 1) * LANES]

    pt = _dot_nt(wt_ref[...], h)
    ang = invf_ref[...] * pos_ref[...].astype(F32)
    cos, sin = jnp.cos(ang), jnp.sin(ang)
    q_scale = (HEAD_DIM ** -0.5) * LOG2E
    for hd in range(N_HEADS):
        rows = slice(hd * HEAD_DIM, (hd + 1) * HEAD_DIM)
        qt_ref[rows, :] = (_rope_t(pt[rows], cos, sin) * q_scale).astype(BF16)

    kv_w = N_KV_HEADS * HEAD_DIM
    r_ks = N_HEADS * HEAD_DIM
    r_kw, r_vs, r_vw, r_ng = r_ks + kv_w, r_ks + 2 * kv_w, r_ks + 3 * kv_w, r_ks + 4 * kv_w
    key = pl.program_id(0) * tm + lax.broadcasted_iota(jnp.int32, (BF16_ROWS, tm), 1)
    blk = lax.broadcasted_iota(jnp.int32, (BF16_ROWS, tm), 0)
    onehot = jnp.where((key % KV_TILE) // SLC_LEN == blk, 1.0, 0.0)
    ones_rows = jnp.where(blk == 0, 1.0, 0.0)
    pad_k = jnp.zeros((LANES - HEAD_DIM - BF16_ROWS, tm), F32)
    pad_w = jnp.zeros((LANES - HEAD_DIM, tm), F32)
    for kh in range(N_KV_HEADS):
        off = kh * HEAD_DIM
        k_s = _rope_t(pt[r_ks + off:r_ks + off + HEAD_DIM], cos, sin)
        ks_ref[kh] = jnp.concatenate([k_s, onehot, pad_k], axis=0).T.astype(BF16)
        k_w = _rope_t(pt[r_kw + off:r_kw + off + HEAD_DIM], cos, sin)
        kw_ref[kh] = jnp.concatenate([k_w, pad_w], axis=0).T.astype(BF16)
        vst_ref[kh] = jnp.concatenate([pt[r_vs + off:r_vs + off + HEAD_DIM], ones_rows], axis=0).astype(BF16)
        vwt_ref[kh] = jnp.concatenate([pt[r_vw + off:r_vw + off + HEAD_DIM], ones_rows], axis=0).astype(BF16)
    ngt_ref[...] = pt[r_ng:r_ng + N_KV_HEADS * GATE_ROWS]


def _inproj(x2, g, pos_row, inv_freq, w_nat, w_t):
    seq, d_model = x2.shape
    tm = ROW_TILE
    n_nat = w_nat.shape[1]
    n_t = w_t.shape[0]
    attn_w = N_HEADS * HEAD_DIM
    row = lambda i: (i, 0)
    col = lambda i: (0, i)
    const = lambda i: (0, 0)
    head_rows = lambda i: (0, i, 0)
    head_cols = lambda i: (0, 0, i)
    return pl.pallas_call(
        _inproj_kernel,
        grid=(seq // tm,),
        in_specs=[
            pl.BlockSpec((tm, d_model), row),
            pl.BlockSpec((1, d_model), const),
            pl.BlockSpec((1, tm), col),
            pl.BlockSpec((ROT_DIM // 2, 1), const),
            pl.BlockSpec((d_model, n_nat), const),
            pl.BlockSpec((n_t, d_model), const),
        ],
        out_specs=[
            pl.BlockSpec((attn_w, tm), col),
            pl.BlockSpec((N_KV_HEADS, tm, LANES), head_rows),
            pl.BlockSpec((N_KV_HEADS, tm, LANES), head_rows),
            pl.BlockSpec((N_KV_HEADS, V_ROWS, tm), head_cols),
            pl.BlockSpec((N_KV_HEADS, V_ROWS, tm), head_cols),
            pl.BlockSpec((N_KV_HEADS * GATE_ROWS, tm), col),
            pl.BlockSpec((tm, KVC_COLS), row),
            pl.BlockSpec(((n_nat - KVC_COLS) // LANES, tm, LANES), head_rows),
        ],
        out_shape=[
            jax.ShapeDtypeStruct((attn_w, seq), BF16),
            jax.ShapeDtypeStruct((N_KV_HEADS, seq, LANES), BF16),
            jax.ShapeDtypeStruct((N_KV_HEADS, seq, LANES), BF16),
            jax.ShapeDtypeStruct((N_KV_HEADS, V_ROWS, seq), BF16),
            jax.ShapeDtypeStruct((N_KV_HEADS, V_ROWS, seq), BF16),
            jax.ShapeDtypeStruct((N_KV_HEADS * GATE_ROWS, seq), F32),
            jax.ShapeDtypeStruct((seq, KVC_COLS), F32),
            jax.ShapeDtypeStruct(((n_nat - KVC_COLS) // LANES, seq, LANES), F32),
        ],
        compiler_params=pltpu.CompilerParams(
            dimension_semantics=("arbitrary",), vmem_limit_bytes=VMEM_LIMIT_BYTES),
        name="inproj",
    )(x2, g, pos_row, inv_freq, w_nat, w_t)


def _compress_kernel(h_ref, pe_ref, w1_ref, w2_ref, pos_ref, pat_ref, out_ref):
    hh = h_ref[0]
    rows, half_w = hh.shape
    a = _dot((hh + pe_ref[0, 0:1]).astype(BF16), w1_ref[0, :half_w])
    b = _dot((hh + pe_ref[0, 1:2]).astype(BF16), w1_ref[0, half_w:])
    pre = a + pltpu.roll(b, rows - 1, axis=0)
    z = _dot(jax.nn.gelu(pre).astype(BF16), w2_ref[0])
    cos, sin_lo, sin_hi = _rope_tables(pos_ref[...], pat_ref[0])
    out_ref[0] = _rope(z, cos, sin_lo, sin_hi)


def _compress(h4, pe4, w1, w2, cmp_pos_col, pat2):
    n, rows, half_w = h4.shape
    hidden = w1.shape[2]
    per = n // 2
    return pl.pallas_call(
        _compress_kernel,
        grid=(n,),
        in_specs=[
            pl.BlockSpec((1, rows, half_w), lambda i: (i, 0, 0)),
            pl.BlockSpec((1, 2, half_w), lambda i: (i, 0, 0)),
            pl.BlockSpec((1, 2 * half_w, hidden), lambda i: (i // per, 0, 0)),
            pl.BlockSpec((1, hidden, LANES), lambda i: (i // per, 0, 0)),
            pl.BlockSpec((rows, 1), lambda i: (0, 0)),
            pl.BlockSpec((1, 3, LANES), lambda i: (i // per, 0, 0)),
        ],
        out_specs=pl.BlockSpec((1, rows, LANES), lambda i: (i, 0, 0)),
        out_shape=jax.ShapeDtypeStruct((n, rows, LANES), F32),
        compiler_params=pltpu.CompilerParams(
            dimension_semantics=("arbitrary",), vmem_limit_bytes=VMEM_LIMIT_BYTES),
        name="compress",
    )(h4, pe4, w1, w2, cmp_pos_col, pat2)


def _attn_kernel(qt_ref, kc_ref, vct_ref, ks_ref, vst_ref, kw_ref, vwt_ref, ng_ref, wb_ref,
                 out_ref, bias_ref, s_ref, acc_ref, m_ref, oc_ref, ow_ref, *, n_slc, topk):
    c = pl.program_id(1)
    qw = Q_PER_KV * Q_TILE
    t0 = c * Q_TILE
    t_lane = t0 + (lax.broadcasted_iota(jnp.int32, (1, qw), 1) & (Q_TILE - 1))
    qt4 = qt_ref[...]
    qt = jnp.concatenate([qt4[g * HEAD_DIM:(g + 1) * HEAD_DIM] for g in range(Q_PER_KV)], axis=1)

    def lanes4(x):
        return jnp.concatenate([x] * Q_PER_KV, axis=1)

    blocks_per_tile = KV_TILE // SLC_LEN
    q_lane = (lax.broadcasted_iota(jnp.int32, (1, qw), 1) & (Q_TILE - 1)).astype(F32)
    tq = t0 + lax.broadcasted_iota(jnp.int32, (1, Q_TILE), 1)

    def front(nj):
        rows = CMP_PER_SLC * nj
        kc = jnp.concatenate([kc_ref[0, r * n_slc:r * n_slc + nj, :] for r in range(CMP_PER_SLC)], axis=0)
        s_c = _dot(kc, qt)
        row = lax.broadcasted_iota(jnp.int32, (rows, 1), 0)
        n_idx = CMP_PER_SLC * (row % nj) + row // nj
        first_q = (n_idx * CMP_STRIDE + (CMP_LEN - 1) - t0).astype(F32)
        s_c = jnp.where(first_q <= q_lane, s_c, NEG_INF)
        m_c = jnp.max(s_c, axis=0, keepdims=True)
        e_c = jnp.exp2(s_c - m_c)
        inv_c = jnp.where(m_c > 0.5 * NEG_INF, 1.0 / jnp.maximum(jnp.sum(e_c, axis=0, keepdims=True), 1e-30), 0.0)
        e_b = e_c.astype(BF16)
        if nj < n_slc:
            gap = jnp.zeros((n_slc - nj, qw), BF16)
            e_b = jnp.concatenate(
                [piece for r in range(CMP_PER_SLC) for piece in (e_b[r * nj:(r + 1) * nj], gap)], axis=0)
        oc_ref[...] = _dot(vct_ref[0], e_b) * inv_c

        p_c = e_c * inv_c
        psum = p_c[:, 0:Q_TILE]
        for g in range(1, Q_PER_KV):
            psum = psum + p_c[:, g * Q_TILE:(g + 1) * Q_TILE]
        j_idx = lax.broadcasted_iota(jnp.int32, (nj, 1), 0)
        straddle = psum[3 * nj:4 * nj]
        from_prev = jnp.where(j_idx == 0, 0.0, pltpu.roll(straddle, 1, axis=0))
        imp = (psum[0:nj] + psum[nj:2 * nj] + psum[2 * nj:3 * nj]
               + 0.5 * straddle + 0.5 * from_prev)

        span = WINDOW + Q_TILE
        w0 = pl.multiple_of(jnp.maximum(t0 - WINDOW, 0), Q_TILE)
        rhs_w = jnp.concatenate([qt, jnp.zeros((LANES - HEAD_DIM, qw), BF16)], axis=0)
        wb_off = pl.multiple_of(WINDOW - (t0 - w0), Q_TILE)
        s_w = _dot(kw_ref[0, pl.ds(w0, span), :], rhs_w) + lanes4(wb_ref[pl.ds(wb_off, span), :])
        e_w = jnp.exp2(s_w - jnp.max(s_w, axis=0, keepdims=True)).astype(BF16)
        acc_w = _dot(vwt_ref[0, :, pl.ds(w0, span)], e_w)
        ow_ref[...] = acc_w[:HEAD_DIM] * (1.0 / jnp.maximum(acc_w[HEAD_DIM:HEAD_DIM + 1], 1e-30))

        forced = (j_idx == 0) | (j_idx == tq // SLC_LEN)
        valid = (j_idx * SLC_LEN) <= tq
        score = jnp.where(valid, jnp.where(forced, FORCE_SCORE, imp), NEG_INF)
        j_f = j_idx.astype(F32)
        for _ in range(min(topk, nj)):
            best = jnp.max(score, axis=0, keepdims=True)
            first = jnp.min(jnp.where(score == best, j_f, float(nj)), axis=0, keepdims=True)
            score = jnp.where(j_f == first, -jnp.inf, score)
        tiles = nj // blocks_per_tile
        bias = jnp.where(valid & (score == -jnp.inf), 0.0, NEG_INF)
        bias_ref[0:tiles, 0:blocks_per_tile, :] = bias.reshape(tiles, blocks_per_tile, Q_TILE)
        if blocks_per_tile < BF16_ROWS:
            bias_ref[0:tiles, blocks_per_tile:, :] = jnp.zeros((tiles, BF16_ROWS - blocks_per_tile, Q_TILE), F32)

    j_vis = (t0 + Q_TILE) // SLC_LEN
    step = n_slc // FRONT_CLASSES
    for k in range(FRONT_CLASSES):
        pl.when((j_vis > k * step) & (j_vis <= (k + 1) * step))(functools.partial(front, (k + 1) * step))

    zeros_tail = jnp.zeros((LANES - HEAD_DIM - BF16_ROWS, qw), BF16)

    def scores(slot, j):
        k0 = pl.multiple_of(j * KV_TILE, KV_TILE)
        rhs = jnp.concatenate([qt, lanes4(bias_ref[j]).astype(BF16), zeros_tail], axis=0)
        s_ref[slot] = _dot(ks_ref[0, pl.ds(k0, KV_TILE), :], rhs)

    def accumulate(slot, j, causal=False):
        k0 = pl.multiple_of(j * KV_TILE, KV_TILE)
        v_t = vst_ref[0, :, pl.ds(k0, KV_TILE)]
        half = qw // LANE_HALVES
        for h in range(LANE_HALVES):
            ln = slice(h * half, (h + 1) * half)
            s = s_ref[slot, :, ln]
            if causal:
                tok = k0 + lax.broadcasted_iota(jnp.int32, (KV_TILE, 1), 0)
                s = jnp.where(tok <= t_lane[:, ln], s, NEG_INF)
            m_old = m_ref[:, ln]
            m_new = jnp.maximum(m_old, jnp.max(s, axis=0, keepdims=True))
            p = jnp.exp2(s - m_new).astype(BF16)
            acc_ref[:, ln] = jnp.exp2(m_old - m_new) * acc_ref[:, ln] + _dot(v_t, p)
            m_ref[:, ln] = m_new

    m_ref[...] = jnp.full((1, qw), NEG_INF, F32)
    acc_ref[...] = jnp.zeros((V_ROWS, qw), F32)
    n_full = t0 // KV_TILE
    scores(0, 0)

    def pair(j):
        scores(1, j + 1)
        accumulate(0, j)
        scores(0, j + 2)
        accumulate(1, j + 1)

    def two_pairs(i, _):
        pair(4 * i)
        pair(4 * i + 2)
        return 0

    n_pairs = n_full // 2
    lax.fori_loop(0, n_pairs // 2, two_pairs, 0)

    @pl.when(n_pairs % 2 == 1)
    def _():
        pair(2 * (n_pairs - 1))

    odd = n_full % 2
    def last_tile(slot):
        accumulate(slot, n_full, causal=True)

    @pl.when(odd == 1)
    def _():
        scores(1, n_full)
        accumulate(0, n_full - 1)
        last_tile(1)

    @pl.when(odd == 0)
    def _():
        last_tile(0)

    acc_s = acc_ref[...]
    o_s = acc_s[:HEAD_DIM] * (1.0 / jnp.maximum(acc_s[HEAD_DIM:HEAD_DIM + 1], 1e-30))

    gates = jax.nn.sigmoid(ng_ref[...])

    def gate(b):
        return jnp.concatenate([gates[b * Q_PER_KV + g:b * Q_PER_KV + g + 1] for g in range(Q_PER_KV)], axis=1)

    o = gate(0) * oc_ref[...] + gate(1) * o_s + gate(2) * ow_ref[...]
    pairs = []
    for g in range(0, Q_PER_KV, 2):
        two = jnp.concatenate([o[:, g * Q_TILE:(g + 1) * Q_TILE], o[:, (g + 1) * Q_TILE:(g + 2) * Q_TILE]], axis=0)
        pairs.append(two.T)
    out_ref[...] = jnp.concatenate(pairs, axis=1).astype(out_ref.dtype)


def _window_bias_table():
    r = jnp.arange(2 * WINDOW + Q_TILE, dtype=jnp.int32)[:, None]
    q = jnp.arange(Q_TILE, dtype=jnp.int32)[None, :]
    return jnp.where((r > q) & (r <= WINDOW + q), 0.0, NEG_INF).astype(F32)


def _attention(qt, kc, vct, ks, vst, kw, vwt, ngt, seq):
    wb = _window_bias_table()
    n_slc = seq // SLC_LEN
    nb = seq // Q_TILE
    qw = Q_PER_KV * Q_TILE
    n_rows = CMP_PER_SLC * n_slc
    topk = min(SLC_TOPK, n_slc)
    assert (n_slc // FRONT_CLASSES) % (KV_TILE // SLC_LEN) == 0

    def per_head(block):
        return pl.BlockSpec(block, lambda k, c: (k, 0, 0), pipeline_mode=pl.Buffered(1))

    return pl.pallas_call(
        functools.partial(_attn_kernel, n_slc=n_slc, topk=topk),
        grid=(N_KV_HEADS, nb),
        in_specs=[
            pl.BlockSpec((Q_PER_KV * HEAD_DIM, Q_TILE), lambda k, c: (k, c)),
            per_head((1, n_rows, HEAD_DIM)),
            per_head((1, HEAD_DIM, n_rows)),
            per_head((1, seq, LANES)),
            per_head((1, V_ROWS, seq)),
            per_head((1, seq, LANES)),
            per_head((1, V_ROWS, seq)),
            pl.BlockSpec((GATE_ROWS, Q_TILE), lambda k, c: (k, c)),
            pl.BlockSpec(wb.shape, lambda k, c: (0, 0), pipeline_mode=pl.Buffered(1)),
        ],
        out_specs=pl.BlockSpec((Q_TILE, Q_PER_KV * HEAD_DIM), lambda k, c: (c, k)),
        out_shape=jax.ShapeDtypeStruct((seq, N_HEADS * HEAD_DIM), BF16),
        scratch_shapes=[
            pltpu.VMEM((n_slc * SLC_LEN // KV_TILE, BF16_ROWS, Q_TILE), F32),
            pltpu.VMEM((SCORE_SLOTS, KV_TILE, qw), F32),
            pltpu.VMEM((V_ROWS, qw), F32),
            pltpu.VMEM((1, qw), F32),
            pltpu.VMEM((HEAD_DIM, qw), F32),
            pltpu.VMEM((HEAD_DIM, qw), F32),
        ],
        compiler_params=pltpu.CompilerParams(
            dimension_semantics=("arbitrary", "arbitrary"), vmem_limit_bytes=VMEM_LIMIT_BYTES),
        name="nsa_attention",
    )(qt, kc, vct, ks, vst, kw, vwt, ngt, wb)


def _discretize_kernel(are_ref, aim_ref, ldt_ref, bre_ref, bim_ref, ar_ref, ai_ref, br_ref, bi_ref):
    a_re, a_im = are_ref[...], aim_ref[...]
    dt = jnp.exp(ldt_ref[...])
    mag = jnp.exp(a_re * dt)
    ang = a_im * dt
    ab_re, ab_im = mag * jnp.cos(ang), mag * jnp.sin(ang)
    n_re, n_im = ab_re - 1.0, ab_im
    den = a_re * a_re + a_im * a_im
    c_re = (n_re * a_re + n_im * a_im) / den
    c_im = (n_im * a_re - n_re * a_im) / den
    b_re, b_im = bre_ref[...], bim_ref[...]
    ar_ref[...] = ab_re
    ai_ref[...] = ab_im
    br_ref[...] = c_re * b_re - c_im * b_im
    bi_ref[...] = c_re * b_im + c_im * b_re


def _discretize(a_re, a_im, log_dt, b_re, b_im):
    g, p, h = b_re.shape
    col = jax.ShapeDtypeStruct((g, p, 1), F32)
    full = jax.ShapeDtypeStruct((g, p, h), F32)
    return pl.pallas_call(
        _discretize_kernel, out_shape=[col, col, full, full], name="ssm_discretize",
    )(a_re[..., None], a_im[..., None], log_dt[:, None, None], b_re, b_im)


def _cmul(ar, ai, br, bi):
    return ar * br - ai * bi, ar * bi + ai * br


def _s5_kernel(u_ref, bblk_ref, a_ref, cblk_ref, d_ref, wglu_ref, y_ref, xr_ref, xi_ref, st_ref, pw_ref):
    seg = SCAN_SEGMENTS
    steps = xr_ref.shape[0]
    n_state = xr_ref.shape[2]
    n_blk, blk_in, blk_state2 = bblk_ref.shape
    blk_state = blk_state2 // 2

    @pl.when(pl.program_id(0) == 0)
    def _():
        st_ref[...] = jnp.zeros_like(st_ref)
        wr, wi = a_ref[0:1, :], a_ref[1:2, :]
        for j in range(steps):
            pw_ref[j, 0] = jnp.broadcast_to(wr, (seg, n_state))
            pw_ref[j, 1] = jnp.broadcast_to(wi, (seg, n_state))
            wr, wi = _cmul(wr, wi, a_ref[0:1, :], a_ref[1:2, :])

    u = jnp.concatenate(
        [jnp.concatenate([u_ref[k, pl.ds(j, seg, stride=steps), :] for j in range(steps)], axis=0)
         for k in range(u_ref.shape[0])], axis=1)
    ub = u.astype(BF16)
    for k in range(n_blk):
        bu = _dot(ub[:, k * blk_in:(k + 1) * blk_in], bblk_ref[k])
        xr_ref[:, :, k * blk_state:(k + 1) * blk_state] = bu[:, :blk_state].reshape(steps, seg, blk_state)
        xi_ref[:, :, k * blk_state:(k + 1) * blk_state] = bu[:, blk_state:].reshape(steps, seg, blk_state)

    lb = SCAN_LANE_BLOCK
    for b0 in range(0, n_state, lb):
        lanes = slice(b0, b0 + lb)
        a_re = a_ref[0:1, lanes]
        a_im = a_ref[1:2, lanes]
        ar8 = jnp.broadcast_to(a_re, (seg, lb))
        ai8 = jnp.broadcast_to(a_im, (seg, lb))

        def local(j, carry):
            xr, xi = carry
            pr, pi = _cmul(ar8, ai8, xr, xi)
            xr = pr + xr_ref[j, :, lanes]
            xi = pi + xi_ref[j, :, lanes]
            xr_ref[j, :, lanes] = xr
            xi_ref[j, :, lanes] = xi
            return xr, xi

        end_r, end_i = lax.fori_loop(0, steps, local, (jnp.zeros((seg, lb), F32), jnp.zeros((seg, lb), F32)))

        pr, pi = a_re, a_im
        for _ in range(steps.bit_length() - 1):
            pr, pi = _cmul(pr, pi, pr, pi)
        cr, ci = st_ref[0:1, lanes], st_ref[1:2, lanes]
        in_r, in_i = [], []
        for s in range(seg):
            in_r.append(cr)
            in_i.append(ci)
            qr, qi = _cmul(pr, pi, cr, ci)
            cr, ci = qr + end_r[s:s + 1], qi + end_i[s:s + 1]
        st_ref[0:1, lanes] = cr
        st_ref[1:2, lanes] = ci
        in_r = jnp.concatenate(in_r, axis=0)
        in_i = jnp.concatenate(in_i, axis=0)

        def fix(j, carry):
            fr, fi = _cmul(pw_ref[j, 0, :, lanes], pw_ref[j, 1, :, lanes], in_r, in_i)
            xr_ref[j, :, lanes] = xr_ref[j, :, lanes] + fr
            xi_ref[j, :, lanes] = xi_ref[j, :, lanes] + fi
            return carry

        lax.fori_loop(0, steps, fix, 0)

    rows = steps * seg
    y = d_ref[...] * u
    for k in range(n_blk):
        st = slice(k * blk_state, (k + 1) * blk_state)
        xs = jnp.concatenate([xr_ref[:, :, st].reshape(rows, blk_state),
                              xi_ref[:, :, st].reshape(rows, blk_state)], axis=1).astype(BF16)
        yk = _dot(xs, cblk_ref[k])
        if k == 0:
            ys = [yk]
        else:
            ys.append(yk)
    y = y + jnp.concatenate(ys, axis=1)
    y = jax.nn.gelu(y)
    y = y * jax.nn.sigmoid(_dot(y.astype(BF16), wglu_ref[...]))
    y_ref[...] = y.astype(y_ref.dtype)


def _s5(u_slabs, bblk, a_lane, cblk, d_row, w_glu):
    n_slab, seq, _ = u_slabs.shape
    width = n_slab * LANES
    tm = ROW_TILE
    steps = tm // SCAN_SEGMENTS
    n_state = a_lane.shape[1]
    const2 = lambda i: (0, 0)
    const3 = lambda i: (0, 0, 0)
    return pl.pallas_call(
        _s5_kernel,
        grid=(seq // tm,),
        in_specs=[
            pl.BlockSpec((n_slab, tm, LANES), lambda i: (0, i, 0)),
            pl.BlockSpec(bblk.shape, const3),
            pl.BlockSpec(a_lane.shape, const2),
            pl.BlockSpec(cblk.shape, const3),
            pl.BlockSpec((1, width), const2),
            pl.BlockSpec((width, width), const2),
        ],
        out_specs=pl.BlockSpec((tm, width), lambda i: (i, 0)),
        out_shape=jax.ShapeDtypeStruct((seq, width), BF16),
        scratch_shapes=[
            pltpu.VMEM((steps, SCAN_SEGMENTS, n_state), F32),
            pltpu.VMEM((steps, SCAN_SEGMENTS, n_state), F32),
            pltpu.VMEM((2, n_state), F32),
            pltpu.VMEM((steps, 2, SCAN_SEGMENTS, n_state), F32),
        ],
        compiler_params=pltpu.CompilerParams(
            dimension_semantics=("arbitrary",), vmem_limit_bytes=VMEM_LIMIT_BYTES),
        name="s5_scan",
    )(u_slabs, bblk, a_lane, cblk, d_row, w_glu)


def _ffn_kernel(x_ref, ya_ref, yb_ref, gmix_ref, wg_ref, wpa_ref, wpb_ref, wout_ref, gffn_ref,
                wup_ref, cw_ref, cb_ref, wdown_ref, gfin_ref, o_ref, tail_ref, *, d_ff, chunk):
    tm, d_model = x_ref.shape

    @pl.when(pl.program_id(0) == 0)
    def _():
        tail_ref[...] = jnp.zeros_like(tail_ref)

    x = x_ref[...]
    h = _rmsnorm(x, gmix_ref[...]).astype(BF16)
    gates = jax.nn.sigmoid(_dot(h, wg_ref[...]))
    merged = (gates[:, :d_model] * _dot(ya_ref[...], wpa_ref[...])
              + gates[:, d_model:] * _dot(yb_ref[...], wpb_ref[...]))
    x1 = x + _dot(merged.astype(BF16), wout_ref[...])
    h2 = _rmsnorm(x1, gffn_ref[...]).astype(BF16)

    def conv_cols(c0):
        cols = slice(c0, c0 + chunk)
        up = _dot(h2, wup_ref[:, cols])
        ext = jnp.concatenate([tail_ref[:, cols], up], axis=0)
        tail_ref[:, cols] = up[tm - SUBLANES:, :]
        prev1 = pltpu.roll(ext, 1, axis=0)[SUBLANES:]
        prev2 = pltpu.roll(ext, 2, axis=0)[SUBLANES:]
        return cb_ref[:, cols] + cw_ref[0:1, cols] * prev2 + cw_ref[1:2, cols] * prev1 + cw_ref[2:3, cols] * up

    acc = x1
    for k in range(d_ff // chunk):
        gate = conv_cols(k * chunk)
        val = conv_cols(d_ff + k * chunk)
        act = (jax.nn.silu(gate) * val).astype(BF16)
        acc = acc + _dot(act, wdown_ref[k * chunk:(k + 1) * chunk, :])
    o_ref[...] = _rmsnorm(acc, gfin_ref[...])


def _ffn(x2, ya, yb, g_mix, w_g, w_pa, w_pb, w_out, g_ffn, w_up, conv_w, conv_b, w_down, g_fin):
    seq, d_model = x2.shape
    d_ff = w_down.shape[0]
    tm = FFN_ROW_TILE
    chunk = FFN_COL_CHUNK
    assert d_ff % chunk == 0 and chunk % LANES == 0
    row = lambda i: (i, 0)

    def resident(arr):
        return pl.BlockSpec(arr.shape, lambda i: (0, 0), pipeline_mode=pl.Buffered(1))

    return pl.pallas_call(
        functools.partial(_ffn_kernel, d_ff=d_ff, chunk=chunk),
        grid=(seq // tm,),
        in_specs=[
            pl.BlockSpec((tm, d_model), row),
            pl.BlockSpec((tm, ya.shape[1]), row),
            pl.BlockSpec((tm, yb.shape[1]), row),
            resident(g_mix), resident(w_g), resident(w_pa), resident(w_pb), resident(w_out), resident(g_ffn),
            resident(w_up), resident(conv_w), resident(conv_b), resident(w_down), resident(g_fin),
        ],
        out_specs=pl.BlockSpec((tm, d_model), row),
        out_shape=jax.ShapeDtypeStruct((seq, d_model), F32),
        scratch_shapes=[pltpu.VMEM((SUBLANES, 2 * d_ff), F32)],
        compiler_params=pltpu.CompilerParams(
            dimension_semantics=("arbitrary",), vmem_limit_bytes=VMEM_LIMIT_BYTES),
        name="merge_ffn",
    )(x2, ya, yb, g_mix, w_g, w_pa, w_pb, w_out, g_ffn, w_up, conv_w, conv_b, w_down, g_fin)


def _block_diag(m, n_blk):
    g, r, c = m.shape
    per = g // n_blk
    eye = jnp.eye(per, dtype=m.dtype)
    m = m.reshape(n_blk, per, r, c)
    return jnp.einsum("bgrc,gk->bgrkc", m, eye).reshape(n_blk, per * r, per * c)


def kernel(x, positions, norm_mix, w_in, cmp_k_pe, cmp_k_w1, cmp_k_w2, cmp_v_pe, cmp_v_w1, cmp_v_w2, ssm_a_re, ssm_a_im, ssm_b_re, ssm_b_im, ssm_c_re, ssm_c_im, ssm_d, ssm_log_dt, ssm_w_glu, w_proj_a, w_proj_b, w_out, norm_ffn, w_up, conv_w, conv_b, w_down, norm_final):
    bsz, seq, d_model = x.shape
    assert bsz == 1 and norm_mix.shape[0] == 1
    assert seq % max(KV_TILE, ROW_TILE) == 0 and KV_TILE // SLC_LEN <= BF16_ROWS and seq >= WINDOW + Q_TILE
    attn_w = N_HEADS * HEAD_DIM
    kv_w = N_KV_HEADS * HEAD_DIM
    ssm_width = ssm_d.shape[1]
    x2 = x[0]
    pos = positions[0]

    sizes = (attn_w,) + (kv_w,) * 6 + (3 * N_HEADS, ssm_width, d_model, d_model)
    offs = [0]
    for s in sizes:
        offs.append(offs[-1] + s)
    w = w_in[0]
    col = lambda i: w[:, offs[i]:offs[i + 1]]
    w_q, w_kc, w_vc, w_ks, w_vs, w_kw, w_vw, w_ng, w_u, w_ga, w_gb = [col(i) for i in range(11)]
    w_ng = w_ng.reshape(d_model, N_KV_HEADS, Q_PER_KV, 3).transpose(0, 1, 3, 2).reshape(d_model, N_KV_HEADS, 3 * Q_PER_KV)
    w_ng = jnp.pad(w_ng, ((0, 0), (0, 0), (0, GATE_ROWS - 3 * Q_PER_KV))).reshape(d_model, N_KV_HEADS * GATE_ROWS)
    w_t = jnp.concatenate([w_q, w_ks, w_kw, w_vs, w_vw, w_ng], axis=1).T.astype(BF16)
    w_nat = jnp.concatenate([w_kc, w_vc, w_u], axis=1).astype(BF16)
    w_gate = jnp.concatenate([w_ga, w_gb], axis=1).astype(BF16)

    inv_freq = ROPE_THETA ** (-jnp.arange(0, ROT_DIM, 2, dtype=F32) / ROT_DIM)
    qt, ks, kw, vst, vwt, ngt, kvc, u = _inproj(x2, norm_mix, pos[None, :], inv_freq[:, None], w_nat, w_t)

    n_half = seq // CMP_STRIDE
    n_cmp = (seq - CMP_LEN) // CMP_STRIDE + 1
    h4 = kvc.reshape(seq, 2 * N_KV_HEADS, HEAD_DIM).transpose(1, 0, 2).reshape(
        2 * N_KV_HEADS, n_half, CMP_STRIDE * HEAD_DIM)
    halves = lambda pe: pe.reshape(2, CMP_STRIDE * HEAD_DIM)
    pe4 = jnp.stack([halves(cmp_k_pe[0])] * N_KV_HEADS + [halves(cmp_v_pe[0])] * N_KV_HEADS)
    w1 = jnp.stack([cmp_k_w1[0], cmp_v_w1[0]]).astype(BF16)
    w2 = jnp.pad(jnp.stack([cmp_k_w2[0], cmp_v_w2[0]]), ((0, 0), (0, 0), (0, LANES - HEAD_DIM))).astype(BF16)
    cmp_end = jnp.arange(n_cmp, dtype=jnp.int32) * CMP_STRIDE + CMP_LEN - 1
    cmp_pos = jnp.pad(pos[cmp_end], (0, n_half - n_cmp))
    pat = _rope_lane_patterns()
    pat2 = jnp.stack([pat, jnp.zeros_like(pat)])
    cmp = _compress(h4, pe4, w1, w2, cmp_pos[:, None], pat2)[:, :, :HEAD_DIM]
    n_slc = seq // SLC_LEN
    cmp = cmp.reshape(2 * N_KV_HEADS, n_slc, CMP_PER_SLC, HEAD_DIM).transpose(0, 2, 1, 3).reshape(
        2 * N_KV_HEADS, n_half, HEAD_DIM).astype(BF16)
    kc = cmp[:N_KV_HEADS]
    vct = cmp[N_KV_HEADS:].transpose(0, 2, 1)

    y_a = _attention(qt, kc, vct, ks, vst, kw, vwt, ngt, seq)

    ab_re, ab_im, bb_re, bb_im = _discretize(ssm_a_re[0], ssm_a_im[0], ssm_log_dt[0], ssm_b_re[0], ssm_b_im[0])
    n_blk = 2
    a_lane = jnp.stack([ab_re.reshape(-1), ab_im.reshape(-1)])
    to_in = lambda b: _block_diag(b.transpose(0, 2, 1), n_blk)
    bblk = jnp.concatenate([to_in(bb_re), to_in(bb_im)], axis=2).astype(BF16)
    to_out = lambda cm: _block_diag(cm.transpose(0, 2, 1), n_blk)
    cblk = jnp.concatenate([to_out(ssm_c_re[0]), -to_out(ssm_c_im[0])], axis=1).astype(BF16)
    steps = ROW_TILE // SCAN_SEGMENTS
    nt = seq // ROW_TILE
    y_b = _s5(u, bblk, a_lane, cblk, ssm_d, ssm_w_glu[0].astype(BF16))
    y_b = y_b.reshape(nt, steps, SCAN_SEGMENTS, ssm_width).transpose(0, 2, 1, 3).reshape(seq, ssm_width)

    out = _ffn(x2, y_a, y_b, norm_mix, w_gate, w_proj_a[0].astype(BF16), w_proj_b[0].astype(BF16),
               w_out[0].astype(BF16), norm_ffn, w_up[0].astype(BF16), conv_w[0], conv_b, w_down[0].astype(BF16),
               norm_final[None, :])
    return out[None]
```
